```python
import jax, jax.numpy as jnp
from jax import lax
import numpy as np

D_MODEL = 4096
BATCH = 2
SEQ = 4096
DEPTH = 2

HG_HEADS = 16
HG_HEAD_DIM = 128
HG_WIDTH = HG_HEADS * HG_HEAD_DIM
HG_CHUNK = 64
NSA_HEADS = 16
NSA_KV_GROUPS = 4
NSA_HEAD_DIM = 128
NSA_WIDTH = NSA_HEADS * NSA_HEAD_DIM
NSA_KV_WIDTH = NSA_KV_GROUPS * NSA_HEAD_DIM
CMP_BLOCK = 32
CMP_STRIDE = 16
CMP_HIDDEN = 256
SLC_BLOCK = 64
SLC_TOPK = 16
SLC_QBLOCK = 32
WIN_SIZE = 512
WIN_QBLOCK = 128
RET_HEADS = 8
RET_QK_DIM = 128
RET_V_DIM = 256
RET_QK_WIDTH = RET_HEADS * RET_QK_DIM
RET_V_WIDTH = RET_HEADS * RET_V_DIM
RET_CHUNK = 128
MIX_WIDTH = HG_WIDTH + NSA_WIDTH + RET_V_WIDTH
D_FF = 4 * D_MODEL
NORM_EPS = 1e-6
IN_SPLITS = ((HG_WIDTH,) * 4
             + (NSA_WIDTH,) + (NSA_KV_WIDTH,) * 6 + (NSA_HEADS * 3,)
             + (RET_QK_WIDTH, RET_QK_WIDTH, RET_V_WIDTH, RET_V_WIDTH)
             + (D_MODEL,) * 3)
N_IN = sum(IN_SPLITS)

kernel_name = "hybrid_hgrn2_nsa_retention_block"


def rmsnorm(x, w):
    x32 = x.astype(jnp.float32)
    y = x32 * lax.rsqrt(jnp.mean(x32 * x32, axis=-1, keepdims=True) + NORM_EPS)
    return (y * w.astype(jnp.float32)).astype(x.dtype)


def masked_softmax(s, valid):
    s = jnp.where(valid, s.astype(jnp.float32), -jnp.inf)
    m = jnp.max(s, axis=-1, keepdims=True)
    m = jnp.where(jnp.isfinite(m), m, 0.0)
    e = jnp.exp(s - m)
    den = jnp.sum(e, axis=-1, keepdims=True)
    return e / jnp.where(den > 0, den, 1.0)


def split_cols(z, sizes):
    offs = np.cumsum(sizes)[:-1].tolist()
    return jnp.split(z, offs, axis=-1)


def alibi_slopes(n):
    return jnp.exp2(-8.0 * jnp.arange(1, n + 1, dtype=jnp.float32) / n)


def gla_chunk_scan(q, k, v, lg, chunk):
    B, H, T, dk = q.shape
    dv = v.shape[-1]
    n = T // chunk

    def split(a):
        return a.reshape(B, H, n, chunk, a.shape[-1]).transpose(2, 0, 1, 3, 4)

    causal = jnp.tril(jnp.ones((chunk, chunk), dtype=bool))[:, :, None]

    def step(S, inp):
        qc, kc, vc, gc = inp
        b = jnp.cumsum(gc, axis=2)
        inter = jnp.einsum('bhik,bhkv->bhiv', qc * jnp.exp(b), S)
        diff = b[:, :, :, None, :] - b[:, :, None, :, :]
        decay = jnp.exp(jnp.where(causal, diff, -jnp.inf))
        att = jnp.einsum('bhik,bhijk,bhjk->bhij', qc, decay, kc)
        intra = jnp.einsum('bhij,bhjv->bhiv', att, vc)
        b_last = b[:, :, -1:, :]
        S = (jnp.exp(b_last[:, :, 0, :, None]) * S
             + jnp.einsum('bhjk,bhjv->bhkv', kc * jnp.exp(b_last - b), vc))
        return S, inter + intra

    S0 = jnp.zeros((B, H, dk, dv), jnp.float32)
    _, ys = lax.scan(step, S0, (split(q), split(k), split(v), split(lg)))
    return ys.transpose(1, 2, 0, 3, 4).reshape(B, H, T, dv)


def hgrn2_mixer(q_in, f_in, i_in, g_in, lower_bound, norm_w):
    B, T, _ = q_in.shape

    def heads(a):
        return a.reshape(B, T, HG_HEADS, HG_HEAD_DIM).transpose(0, 2, 1, 3).astype(jnp.float32)

    q = jax.nn.silu(heads(q_in)) * HG_HEAD_DIM ** -0.5
    lb = lower_bound.astype(jnp.float32).reshape(HG_HEADS, 1, HG_HEAD_DIM)
    f = lb + (1.0 - lb) * jax.nn.sigmoid(heads(f_in))
    k = 1.0 - f
    o = gla_chunk_scan(q, k, heads(i_in), jnp.log(f), HG_CHUNK)
    o = o.transpose(0, 2, 1, 3)
    g = g_in.reshape(B, T, HG_HEADS, HG_HEAD_DIM).astype(jnp.float32)
    o = (o * lax.rsqrt(jnp.mean(o * o, axis=-1, keepdims=True) + NORM_EPS)
         * norm_w.astype(jnp.float32) * jax.nn.silu(g))
    return o.reshape(B, T, HG_WIDTH).astype(q_in.dtype)


def compress_blocks(kv, pe, w1, w2):
    B, G, T, d = kv.shape
    n_cmp = (T - CMP_BLOCK) // CMP_STRIDE + 1
    idx = jnp.arange(n_cmp)[:, None] * CMP_STRIDE + jnp.arange(CMP_BLOCK)[None, :]
    blocks = kv[:, :, idx, :] + pe
    h = jax.nn.silu(jnp.einsum('bgnf,fh->bgnh', blocks.reshape(B, G, n_cmp, CMP_BLOCK * d), w1))
    return jnp.einsum('bgnh,hd->bgnd', h, w2)


def select_blocks(imp_cmp, T):
    n_cmp = imp_cmp.shape[-1]
    n_slc = T // SLC_BLOCK
    c_start = jnp.arange(n_cmp) * CMP_STRIDE
    s_start = jnp.arange(n_slc) * SLC_BLOCK
    overlap = jnp.clip(jnp.minimum(c_start[:, None] + CMP_BLOCK, s_start[None, :] + SLC_BLOCK)
                       - jnp.maximum(c_start[:, None], s_start[None, :]), 0)
    w_map = overlap.astype(jnp.float32) / CMP_STRIDE
    imp = jnp.einsum('bgtn,ns->bgts', imp_cmp, w_map)
    cur = (jnp.arange(T) // SLC_BLOCK)[:, None]
    blk = jnp.arange(n_slc)[None, :]
    forced = (blk == 0) | (blk == cur) | (blk == cur - 1)
    score = jnp.where(blk > cur, -jnp.inf, jnp.where(forced, jnp.inf, imp))
    _, idx = lax.top_k(score, min(SLC_TOPK, n_slc))
    return idx


def selected_attention(q, k, v, sel_idx, slopes):
    B, G, R, T, d = q.shape
    n_slc = T // SLC_BLOCK
    nsel = sel_idx.shape[-1]
    kb = k.reshape(B, G, n_slc, SLC_BLOCK, d)
    vb = v.reshape(B, G, n_slc, SLC_BLOCK, d)
    nq = T // SLC_QBLOCK
    q_blocks = q.reshape(B, G, R, nq, SLC_QBLOCK, d).transpose(3, 0, 1, 2, 4, 5)
    i_blocks = sel_idx.reshape(B, G, nq, SLC_QBLOCK, nsel).transpose(2, 0, 1, 3, 4)
    t_blocks = jnp.arange(T).reshape(nq, SLC_QBLOCK)
    bi = jnp.arange(B)[:, None, None, None]
    gi = jnp.arange(G)[None, :, None, None]
    offs = jnp.arange(SLC_BLOCK)
    scale = d ** -0.5

    def attend(args):
        qb, ib, tb = args
        kg = kb[bi, gi, ib]
        vg = vb[bi, gi, ib]
        pos = ib[..., None] * SLC_BLOCK + offs
        dist = (tb[None, None, :, None, None] - pos)[:, :, None]
        s = (jnp.einsum('bgrqd,bgqnld->bgrqnl', qb, kg).astype(jnp.float32) * scale
             - slopes[None, :, :, None, None, None] * dist.astype(jnp.float32))
        Q = qb.shape[3]
        p = masked_softmax(s.reshape(B, G, R, Q, nsel * SLC_BLOCK),
                           (dist >= 0).reshape(B, G, 1, Q, nsel * SLC_BLOCK))
        p = p.reshape(B, G, R, Q, nsel, SLC_BLOCK)
        return jnp.einsum('bgrqnl,bgqnld->bgrqd', p.astype(vg.dtype), vg)

    o = lax.map(attend, (q_blocks, i_blocks, t_blocks))
    return o.transpose(1, 2, 3, 0, 4, 5).reshape(B, G, R, T, d)


def window_attention(q, k, v, slopes):
    B, G, R, T, d = q.shape
    nb = T // WIN_QBLOCK
    span = WIN_SIZE + WIN_QBLOCK
    kp = jnp.pad(k, ((0, 0), (0, 0), (WIN_SIZE, 0), (0, 0)))
    vp = jnp.pad(v, ((0, 0), (0, 0), (WIN_SIZE, 0), (0, 0)))
    idx = jnp.arange(nb)[:, None] * WIN_QBLOCK + jnp.arange(span)[None, :]
    kb = kp[:, :, idx]
    vb = vp[:, :, idx]
    qb = q.reshape(B, G, R, nb, WIN_QBLOCK, d)
    tq = jnp.arange(nb)[:, None] * WIN_QBLOCK + jnp.arange(WIN_QBLOCK)[None, :]
    tk = idx - WIN_SIZE
    dist = tq[:, :, None] - tk[:, None, :]
    valid = (dist >= 0) & (dist < WIN_SIZE) & (tk[:, None, :] >= 0)
    s = (jnp.einsum('bgrcqd,bgckd->bgrcqk', qb, kb).astype(jnp.float32) * d ** -0.5
         - slopes[None, :, :, None, None, None] * dist.astype(jnp.float32))
    p = masked_softmax(s, valid)
    o = jnp.einsum('bgrcqk,bgckd->bgrcqd', p.astype(vb.dtype), vb)
    return o.reshape(B, G, R, T, d)


def nsa_mixer(q_in, kc_in, vc_in, ks_in, vs_in, kw_in, vw_in, gate_in,
              pe_k, pe_v, w1_k, w1_v, w2_k, w2_v):
    B, T, _ = q_in.shape
    G = NSA_KV_GROUPS
    R = NSA_HEADS // NSA_KV_GROUPS
    d = NSA_HEAD_DIM
    q = q_in.reshape(B, T, G, R, d).transpose(0, 2, 3, 1, 4)

    def kv(a):
        return a.reshape(B, T, G, d).transpose(0, 2, 1, 3)

    slopes = alibi_slopes(NSA_HEADS).reshape(G, R)
    tq = jnp.arange(T)
    kc = compress_blocks(kv(kc_in), pe_k, w1_k, w2_k)
    vc = compress_blocks(kv(vc_in), pe_v, w1_v, w2_v)
    n_cmp = kc.shape[2]
    c_end = jnp.arange(n_cmp) * CMP_STRIDE + CMP_BLOCK - 1
    dist = tq[:, None] - c_end[None, :]
    s = (jnp.einsum('bgrtd,bgnd->bgrtn', q, kc).astype(jnp.float32) * d ** -0.5
         - slopes[:, :, None, None] * dist.astype(jnp.float32))
    p_cmp = masked_softmax(s, dist >= 0)
    o_cmp = jnp.einsum('bgrtn,bgnd->bgrtd', p_cmp.astype(vc.dtype), vc)
    sel_idx = select_blocks(jnp.sum(p_cmp, axis=2), T)
    o_slc = selected_attention(q, kv(ks_in), kv(vs_in), sel_idx, slopes)
    o_win = window_attention(q, kv(kw_in), kv(vw_in), slopes)
    gates = jax.nn.sigmoid(gate_in.astype(jnp.float32)).reshape(B, T, G, R, 3).transpose(0, 2, 3, 1, 4)
    o = (gates[..., 0:1] * o_cmp.astype(jnp.float32) + gates[..., 1:2] * o_slc.astype(jnp.float32)
         + gates[..., 2:3] * o_win.astype(jnp.float32))
    return o.transpose(0, 3, 1, 2, 4).reshape(B, T, NSA_WIDTH).astype(q_in.dtype)


def retention_chunk_scan(q, k, v, log_gamma, chunk):
    B, H, T, dk = q.shape
    dv = v.shape[-1]
    n = T // chunk
    pos = jnp.arange(chunk, dtype=jnp.float32)
    rel = pos[:, None] - pos[None, :]
    lg = log_gamma[:, None, None]
    intra_decay = jnp.where(rel >= 0, jnp.exp(lg * jnp.maximum(rel, 0.0)), 0.0)
    query_decay = jnp.exp(log_gamma[:, None] * (pos + 1.0))[:, :, None]
    key_decay = jnp.exp(log_gamma[:, None] * (chunk - 1.0 - pos))[:, :, None]
    chunk_decay = jnp.exp(log_gamma * chunk)[:, None, None]

    def split(a):
        return a.reshape(B, H, n, chunk, a.shape[-1]).transpose(2, 0, 1, 3, 4)

    def step(S, inp):
        qc, kc, vc = inp
        inter = jnp.einsum('bhik,bhkv->bhiv', qc, S) * query_decay
        scores = jnp.einsum('bhik,bhjk->bhij', qc, kc) * intra_decay
        intra = jnp.einsum('bhij,bhjv->bhiv', scores, vc)
        S = chunk_decay * S + jnp.einsum('bhjk,bhjv->bhkv', kc * key_decay, vc)
        return S, inter + intra

    S0 = jnp.zeros((B, H, dk, dv), jnp.float32)
    _, ys = lax.scan(step, S0, (split(q), split(k), split(v)))
    return ys.transpose(1, 2, 0, 3, 4).reshape(B, H, T, dv)


def retention_mixer(q_in, k_in, v_in, g_in):
    B, T, _ = q_in.shape

    def heads(a, dh):
        return a.reshape(B, T, RET_HEADS, dh).transpose(0, 2, 1, 3).astype(jnp.float32)

    q = heads(q_in, RET_QK_DIM)
    k = heads(k_in, RET_QK_DIM) * RET_QK_DIM ** -0.5
    v = heads(v_in, RET_V_DIM)
    log_gamma = jnp.log(1.0 - jnp.exp2(-5.0 - jnp.arange(RET_HEADS, dtype=jnp.float32)))
    o = retention_chunk_scan(q, k, v, log_gamma, RET_CHUNK).transpose(0, 2, 1, 3)
    mu = jnp.mean(o, axis=-1, keepdims=True)
    oc = o - mu
    o = oc * lax.rsqrt(jnp.mean(oc * oc, axis=-1, keepdims=True) + NORM_EPS)
    g = g_in.reshape(B, T, RET_HEADS, RET_V_DIM).astype(jnp.float32)
    return (o * jax.nn.silu(g)).reshape(B, T, RET_V_WIDTH).astype(q_in.dtype)


def setup_inputs(seed: int = 0) -> dict:
    key = jax.random.key(seed)
    ks = jax.random.split(key, 17)
    f32 = jnp.float32

    def normal(k, shape, scale):
        return jax.random.normal(k, shape, f32) * scale

    L = DEPTH
    d = NSA_HEAD_DIM
    return {
        "x": normal(ks[0], (BATCH, SEQ, D_MODEL), 1.0),
        "norm1_w": 1.0 + normal(ks[1], (L, D_MODEL), 0.02),
        "w_in": normal(ks[2], (L, D_MODEL, N_IN), D_MODEL ** -0.5),
        "hgrn_lb_table": normal(ks[3], (L, HG_WIDTH), 0.5),
        "hgrn_norm_w": 1.0 + normal(ks[4], (L, HG_HEAD_DIM), 0.02),
        "cmp_pe_k": normal(ks[5], (L, CMP_BLOCK, d), 0.1),
        "cmp_pe_v": normal(ks[6], (L, CMP_BLOCK, d), 0.1),
        "cmp_w1_k": normal(ks[7], (L, CMP_BLOCK * d, CMP_HIDDEN), (CMP_BLOCK * d) ** -0.5),
        "cmp_w1_v": normal(ks[8], (L, CMP_BLOCK * d, CMP_HIDDEN), (CMP_BLOCK * d) ** -0.5),
        "cmp_w2_k": normal(ks[9], (L, CMP_HIDDEN, d), CMP_HIDDEN ** -0.5),
        "cmp_w2_v": normal(ks[10], (L, CMP_HIDDEN, d), CMP_HIDDEN ** -0.5),
        "w_branch": normal(ks[11], (L, MIX_WIDTH, D_MODEL), HG_WIDTH ** -0.5),
        "w_out": normal(ks[12], (L, D_MODEL, D_MODEL), D_MODEL ** -0.5),
        "norm2_w": 1.0 + normal(ks[13], (L, D_MODEL), 0.02),
        "w_ff1": normal(ks[14], (L, D_MODEL, D_FF), D_MODEL ** -0.5),
        "w_ff2": normal(ks[15], (L, D_FF, D_MODEL), D_FF ** -0.5),
        "final_norm_w": 1.0 + normal(ks[16], (D_MODEL,), 0.02),
    }


def reference(x, norm1_w, w_in, hgrn_lb_table, hgrn_norm_w, cmp_pe_k, cmp_pe_v,
              cmp_w1_k, cmp_w1_v, cmp_w2_k, cmp_w2_v, w_branch, w_out, norm2_w,
              w_ff1, w_ff2, final_norm_w):
    p_lb = jax.nn.softmax(hgrn_lb_table.astype(jnp.float32), axis=0)
    lower_bounds = jnp.cumsum(p_lb, axis=0) - p_lb[0]
    for l in range(DEPTH):
        h = rmsnorm(x, norm1_w[l])
        z = jnp.einsum('btd,dn->btn', h, w_in[l])
        (a_q, a_f, a_i, a_g,
         b_q, b_kc, b_vc, b_ks, b_vs, b_kw, b_vw, b_gate,
         c_q, c_k, c_v, c_g,
         m_a, m_b, m_c) = split_cols(z, IN_SPLITS)
        o_a = hgrn2_mixer(a_q, a_f, a_i, a_g, lower_bounds[l], hgrn_norm_w[l])
        o_b = nsa_mixer(b_q, b_kc, b_vc, b_ks, b_vs, b_kw, b_vw, b_gate,
                        cmp_pe_k[l], cmp_pe_v[l], cmp_w1_k[l], cmp_w1_v[l],
                        cmp_w2_k[l], cmp_w2_v[l])
        o_c = retention_mixer(c_q, c_k, c_v, c_g)
        wb = w_branch[l]
        merged = (jax.nn.sigmoid(m_a) * jnp.einsum('btc,cd->btd', o_a, wb[:HG_WIDTH])
                  + jax.nn.sigmoid(m_b) * jnp.einsum('btc,cd->btd', o_b, wb[HG_WIDTH:HG_WIDTH + NSA_WIDTH])
                  + jax.nn.sigmoid(m_c) * jnp.einsum('btc,cd->btd', o_c, wb[HG_WIDTH + NSA_WIDTH:]))
        x = x + jnp.einsum('btd,de->bte', merged, w_out[l])
        h = rmsnorm(x, norm2_w[l])
        u = jax.nn.relu(jnp.einsum('btd,df->btf', h, w_ff1[l]))
        x = x + jnp.einsum('btf,fd->btd', u * u, w_ff2[l])
    return rmsnorm(x, final_norm_w)
```

```python
import functools

import numpy as np
import jax
import jax.numpy as jnp
from jax import lax
from jax.experimental import pallas as pl
from jax.experimental.pallas import tpu as pltpu

F32 = jnp.float32
BF16 = jnp.bfloat16
HIGHEST = lax.Precision.HIGHEST

D_MODEL = 4096
DEPTH = 2
NORM_EPS = 1e-6
LANE = 128

HG_HEADS = 16
HG_DIM = 128
HG_WIDTH = HG_HEADS * HG_DIM
HG_CHUNK = 64
HG_SUB = 16
NSA_HEADS = 16
NSA_GROUPS = 4
NSA_REP = NSA_HEADS // NSA_GROUPS
NSA_DIM = 128
NSA_WIDTH = NSA_HEADS * NSA_DIM
NSA_KV_WIDTH = NSA_GROUPS * NSA_DIM
CMP_BLOCK = 32
CMP_STRIDE = 16
CMP_HIDDEN = 256
SLC_BLOCK = 64
SLC_SHIFT = 6
SLC_TOPK = 16
WIN_SIZE = 512
RET_HEADS = 8
RET_QK = 128
RET_V = 256
RET_QK_WIDTH = RET_HEADS * RET_QK
RET_V_WIDTH = RET_HEADS * RET_V
RET_CHUNK = 128
MIX_WIDTH = HG_WIDTH + NSA_WIDTH + RET_V_WIDTH
D_FF = 4 * D_MODEL

GATE_COLS = NSA_HEADS * 3
GATE_PAD = 512
OFF_AQ = 0
OFF_AF = OFF_AQ + HG_WIDTH
OFF_AI = OFF_AF + HG_WIDTH
OFF_AG = OFF_AI + HG_WIDTH
OFF_BQ = OFF_AG + HG_WIDTH
OFF_BKC = OFF_BQ + NSA_WIDTH
OFF_BVC = OFF_BKC + NSA_KV_WIDTH
OFF_BKS = OFF_BVC + NSA_KV_WIDTH
OFF_BVS = OFF_BKS + NSA_KV_WIDTH
OFF_BKW = OFF_BVS + NSA_KV_WIDTH
OFF_BVW = OFF_BKW + NSA_KV_WIDTH
OFF_BGATE = OFF_BVW + NSA_KV_WIDTH
OFF_CQ = OFF_BGATE + GATE_PAD
OFF_CK = OFF_CQ + RET_QK_WIDTH
OFF_CV = OFF_CK + RET_QK_WIDTH
OFF_CG = OFF_CV + RET_V_WIDTH
OFF_MA = OFF_CG + RET_V_WIDTH
OFF_MB = OFF_MA + D_MODEL
OFF_MC = OFF_MB + D_MODEL
N_PACK = OFF_MC + D_MODEL
N_IN_SRC = OFF_BGATE + GATE_COLS + (N_PACK - OFF_CQ)

NEG = -1e30
VMEM_LIMIT = 48 * 1024 * 1024


def _cparams(*sem):
    return pltpu.CompilerParams(dimension_semantics=sem, vmem_limit_bytes=VMEM_LIMIT)


def _sigmoid(x):
    return 1.0 / (1.0 + jnp.exp(-x))


def _silu(x):
    return x * _sigmoid(x)


def _dot(a, b, precision=None):
    return jnp.dot(a, b, preferred_element_type=F32, precision=precision)


def _dot_nt(a, b, precision=None):
    return lax.dot_general(a, b, (((1,), (1,)), ((), ())), preferred_element_type=F32, precision=precision)


def _rmsnorm_kernel(x_ref, w_ref, o_ref):
    x = x_ref[...]
    y = x * lax.rsqrt(jnp.mean(x * x, axis=-1, keepdims=True) + NORM_EPS)
    o_ref[...] = (y * w_ref[...]).astype(o_ref.dtype)


def _rmsnorm(x, w, out_dtype, tm=256):
    m, d = x.shape
    return pl.pallas_call(
        _rmsnorm_kernel,
        grid=(m // tm,),
        in_specs=[pl.BlockSpec((tm, d), lambda i: (i, 0)), pl.BlockSpec((1, d), lambda i: (0, 0))],
        out_specs=pl.BlockSpec((tm, d), lambda i: (i, 0)),
        out_shape=jax.ShapeDtypeStruct((m, d), out_dtype),
        compiler_params=_cparams("parallel"),
        name="rmsnorm",
    )(x, w.reshape(1, d))


def _mm_kernel(a_ref, b_ref, o_ref, *, relu2):
    acc = _dot(a_ref[...], b_ref[...])
    if relu2:
        acc = jnp.maximum(acc, 0.0)
        acc = acc * acc
    o_ref[...] = acc.astype(o_ref.dtype)


def _matmul(a, b, out_dtype, *, relu2=False, tm=1024, tn=512, name="matmul"):
    m, k = a.shape
    _, n = b.shape
    return pl.pallas_call(
        functools.partial(_mm_kernel, relu2=relu2),
        grid=(m // tm, n // tn),
        in_specs=[pl.BlockSpec((tm, k), lambda i, j: (i, 0)), pl.BlockSpec((k, tn), lambda i, j: (0, j))],
        out_specs=pl.BlockSpec((tm, tn), lambda i, j: (i, j)),
        out_shape=jax.ShapeDtypeStruct((m, n), out_dtype),
        compiler_params=_cparams("parallel", "arbitrary"),
        name=name,
    )(a, b)


def _mm_res_kernel(a_ref, b_ref, r_ref, o_ref, acc_ref, *, nk):
    kk = pl.program_id(2)
    part = _dot(a_ref[...], b_ref[...])

    @pl.when(kk == 0)
    def _():
        acc_ref[...] = part

    @pl.when(kk > 0)
    def _():
        acc_ref[...] += part

    @pl.when(kk == nk - 1)
    def _():
        o_ref[...] = r_ref[...] + acc_ref[...]


def _matmul_residual(a, b, res, *, tm=1024, tn=512, tk=4096, name="matmul_res"):
    m, k = a.shape
    _, n = b.shape
    nk = k // tk
    return pl.pallas_call(
        functools.partial(_mm_res_kernel, nk=nk),
        grid=(m // tm, n // tn, nk),
        in_specs=[pl.BlockSpec((tm, tk), lambda i, j, q: (i, q)),
                  pl.BlockSpec((tk, tn), lambda i, j, q: (q, j)),
                  pl.BlockSpec((tm, tn), lambda i, j, q: (i, j))],
        out_specs=pl.BlockSpec((tm, tn), lambda i, j, q: (i, j)),
        out_shape=jax.ShapeDtypeStruct((m, n), F32),
        scratch_shapes=[pltpu.VMEM((tm, tn), F32)],
        compiler_params=_cparams("parallel", "arbitrary", "arbitrary"),
        name=name,
    )(a, b, res)


def _merge_kernel(oa_ref, ob_ref, oc_ref, wa_ref, wb_ref, wc_ref, ga_ref, gb_ref, gc_ref, o_ref):
    acc = _sigmoid(ga_ref[...]) * _dot(oa_ref[...], wa_ref[...])
    acc += _sigmoid(gb_ref[...]) * _dot(ob_ref[...], wb_ref[...])
    acc += _sigmoid(gc_ref[...]) * _dot(oc_ref[...], wc_ref[...])
    o_ref[...] = acc.astype(o_ref.dtype)


def _branch_merge(o_a, o_b, o_c, wb, z, *, tm=512, tn=512):
    m = o_a.shape[0]
    kb = o_a.shape[1]
    o_spec = pl.BlockSpec((tm, kb), lambda i, j: (i, 0))

    def w_spec(r):
        return pl.BlockSpec((kb, tn), lambda i, j: (r, j))

    def g_spec(off):
        return pl.BlockSpec((tm, tn), lambda i, j: (i, off // tn + j))

    return pl.pallas_call(
        _merge_kernel,
        grid=(m // tm, D_MODEL // tn),
        in_specs=[o_spec, o_spec, o_spec, w_spec(0), w_spec(1), w_spec(2),
                  g_spec(OFF_MA), g_spec(OFF_MB), g_spec(OFF_MC)],
        out_specs=pl.BlockSpec((tm, tn), lambda i, j: (i, j)),
        out_shape=jax.ShapeDtypeStruct((m, D_MODEL), BF16),
        compiler_params=_cparams("parallel", "arbitrary"),
        name="branch_merge",
    )(o_a, o_b, o_c, wb, wb, wb, z, z, z)


def _hgrn_chunk(q_in, f_in, v, vt, g, lb, nw, st, tri):
    c = HG_CHUNK
    qc = _silu(q_in) * (HG_DIM ** -0.5)
    f = lb + (1.0 - lb) * _sigmoid(f_in)
    kk = 1.0 - f
    b = _dot(tri, jnp.log(f), precision=HIGHEST)
    o = _dot_nt((qc * jnp.exp(b)).astype(BF16), st.astype(BF16))
    b_last = b[c - 1:c, :]
    kdec = kk * jnp.exp(b_last - b)
    st_new = st * jnp.exp(b_last) + _dot(vt.astype(BF16), kdec.astype(BF16))
    row = lax.broadcasted_iota(jnp.int32, (HG_SUB, HG_DIM), 0)
    col = lax.broadcasted_iota(jnp.int32, (HG_SUB, c), 1)
    v16 = v.astype(BF16)
    parts = []
    for a in range(c // HG_SUB):
        lo = a * HG_SUB
        ba = b[lo:lo + HG_SUB, :]
        qa = qc[lo:lo + HG_SUB, :]
        acc = jnp.zeros((HG_SUB, HG_DIM), F32)
        if a > 0:
            bref = b[lo:lo + 1, :]
            qn = qa * jnp.exp(ba - bref)
            kn = kk * jnp.exp(jnp.minimum(bref - b, 0.0))
            att = _dot_nt(qn.astype(BF16), kn.astype(BF16))
            acc = _dot(jnp.where(col < lo, att, 0.0).astype(BF16), v16)
        for j in range(HG_SUB):
            jj = lo + j
            d = jnp.where(row >= j, ba - b[jj:jj + 1, :], NEG)
            w = jnp.sum(qa * kk[jj:jj + 1, :] * jnp.exp(d), axis=-1, keepdims=True)
            acc = acc + w * v[jj:jj + 1, :]
        parts.append(acc)
    o = o + jnp.concatenate(parts, axis=0)
    o = o * lax.rsqrt(jnp.mean(o * o, axis=-1, keepdims=True) + NORM_EPS) * nw * _silu(g)
    return o, st_new


def _hgrn_kernel(tab_ref, nw_ref, q_ref, f_ref, v_ref, vt_ref, g_ref, o_ref, st_ref, *, layer, rows):
    @pl.when(pl.program_id(2) == 0)
    def _():
        st_ref[...] = jnp.zeros_like(st_ref)

    tab = tab_ref[...]
    e = jnp.exp(tab - jnp.max(tab, axis=0, keepdims=True))
    p = e / jnp.sum(e, axis=0, keepdims=True)
    lb = jnp.sum(p[:layer + 1, :], axis=0, keepdims=True) - p[0:1, :]
    nw = nw_ref[...]
    r_i = lax.broadcasted_iota(jnp.int32, (HG_CHUNK, HG_CHUNK), 0)
    c_i = lax.broadcasted_iota(jnp.int32, (HG_CHUNK, HG_CHUNK), 1)
    tri = (r_i >= c_i).astype(F32)
    st = st_ref[...]
    for ch in range(rows // HG_CHUNK):
        sl = slice(ch * HG_CHUNK, (ch + 1) * HG_CHUNK)
        o, st = _hgrn_chunk(q_ref[sl, :], f_ref[sl, :], v_ref[sl, :], vt_ref[0, 0, :, sl], g_ref[sl, :],
                            lb, nw, st, tri)
        o_ref[sl, :] = o.astype(o_ref.dtype)
    st_ref[...] = st


def _hgrn2(z, vt, lb_table, norm_w, layer, batch, seq, rows=256):
    nrow = seq // rows

    def col(off):
        return pl.BlockSpec((rows, HG_DIM), lambda b, h, c: (b * nrow + c, off // HG_DIM + h))

    return pl.pallas_call(
        functools.partial(_hgrn_kernel, layer=layer, rows=rows),
        grid=(batch, HG_HEADS, nrow),
        in_specs=[pl.BlockSpec((DEPTH, HG_DIM), lambda b, h, c: (0, h)),
                  pl.BlockSpec((1, HG_DIM), lambda b, h, c: (0, 0)),
                  col(OFF_AQ), col(OFF_AF), col(OFF_AI),
                  pl.BlockSpec((1, 1, HG_DIM, rows), lambda b, h, c: (b, h, 0, c)),
                  col(OFF_AG)],
        out_specs=pl.BlockSpec((rows, HG_DIM), lambda b, h, c: (b * nrow + c, h)),
        out_shape=jax.ShapeDtypeStruct((batch * seq, HG_WIDTH), BF16),
        scratch_shapes=[pltpu.VMEM((HG_DIM, HG_DIM), F32)],
        compiler_params=_cparams("parallel", "parallel", "arbitrary"),
        name="hgrn2",
    )(lb_table, norm_w.reshape(1, HG_DIM), z, z, z, vt, z)


def _ret_kernel(lg_ref, q_ref, kt_ref, v_ref, g_ref, o_ref, s_ref, *, rows):
    @pl.when(pl.program_id(2) == 0)
    def _():
        s_ref[...] = jnp.zeros_like(s_ref)

    c = RET_CHUNK
    lg_v = lg_ref[0]
    lg_k = lg_v[:, :c]
    pos_r = lax.broadcasted_iota(jnp.int32, (c, RET_V), 0).astype(F32)
    query_decay = jnp.exp(lg_v * (pos_r + 1.0))
    rel = (lax.broadcasted_iota(jnp.int32, (c, c), 0) - lax.broadcasted_iota(jnp.int32, (c, c), 1)).astype(F32)
    intra_decay = jnp.where(rel >= 0, jnp.exp(lg_k * jnp.maximum(rel, 0.0)), 0.0)
    pos_c = lax.broadcasted_iota(jnp.int32, (1, c), 1).astype(F32)
    key_decay = jnp.exp(lg_k * (c - 1.0 - pos_c))
    chunk_decay = jnp.exp(lg_v * float(c))
    s = s_ref[...]
    for ch in range(rows // c):
        sl = slice(ch * c, (ch + 1) * c)
        q = q_ref[sl, :].astype(BF16)
        kt = kt_ref[0, 0, :, sl] * (RET_QK ** -0.5)
        v = v_ref[sl, :].astype(BF16)
        inter = _dot(q, s.astype(BF16)) * query_decay
        scores = _dot(q, kt.astype(BF16)) * intra_decay
        o = inter + _dot(scores.astype(BF16), v)
        s = chunk_decay * s + _dot((kt * key_decay).astype(BF16), v)
        mu = jnp.mean(o, axis=-1, keepdims=True)
        oc = o - mu
        o = oc * lax.rsqrt(jnp.mean(oc * oc, axis=-1, keepdims=True) + NORM_EPS)
        o_ref[sl, :] = (o * _silu(g_ref[sl, :])).astype(o_ref.dtype)
    s_ref[...] = s


def _retention(z, kt, batch, seq, rows=256):
    nrow = seq // rows
    log_gamma = jnp.log(1.0 - jnp.exp2(-5.0 - jnp.arange(RET_HEADS, dtype=F32)))
    lg = jnp.broadcast_to(log_gamma[:, None, None], (RET_HEADS, 1, RET_V))
    return pl.pallas_call(
        functools.partial(_ret_kernel, rows=rows),
        grid=(batch, RET_HEADS, nrow),
        in_specs=[pl.BlockSpec((1, 1, RET_V), lambda b, h, c: (h, 0, 0)),
                  pl.BlockSpec((rows, RET_QK), lambda b, h, c: (b * nrow + c, OFF_CQ // RET_QK + h)),
                  pl.BlockSpec((1, 1, RET_QK, rows), lambda b, h, c: (b, h, 0, c)),
                  pl.BlockSpec((rows, RET_V), lambda b, h, c: (b * nrow + c, OFF_CV // RET_V + h)),
                  pl.BlockSpec((rows, RET_V), lambda b, h, c: (b * nrow + c, OFF_CG // RET_V + h))],
        out_specs=pl.BlockSpec((rows, RET_V), lambda b, h, c: (b * nrow + c, h)),
        out_shape=jax.ShapeDtypeStruct((batch * seq, RET_V_WIDTH), BF16),
        scratch_shapes=[pltpu.VMEM((RET_QK, RET_V), F32)],
        compiler_params=_cparams("parallel", "parallel", "arbitrary"),
        name="retention",
    )(lg, z, kt, z, z)


def _compress_kernel(y_ref, pe_ref, w1_ref, w2_ref, o_ref):
    half = CMP_STRIDE * NSA_DIM
    y = y_ref[0, 0]
    pe = pe_ref[0]
    top = _dot((y + pe[:, :half]).astype(BF16), w1_ref[0, :half, :])
    bot = _dot((y + pe[:, half:]).astype(BF16), w1_ref[0, half:, :])
    nc = y.shape[0]
    h = top + pltpu.roll(bot, nc - 1, 0)
    o_ref[0, 0] = _dot(_silu(h).astype(BF16), w2_ref[0])


def _compress(y, pe, w1, w2):
    _, bg, nc, half = y.shape
    return pl.pallas_call(
        _compress_kernel,
        grid=(2, bg),
        in_specs=[pl.BlockSpec((1, 1, nc, half), lambda s, i: (s, i, 0, 0)),
                  pl.BlockSpec((1, 1, 2 * half), lambda s, i: (s, 0, 0)),
                  pl.BlockSpec((1, 2 * half, CMP_HIDDEN), lambda s, i: (s, 0, 0)),
                  pl.BlockSpec((1, CMP_HIDDEN, NSA_DIM), lambda s, i: (s, 0, 0))],
        out_specs=pl.BlockSpec((1, 1, nc, NSA_DIM), lambda s, i: (s, i, 0, 0)),
        out_shape=jax.ShapeDtypeStruct((2, bg, nc, NSA_DIM), F32),
        compiler_params=_cparams("parallel", "parallel"),
        name="nsa_compress",
    )(y, pe, w1, w2)


def _cmp_kernel(slope_ref, wmap_ref, q_ref, kc_ref, vc_ref, o_ref, sel_ref, score_ref, *, tq, nc, nslc):
    t0 = pl.program_id(2) * tq
    kc = kc_ref[0, 0]
    vc = vc_ref[0, 0].astype(BF16)
    t_i = t0 + lax.broadcasted_iota(jnp.int32, (tq, nc), 0)
    n_i = lax.broadcasted_iota(jnp.int32, (tq, nc), 1)
    dist = t_i - (n_i * CMP_STRIDE + CMP_BLOCK - 1)
    valid = (dist >= 0) & (n_i < nc - 1)
    distf = dist.astype(F32)
    imp = jnp.zeros((tq, nc), F32)
    for r in range(NSA_REP):
        q = q_ref[:, r * NSA_DIM:(r + 1) * NSA_DIM]
        slope = slope_ref[0, r:r + 1, :nc]
        s = _dot_nt(q, kc, precision=HIGHEST) * (NSA_DIM ** -0.5) - slope * distf
        s = jnp.where(valid, s, NEG)
        m = jnp.max(s, axis=-1, keepdims=True)
        e = jnp.where(valid, jnp.exp(s - m), 0.0)
        den = jnp.sum(e, axis=-1, keepdims=True)
        p = e / jnp.where(den > 0, den, 1.0)
        o_ref[:, r * NSA_DIM:(r + 1) * NSA_DIM] = _dot(p.astype(BF16), vc)
        imp = imp + p
    imp_t = _dot_nt(wmap_ref[...], imp, precision=HIGHEST)
    blk = lax.broadcasted_iota(jnp.int32, (nslc, tq), 0)
    cur = jnp.right_shift(t0 + lax.broadcasted_iota(jnp.int32, (nslc, tq), 1), SLC_SHIFT)
    forced = (blk == 0) | (blk == cur) | (blk == cur - 1)
    score = jnp.where(blk > cur, -jnp.inf, jnp.where(forced, jnp.inf, imp_t))
    score_ref[...] = score
    rank = jnp.zeros((nslc, tq), jnp.int32)
    for s_i in range(nslc):
        other = score_ref[s_i:s_i + 1, :]
        beats = (other > score) | ((other == score) & (blk > s_i))
        rank = rank + beats.astype(jnp.int32)
    sel_ref[0, 0] = (rank < min(SLC_TOPK, nslc)).astype(F32)


def _cmp_attention(z, kc, vc, batch, seq, tq=256):
    nc = seq // CMP_STRIDE
    nslc = seq // SLC_BLOCK
    nq = seq // tq
    slopes = np.exp2(-8.0 * np.arange(1, NSA_HEADS + 1, dtype=np.float64) / NSA_HEADS).astype(np.float32)
    slope_tab = np.zeros((NSA_GROUPS, 8, 512), np.float32)
    slope_tab[:, :NSA_REP, :] = slopes.reshape(NSA_GROUPS, NSA_REP)[:, :, None]
    c_start = np.arange(nc) * CMP_STRIDE
    s_start = np.arange(nslc) * SLC_BLOCK
    overlap = np.clip(np.minimum(c_start[:, None] + CMP_BLOCK, s_start[None, :] + SLC_BLOCK)
                      - np.maximum(c_start[:, None], s_start[None, :]), 0, None)
    wmap_t = (overlap.astype(np.float32) / CMP_STRIDE).T.copy()
    wmap_t[:, nc - 1] = 0.0
    return pl.pallas_call(
        functools.partial(_cmp_kernel, tq=tq, nc=nc, nslc=nslc),
        grid=(batch, NSA_GROUPS, nq),
        in_specs=[pl.BlockSpec((1, 8, 512), lambda b, g, i: (g, 0, 0)),
                  pl.BlockSpec((nslc, nc), lambda b, g, i: (0, 0)),
                  pl.BlockSpec((tq, NSA_REP * NSA_DIM), lambda b, g, i: (b * nq + i, OFF_BQ // (NSA_REP * NSA_DIM) + g)),
                  pl.BlockSpec((1, 1, nc, NSA_DIM), lambda b, g, i: (0, b * NSA_GROUPS + g, 0, 0)),
                  pl.BlockSpec((1, 1, nc, NSA_DIM), lambda b, g, i: (1, b * NSA_GROUPS + g, 0, 0))],
        out_specs=[pl.BlockSpec((tq, NSA_REP * NSA_DIM), lambda b, g, i: (b * nq + i, g)),
                   pl.BlockSpec((1, 1, nslc, tq), lambda b, g, i: (b, g, 0, i))],
        out_shape=[jax.ShapeDtypeStruct((batch * seq, NSA_WIDTH), F32),
                   jax.ShapeDtypeStruct((batch, NSA_GROUPS, nslc, seq), F32)],
        scratch_shapes=[pltpu.VMEM((nslc, tq), F32)],
        compiler_params=_cparams("parallel", "parallel", "arbitrary"),
        name="nsa_cmp_select",
    )(jnp.asarray(slope_tab), jnp.asarray(wmap_t), z, kc, vc)


def _softmax_step(s_all, valid, bias, slope_ref, k_lanes, v16, m, l, acc, tq):
    ms, ls, accs = [], [], []
    for r in range(NSA_REP):
        rs = slice(r * tq, (r + 1) * tq)
        slope = slope_ref[0, r:r + 1, :k_lanes]
        s = s_all[rs, :] * (NSA_DIM ** -0.5) - slope * bias
        s = jnp.where(valid, s, NEG)
        m_new = jnp.maximum(m[rs, :], jnp.max(s, axis=-1, keepdims=True))
        alpha = jnp.exp(m[rs, :] - m_new)
        p = jnp.where(valid, jnp.exp(s - m_new), 0.0)
        ls.append(alpha * l[rs, :] + jnp.sum(p, axis=-1, keepdims=True))
        accs.append(alpha * acc[rs, :] + _dot(p.astype(BF16), v16))
        ms.append(m_new)
    return jnp.concatenate(ms, axis=0), jnp.concatenate(ls, axis=0), jnp.concatenate(accs, axis=0)


def _slc_win_kernel(slope_ref, q_ref, ks_ref, vs_ref, kw_ref, vw_ref, sel_ref, gate_ref, ocmp_ref, o_ref,
                    *, tq, tk, nslc):
    qi = pl.program_id(2)
    t0 = qi * tq
    q = jnp.concatenate([q_ref[:, r * NSA_DIM:(r + 1) * NSA_DIM] for r in range(NSA_REP)], axis=0).astype(BF16)
    rows = NSA_REP * tq
    init = (jnp.full((rows, 1), NEG, F32), jnp.zeros((rows, 1), F32), jnp.zeros((rows, NSA_DIM), F32))

    sel = sel_ref[0, 0].astype(BF16)
    bpc = tk // SLC_BLOCK

    def slc_body(c, carry):
        k0 = pl.multiple_of(c * tk, tk)
        k = ks_ref[pl.ds(k0, tk), :].astype(BF16)
        v = vs_ref[pl.ds(k0, tk), :].astype(BF16)
        expand = (lax.broadcasted_iota(jnp.int32, (nslc, tk), 0)
                  == c * bpc + jnp.right_shift(lax.broadcasted_iota(jnp.int32, (nslc, tk), 1), SLC_SHIFT))
        picked = _dot(sel, expand.astype(BF16))
        dist = (t0 + lax.broadcasted_iota(jnp.int32, (tq, tk), 0)) - (k0 + lax.broadcasted_iota(jnp.int32, (tq, tk), 1))
        valid = (picked > 0.5) & (dist >= 0)
        return _softmax_step(_dot_nt(q, k), valid, dist.astype(F32), slope_ref, tk, v, *carry, tq)

    m_s, l_s, acc_s = lax.fori_loop(0, (t0 + tq - 1) // tk + 1, slc_body, init)

    def win_body(c, carry):
        k0 = pl.multiple_of(c * tq, tq)
        k = kw_ref[pl.ds(k0, tq), :].astype(BF16)
        v = vw_ref[pl.ds(k0, tq), :].astype(BF16)
        dist = (t0 + lax.broadcasted_iota(jnp.int32, (tq, tq), 0)) - (k0 + lax.broadcasted_iota(jnp.int32, (tq, tq), 1))
        valid = (dist >= 0) & (dist < WIN_SIZE)
        return _softmax_step(_dot_nt(q, k), valid, dist.astype(F32), slope_ref, tq, v, *carry, tq)

    m_w, l_w, acc_w = lax.fori_loop(jnp.maximum(qi - WIN_SIZE // tq, 0), qi + 1, win_body, init)

    gates = _sigmoid(gate_ref[0, 0])
    for r in range(NSA_REP):
        rs = slice(r * tq, (r + 1) * tq)
        cs = slice(r * NSA_DIM, (r + 1) * NSA_DIM)
        o_slc = acc_s[rs, :] / l_s[rs, :]
        o_win = acc_w[rs, :] / l_w[rs, :]
        o = (gates[:, 3 * r:3 * r + 1] * ocmp_ref[:, cs] + gates[:, 3 * r + 1:3 * r + 2] * o_slc
             + gates[:, 3 * r + 2:3 * r + 3] * o_win)
        o_ref[:, cs] = o.astype(o_ref.dtype)


def _slc_win_attention(z, sel, gates, o_cmp, batch, seq, tq=128, tk=512):
    nslc = seq // SLC_BLOCK
    nq = seq // tq
    tk = min(tk, seq)
    slopes = np.exp2(-8.0 * np.arange(1, NSA_HEADS + 1, dtype=np.float64) / NSA_HEADS).astype(np.float32)
    slope_tab = np.zeros((NSA_GROUPS, 8, 512), np.float32)
    slope_tab[:, :NSA_REP, :] = slopes.reshape(NSA_GROUPS, NSA_REP)[:, :, None]
    gw = NSA_REP * NSA_DIM

    def kv(off):
        return pl.BlockSpec((seq, NSA_DIM), lambda b, g, i: (b, off // NSA_DIM + g))

    return pl.pallas_call(
        functools.partial(_slc_win_kernel, tq=tq, tk=tk, nslc=nslc),
        grid=(batch, NSA_GROUPS, nq),
        in_specs=[pl.BlockSpec((1, 8, 512), lambda b, g, i: (g, 0, 0)),
                  pl.BlockSpec((tq, gw), lambda b, g, i: (b * nq + i, OFF_BQ // gw + g)),
                  kv(OFF_BKS), kv(OFF_BVS), kv(OFF_BKW), kv(OFF_BVW),
                  pl.BlockSpec((1, 1, tq, nslc), lambda b, g, i: (b, g, i, 0)),
                  pl.BlockSpec((1, 1, tq, 3 * NSA_REP), lambda b, g, i: (b, g, i, 0)),
                  pl.BlockSpec((tq, gw), lambda b, g, i: (b * nq + i, g))],
        out_specs=pl.BlockSpec((tq, gw), lambda b, g, i: (b * nq + i, g)),
        out_shape=jax.ShapeDtypeStruct((batch * seq, NSA_WIDTH), BF16),
        compiler_params=_cparams("parallel", "parallel", "arbitrary"),
        name="nsa_slc_win",
    )(jnp.asarray(slope_tab), z, z, z, z, z, sel, gates, o_cmp)


def _nsa(z, pe, w1, w2, batch, seq):
    nc = seq // CMP_STRIDE

    def blocks(off):
        a = z[:, off:off + NSA_KV_WIDTH].reshape(batch, seq, NSA_GROUPS, NSA_DIM).transpose(0, 2, 1, 3)
        return a.reshape(batch * NSA_GROUPS, nc, CMP_STRIDE * NSA_DIM)

    y = jnp.stack([blocks(OFF_BKC), blocks(OFF_BVC)])
    kvc = _compress(y, pe, w1, w2)
    o_cmp, sel_t = _cmp_attention(z, kvc, kvc, batch, seq)
    sel = sel_t.transpose(0, 1, 3, 2)
    gates = (z[:, OFF_BGATE:OFF_BGATE + GATE_COLS].reshape(batch, seq, NSA_GROUPS, 3 * NSA_REP)
             .transpose(0, 2, 1, 3))
    return _slc_win_attention(z, sel, gates, o_cmp, batch, seq)


def _pack_w_in(w):
    head = w[:, :OFF_BGATE + GATE_COLS]
    tail = w[:, OFF_BGATE + GATE_COLS:]
    pad = jnp.zeros((w.shape[0], GATE_PAD - GATE_COLS), w.dtype)
    return jnp.concatenate([head, pad, tail], axis=1).astype(BF16)


def _layer(x, layer, norm1_w, w_in, lb_table, hgrn_norm_w, pe, w1, w2, w_branch, w_out, norm2_w, w_ff1, w_ff2,
           batch, seq):
    h = _rmsnorm(x, norm1_w, BF16)
    z = _matmul(h, _pack_w_in(w_in), F32, name="proj_in")
    vt = z[:, OFF_AI:OFF_AI + HG_WIDTH].reshape(batch, seq, HG_HEADS, HG_DIM).transpose(0, 2, 3, 1)
    o_a = _hgrn2(z, vt, lb_table, hgrn_norm_w, layer, batch, seq)
    o_b = _nsa(z, pe, w1, w2, batch, seq)
    kt = z[:, OFF_CK:OFF_CK + RET_QK_WIDTH].reshape(batch, seq, RET_HEADS, RET_QK).transpose(0, 2, 3, 1)
    o_c = _retention(z, kt, batch, seq)
    merged = _branch_merge(o_a, o_b, o_c, w_branch.astype(BF16), z)
    x = _matmul_residual(merged, w_out.astype(BF16), x, name="proj_out")
    h = _rmsnorm(x, norm2_w, BF16)
    u = _matmul(h, w_ff1.astype(BF16), BF16, relu2=True, name="ffn_up")
    return _matmul_residual(u, w_ff2.astype(BF16), x, tk=4096, name="ffn_down")


def kernel(x, norm1_w, w_in, hgrn_lb_table, hgrn_norm_w, cmp_pe_k, cmp_pe_v, cmp_w1_k, cmp_w1_v, cmp_w2_k,
           cmp_w2_v, w_branch, w_out, norm2_w, w_ff1, w_ff2, final_norm_w):
    batch, seq, d = x.shape
    xf = x.reshape(batch * seq, d)
    for l in range(DEPTH):
        pe = jnp.stack([cmp_pe_k[l].reshape(1, -1), cmp_pe_v[l].reshape(1, -1)])
        w1 = jnp.stack([cmp_w1_k[l], cmp_w1_v[l]]).astype(BF16)
        w2 = jnp.stack([cmp_w2_k[l], cmp_w2_v[l]]).astype(BF16)
        xf = _layer(xf, l, norm1_w[l], w_in[l], hgrn_lb_table, hgrn_norm_w[l], pe, w1, w2, w_branch[l], w_out[l],
                    norm2_w[l], w_ff1[l], w_ff2[l], batch, seq)
    return _rmsnorm(xf, final_norm_w, F32).reshape(batch, seq, d)
```

```python
import functools

import numpy as np
import jax
import jax.numpy as jnp
from jax import lax
from jax.experimental import pallas as pl
from jax.experimental.pallas import tpu as pltpu

F32 = jnp.float32
BF16 = jnp.bfloat16
HIGHEST = lax.Precision.HIGHEST

D_MODEL = 4096
DEPTH = 2
NORM_EPS = 1e-6
LANE = 128

HG_HEADS = 16
HG_DIM = 128
HG_WIDTH = HG_HEADS * HG_DIM
HG_CHUNK = 64
HG_SUB = 16
NSA_HEADS = 16
NSA_GROUPS = 4
NSA_REP = NSA_HEADS // NSA_GROUPS
NSA_DIM = 128
NSA_WIDTH = NSA_HEADS * NSA_DIM
NSA_KV_WIDTH = NSA_GROUPS * NSA_DIM
NSA_GW = NSA_REP * NSA_DIM
CMP_BLOCK = 32
CMP_STRIDE = 16
CMP_HIDDEN = 256
SLC_BLOCK = 64
SLC_SHIFT = 6
SLC_TOPK = 16
WIN_SIZE = 512
ATT_T = 128
RET_HEADS = 8
RET_QK = 128
RET_V = 256
RET_QK_WIDTH = RET_HEADS * RET_QK
RET_V_WIDTH = RET_HEADS * RET_V
RET_CHUNK = 128
MIX_WIDTH = HG_WIDTH + NSA_WIDTH + RET_V_WIDTH
D_FF = 4 * D_MODEL

PROJ_TN = 512
GATE_COLS = NSA_HEADS * 3
OFF_AQ = 0
OFF_AF = OFF_AQ + HG_WIDTH
OFF_AI = OFF_AF + HG_WIDTH
OFF_AG = OFF_AI + HG_WIDTH
OFF_BQ = OFF_AG + HG_WIDTH
OFF_BKC = OFF_BQ + NSA_WIDTH
OFF_BVC = OFF_BKC + NSA_KV_WIDTH
OFF_BKS = OFF_BVC + NSA_KV_WIDTH
OFF_BVS = OFF_BKS + NSA_KV_WIDTH
OFF_BKW = OFF_BVS + NSA_KV_WIDTH
OFF_BVW = OFF_BKW + NSA_KV_WIDTH
OFF_BGATE = OFF_BVW + NSA_KV_WIDTH
ZA_WIDTH = OFF_BGATE + PROJ_TN
ZB_CQ = 0
ZB_CK = ZB_CQ + RET_QK_WIDTH
ZB_CV = ZB_CK + RET_QK_WIDTH
ZB_CG = ZB_CV + RET_V_WIDTH
ZB_MA = ZB_CG + RET_V_WIDTH
ZB_MB = ZB_MA + D_MODEL
ZB_MC = ZB_MB + D_MODEL
ZB_WIDTH = ZB_MC + D_MODEL

NEG = -1e30
VMEM_LIMIT = 56 * 1024 * 1024


def _cparams(*sem):
    return pltpu.CompilerParams(dimension_semantics=sem, vmem_limit_bytes=VMEM_LIMIT)


def _sigmoid(x):
    return 1.0 / (1.0 + jnp.exp(-x))


def _silu(x):
    return x * _sigmoid(x)


def _dot(a, b, precision=None):
    return jnp.dot(a, b, preferred_element_type=F32, precision=precision)


def _dot_nt(a, b, precision=None):
    return lax.dot_general(a, b, (((1,), (1,)), ((), ())), preferred_element_type=F32, precision=precision)


def _rmsnorm_kernel(x_ref, w_ref, o_ref):
    x = x_ref[...]
    y = x * lax.rsqrt(jnp.mean(x * x, axis=-1, keepdims=True) + NORM_EPS)
    o_ref[...] = (y * w_ref[...]).astype(o_ref.dtype)


def _rmsnorm(x, w, out_dtype, tm=256):
    m, d = x.shape
    return pl.pallas_call(
        _rmsnorm_kernel,
        grid=(m // tm,),
        in_specs=[pl.BlockSpec((tm, d), lambda i: (i, 0)), pl.BlockSpec((1, d), lambda i: (0, 0))],
        out_specs=pl.BlockSpec((tm, d), lambda i: (i, 0)),
        out_shape=jax.ShapeDtypeStruct((m, d), out_dtype),
        compiler_params=_cparams("parallel"),
        name="rmsnorm",
    )(x, w.reshape(1, d))


CAST_ROWS = 512


def _cast_weight(w_ref, wbf_ref, extra_ref=None, shift=0):
    k, tn = wbf_ref.shape
    for r0 in range(0, k, CAST_ROWS):
        w = w_ref[0, r0:r0 + CAST_ROWS, :]
        if extra_ref is not None:
            w = jnp.concatenate([w, extra_ref[0, r0:r0 + CAST_ROWS, :]], axis=1)[:, shift:shift + tn]
        wbf_ref[r0:r0 + CAST_ROWS, :] = w.astype(BF16)


def _ws_kernel(*refs, shift, relu2, residual):
    a_ref, w_ref = refs[0], refs[1]
    pos = 2
    extra_ref = None
    if shift:
        extra_ref = refs[pos]
        pos += 1
    res_ref = None
    if residual:
        res_ref = refs[pos]
        pos += 1
    o_ref, wbf_ref = refs[pos], refs[pos + 1]

    @pl.when(pl.program_id(1) == 0)
    def _():
        _cast_weight(w_ref, wbf_ref, extra_ref, shift)

    acc = _dot(a_ref[...], wbf_ref[...])
    if relu2:
        acc = jnp.maximum(acc, 0.0)
        acc = acc * acc
    if residual:
        acc = acc + res_ref[...]
    o_ref[...] = acc.astype(o_ref.dtype)


def _ws_matmul(a, w, layer, out_dtype, *, n_tiles, tile0=0, shift=0, relu2=False, res=None, tm=1024,
               tn=PROJ_TN, name="ws_matmul"):
    m, k = a.shape
    in_specs = [pl.BlockSpec((tm, k), lambda j, i: (i, 0)),
                pl.BlockSpec((1, k, tn), lambda j, i: (layer, 0, tile0 + j))]
    args = [a, w]
    if shift:
        per = tn // LANE
        in_specs.append(pl.BlockSpec((1, k, LANE), lambda j, i: (layer, 0, (tile0 + j + 1) * per)))
        args.append(w)
    if res is not None:
        in_specs.append(pl.BlockSpec((tm, tn), lambda j, i: (i, j)))
        args.append(res)
    return pl.pallas_call(
        functools.partial(_ws_kernel, shift=shift, relu2=relu2, residual=res is not None),
        grid=(n_tiles, m // tm),
        in_specs=in_specs,
        out_specs=pl.BlockSpec((tm, tn), lambda j, i: (i, j)),
        out_shape=jax.ShapeDtypeStruct((m, n_tiles * tn), out_dtype),
        scratch_shapes=[pltpu.VMEM((k, tn), BF16)],
        compiler_params=_cparams("arbitrary", "arbitrary"),
        name=name,
    )(*args)


def _mm_res_kernel(a_ref, b_ref, r_ref, o_ref, acc_ref, *, nk):
    kk = pl.program_id(2)
    part = _dot(a_ref[...], b_ref[0])

    @pl.when(kk == 0)
    def _():
        acc_ref[...] = part

    @pl.when(kk > 0)
    def _():
        acc_ref[...] += part

    @pl.when(kk == nk - 1)
    def _():
        o_ref[...] = r_ref[...] + acc_ref[...]


def _matmul_residual(a, b, layer, res, *, tm=1024, tn=512, tk=4096, name="matmul_res"):
    m, k = a.shape
    n = b.shape[2]
    nk = k // tk
    return pl.pallas_call(
        functools.partial(_mm_res_kernel, nk=nk),
        grid=(m // tm, n // tn, nk),
        in_specs=[pl.BlockSpec((tm, tk), lambda i, j, q: (i, q)),
                  pl.BlockSpec((1, tk, tn), lambda i, j, q: (layer, q, j)),
                  pl.BlockSpec((tm, tn), lambda i, j, q: (i, j))],
        out_specs=pl.BlockSpec((tm, tn), lambda i, j, q: (i, j)),
        out_shape=jax.ShapeDtypeStruct((m, n), F32),
        scratch_shapes=[pltpu.VMEM((tm, tn), F32)],
        compiler_params=_cparams("parallel", "arbitrary", "arbitrary"),
        name=name,
    )(a, b, res)


def _merge_kernel(oa_ref, ob_ref, oc_ref, wa_ref, wb_ref, wc_ref, ga_ref, gb_ref, gc_ref, o_ref,
                  wa16, wb16, wc16):
    @pl.when(pl.program_id(1) == 0)
    def _():
        _cast_weight(wa_ref, wa16)
        _cast_weight(wb_ref, wb16)
        _cast_weight(wc_ref, wc16)

    acc = _sigmoid(ga_ref[...]) * _dot(oa_ref[...], wa16[...])
    acc += _sigmoid(gb_ref[...]) * _dot(ob_ref[...], wb16[...])
    acc += _sigmoid(gc_ref[...]) * _dot(oc_ref[...], wc16[...])
    o_ref[...] = acc.astype(o_ref.dtype)


def _branch_merge(o_a, o_b, o_c, w_branch, layer, zb, *, tm=512, tn=512):
    m, kb = o_a.shape
    o_spec = pl.BlockSpec((tm, kb), lambda j, i: (i, 0))

    def w_spec(r):
        return pl.BlockSpec((1, kb, tn), lambda j, i: (layer, r, j))

    def g_spec(off):
        return pl.BlockSpec((tm, tn), lambda j, i: (i, off // tn + j))

    return pl.pallas_call(
        _merge_kernel,
        grid=(D_MODEL // tn, m // tm),
        in_specs=[o_spec, o_spec, o_spec, w_spec(0), w_spec(1), w_spec(2),
                  g_spec(ZB_MA), g_spec(ZB_MB), g_spec(ZB_MC)],
        out_specs=pl.BlockSpec((tm, tn), lambda j, i: (i, j)),
        out_shape=jax.ShapeDtypeStruct((m, D_MODEL), BF16),
        scratch_shapes=[pltpu.VMEM((kb, tn), BF16)] * 3,
        compiler_params=_cparams("arbitrary", "arbitrary"),
        name="branch_merge",
    )(o_a, o_b, o_c, w_branch, w_branch, w_branch, zb, zb, zb)


def _hgrn_chunk(q_in, f_in, v, vt, g, lb, nw, st, tri):
    c = HG_CHUNK
    qc = _silu(q_in) * (HG_DIM ** -0.5)
    f = lb + (1.0 - lb) * _sigmoid(f_in)
    kk = 1.0 - f
    b = _dot(tri, jnp.log(f), precision=HIGHEST)
    o = _dot_nt((qc * jnp.exp(b)).astype(BF16), st.astype(BF16))
    b_last = b[c - 1:c, :]
    kdec = kk * jnp.exp(b_last - b)
    st_new = st * jnp.exp(b_last) + _dot(vt.astype(BF16), kdec.astype(BF16))
    row = lax.broadcasted_iota(jnp.int32, (HG_SUB, HG_DIM), 0)
    col = lax.broadcasted_iota(jnp.int32, (HG_SUB, c), 1)
    v16 = v.astype(BF16)
    parts = []
    for a in range(c // HG_SUB):
        lo = a * HG_SUB
        ba = b[lo:lo + HG_SUB, :]
        qa = qc[lo:lo + HG_SUB, :]
        acc = jnp.zeros((HG_SUB, HG_DIM), F32)
        if a > 0:
            bref = b[lo:lo + 1, :]
            qn = qa * jnp.exp(ba - bref)
            kn = kk * jnp.exp(jnp.minimum(bref - b, 0.0))
            att = _dot_nt(qn.astype(BF16), kn.astype(BF16))
            acc = _dot(jnp.where(col < lo, att, 0.0).astype(BF16), v16)
        for j in range(HG_SUB):
            jj = lo + j
            d = jnp.where(row >= j, ba - b[jj:jj + 1, :], NEG)
            w = jnp.sum(qa * kk[jj:jj + 1, :] * jnp.exp(d), axis=-1, keepdims=True)
            acc = acc + w * v[jj:jj + 1, :]
        parts.append(acc)
    o = o + jnp.concatenate(parts, axis=0)
    o = o * lax.rsqrt(jnp.mean(o * o, axis=-1, keepdims=True) + NORM_EPS) * nw * _silu(g)
    return o, st_new


def _hgrn_kernel(tab_ref, nw_ref, q_ref, f_ref, v_ref, vt_ref, g_ref, o_ref, st_ref, *, layer, rows):
    @pl.when(pl.program_id(2) == 0)
    def _():
        st_ref[...] = jnp.zeros_like(st_ref)

    tab = tab_ref[...]
    e = jnp.exp(tab - jnp.max(tab, axis=0, keepdims=True))
    p = e / jnp.sum(e, axis=0, keepdims=True)
    lb = jnp.sum(p[:layer + 1, :], axis=0, keepdims=True) - p[0:1, :]
    nw = nw_ref[...]
    r_i = lax.broadcasted_iota(jnp.int32, (HG_CHUNK, HG_CHUNK), 0)
    c_i = lax.broadcasted_iota(jnp.int32, (HG_CHUNK, HG_CHUNK), 1)
    tri = (r_i >= c_i).astype(F32)
    st = st_ref[...]
    for ch in range(rows // HG_CHUNK):
        sl = slice(ch * HG_CHUNK, (ch + 1) * HG_CHUNK)
        o, st = _hgrn_chunk(q_ref[sl, :], f_ref[sl, :], v_ref[sl, :], vt_ref[0, 0, :, sl], g_ref[sl, :],
                            lb, nw, st, tri)
        o_ref[sl, :] = o.astype(o_ref.dtype)
    st_ref[...] = st


def _hgrn2(za, vt, lb_table, norm_w, layer, batch, seq, rows=256):
    nrow = seq // rows

    def col(off):
        return pl.BlockSpec((rows, HG_DIM), lambda b, h, c: (b * nrow + c, off // HG_DIM + h))

    return pl.pallas_call(
        functools.partial(_hgrn_kernel, layer=layer, rows=rows),
        grid=(batch, HG_HEADS, nrow),
        in_specs=[pl.BlockSpec((DEPTH, HG_DIM), lambda b, h, c: (0, h)),
                  pl.BlockSpec((1, HG_DIM), lambda b, h, c: (0, 0)),
                  col(OFF_AQ), col(OFF_AF), col(OFF_AI),
                  pl.BlockSpec((1, 1, HG_DIM, rows), lambda b, h, c: (b, h, 0, c)),
                  col(OFF_AG)],
        out_specs=pl.BlockSpec((rows, HG_DIM), lambda b, h, c: (b * nrow + c, h)),
        out_shape=jax.ShapeDtypeStruct((batch * seq, HG_WIDTH), BF16),
        scratch_shapes=[pltpu.VMEM((HG_DIM, HG_DIM), F32)],
        compiler_params=_cparams("parallel", "parallel", "arbitrary"),
        name="hgrn2",
    )(lb_table, norm_w.reshape(1, HG_DIM), za, za, za, vt, za)


def _ret_kernel(lg_ref, q_ref, kt_ref, v_ref, g_ref, o_ref, s_ref, *, rows):
    @pl.when(pl.program_id(2) == 0)
    def _():
        s_ref[...] = jnp.zeros_like(s_ref)

    c = RET_CHUNK
    lg_v = lg_ref[0]
    lg_k = lg_v[:, :c]
    pos_r = lax.broadcasted_iota(jnp.int32, (c, RET_V), 0).astype(F32)
    query_decay = jnp.exp(lg_v * (pos_r + 1.0))
    rel = (lax.broadcasted_iota(jnp.int32, (c, c), 0) - lax.broadcasted_iota(jnp.int32, (c, c), 1)).astype(F32)
    intra_decay = jnp.where(rel >= 0, jnp.exp(lg_k * jnp.maximum(rel, 0.0)), 0.0)
    pos_c = lax.broadcasted_iota(jnp.int32, (1, c), 1).astype(F32)
    key_decay = jnp.exp(lg_k * (c - 1.0 - pos_c))
    chunk_decay = jnp.exp(lg_v * float(c))
    s = s_ref[...]
    for ch in range(rows // c):
        sl = slice(ch * c, (ch + 1) * c)
        q = q_ref[sl, :].astype(BF16)
        kt = kt_ref[0, 0, :, sl] * (RET_QK ** -0.5)
        v = v_ref[sl, :].astype(BF16)
        inter = _dot(q, s.astype(BF16)) * query_decay
        scores = _dot(q, kt.astype(BF16)) * intra_decay
        o = inter + _dot(scores.astype(BF16), v)
        s = chunk_decay * s + _dot((kt * key_decay).astype(BF16), v)
        mu = jnp.mean(o, axis=-1, keepdims=True)
        oc = o - mu
        o = oc * lax.rsqrt(jnp.mean(oc * oc, axis=-1, keepdims=True) + NORM_EPS)
        o_ref[sl, :] = (o * _silu(g_ref[sl, :])).astype(o_ref.dtype)
    s_ref[...] = s


def _retention(zb, kt, batch, seq, rows=256):
    nrow = seq // rows
    log_gamma = jnp.log(1.0 - jnp.exp2(-5.0 - jnp.arange(RET_HEADS, dtype=F32)))
    lg = jnp.broadcast_to(log_gamma[:, None, None], (RET_HEADS, 1, RET_V))
    return pl.pallas_call(
        functools.partial(_ret_kernel, rows=rows),
        grid=(batch, RET_HEADS, nrow),
        in_specs=[pl.BlockSpec((1, 1, RET_V), lambda b, h, c: (h, 0, 0)),
                  pl.BlockSpec((rows, RET_QK), lambda b, h, c: (b * nrow + c, ZB_CQ // RET_QK + h)),
                  pl.BlockSpec((1, 1, RET_QK, rows), lambda b, h, c: (b, h, 0, c)),
                  pl.BlockSpec((rows, RET_V), lambda b, h, c: (b * nrow + c, ZB_CV // RET_V + h)),
                  pl.BlockSpec((rows, RET_V), lambda b, h, c: (b * nrow + c, ZB_CG // RET_V + h))],
        out_specs=pl.BlockSpec((rows, RET_V), lambda b, h, c: (b * nrow + c, h)),
        out_shape=jax.ShapeDtypeStruct((batch * seq, RET_V_WIDTH), BF16),
        scratch_shapes=[pltpu.VMEM((RET_QK, RET_V), F32)],
        compiler_params=_cparams("parallel", "parallel", "arbitrary"),
        name="retention",
    )(lg, zb, kt, zb, zb)


def _compress_kernel(y_ref, pe_ref, w1_ref, w2_ref, o_ref):
    half = CMP_STRIDE * NSA_DIM
    y = y_ref[0, 0]
    pe = pe_ref[0]
    top = _dot((y + pe[:, :half]).astype(BF16), w1_ref[0, :half, :])
    bot = _dot((y + pe[:, half:]).astype(BF16), w1_ref[0, half:, :])
    nc = y.shape[0]
    h = top + pltpu.roll(bot, nc - 1, 0)
    o_ref[0, 0] = _dot(_silu(h).astype(BF16), w2_ref[0])


def _compress(y, pe, w1, w2):
    _, bg, nc, half = y.shape
    return pl.pallas_call(
        _compress_kernel,
        grid=(2, bg),
        in_specs=[pl.BlockSpec((1, 1, nc, half), lambda s, i: (s, i, 0, 0)),
                  pl.BlockSpec((1, 1, 2 * half), lambda s, i: (s, 0, 0)),
                  pl.BlockSpec((1, 2 * half, CMP_HIDDEN), lambda s, i: (s, 0, 0)),
                  pl.BlockSpec((1, CMP_HIDDEN, NSA_DIM), lambda s, i: (s, 0, 0))],
        out_specs=pl.BlockSpec((1, 1, nc, NSA_DIM), lambda s, i: (s, i, 0, 0)),
        out_shape=jax.ShapeDtypeStruct((2, bg, nc, NSA_DIM), F32),
        compiler_params=_cparams("parallel", "parallel"),
        name="nsa_compress",
    )(y, pe, w1, w2)


def _slope_table():
    slopes = np.exp2(-8.0 * np.arange(1, NSA_HEADS + 1, dtype=np.float64) / NSA_HEADS).astype(np.float32)
    tab = np.zeros((NSA_GROUPS, 8, 512), np.float32)
    tab[:, :NSA_REP, :] = slopes.reshape(NSA_GROUPS, NSA_REP)[:, :, None]
    return jnp.asarray(tab)


def _cmp_kernel(slope_ref, wmap_ref, q_ref, kc_ref, vc_ref, gate_ref, o_ref, sel_ref, used_ref, score_ref,
                *, tq, nc, nslc):
    t0 = pl.program_id(2) * tq
    kc = kc_ref[0, 0]
    vc = vc_ref[0, 0].astype(BF16)
    gates = _sigmoid(gate_ref[0, 0])
    t_i = t0 + lax.broadcasted_iota(jnp.int32, (tq, nc), 0)
    n_i = lax.broadcasted_iota(jnp.int32, (tq, nc), 1)
    dist = t_i - (n_i * CMP_STRIDE + CMP_BLOCK - 1)
    valid = (dist >= 0) & (n_i < nc - 1)
    distf = dist.astype(F32)
    imp = jnp.zeros((tq, nc), F32)
    for r in range(NSA_REP):
        q = q_ref[:, r * NSA_DIM:(r + 1) * NSA_DIM]
        slope = slope_ref[0, r:r + 1, :nc]
        s = _dot_nt(q, kc, precision=HIGHEST) * (NSA_DIM ** -0.5) - slope * distf
        s = jnp.where(valid, s, NEG)
        m = jnp.max(s, axis=-1, keepdims=True)
        e = jnp.where(valid, jnp.exp(s - m), 0.0)
        den = jnp.sum(e, axis=-1, keepdims=True)
        p = e / jnp.where(den > 0, den, 1.0)
        o_ref[:, r * NSA_DIM:(r + 1) * NSA_DIM] = gates[:, 3 * r:3 * r + 1] * _dot(p.astype(BF16), vc)
        imp = imp + p
    imp_t = _dot_nt(wmap_ref[...], imp, precision=HIGHEST)
    blk = lax.broadcasted_iota(jnp.int32, (nslc, tq), 0)
    cur = jnp.right_shift(t0 + lax.broadcasted_iota(jnp.int32, (nslc, tq), 1), SLC_SHIFT)
    forced = (blk == 0) | (blk == cur) | (blk == cur - 1)
    score = jnp.where(blk > cur, -jnp.inf, jnp.where(forced, jnp.inf, imp_t))
    score_ref[...] = score
    rank = jnp.zeros((nslc, tq), jnp.int32)
    for s_i in range(nslc):
        other = score_ref[s_i:s_i + 1, :]
        beats = (other > score) | ((other == score) & (blk > s_i))
        rank = rank + beats.astype(jnp.int32)
    sel = ((rank < min(SLC_TOPK, nslc)) & (blk <= cur)).astype(F32)
    sel_ref[0, 0] = sel
    for a in range(tq // ATT_T):
        used_ref[0, 0, 0, :, a:a + 1] = jnp.max(sel[:, a * ATT_T:(a + 1) * ATT_T], axis=1, keepdims=True)


def _cmp_attention(za, kvc, gates, batch, seq, tq=256):
    nc = seq // CMP_STRIDE
    nslc = seq // SLC_BLOCK
    nq = seq // tq
    c_start = np.arange(nc) * CMP_STRIDE
    s_start = np.arange(nslc) * SLC_BLOCK
    overlap = np.clip(np.minimum(c_start[:, None] + CMP_BLOCK, s_start[None, :] + SLC_BLOCK)
                      - np.maximum(c_start[:, None], s_start[None, :]), 0, None)
    wmap_t = (overlap.astype(np.float32) / CMP_STRIDE).T.copy()
    wmap_t[:, nc - 1] = 0.0
    sub = tq // ATT_T
    return pl.pallas_call(
        functools.partial(_cmp_kernel, tq=tq, nc=nc, nslc=nslc),
        grid=(batch, NSA_GROUPS, nq),
        in_specs=[pl.BlockSpec((1, 8, 512), lambda b, g, i: (g, 0, 0)),
                  pl.BlockSpec((nslc, nc), lambda b, g, i: (0, 0)),
                  pl.BlockSpec((tq, NSA_GW), lambda b, g, i: (b * nq + i, OFF_BQ // NSA_GW + g)),
                  pl.BlockSpec((1, 1, nc, NSA_DIM), lambda b, g, i: (0, b * NSA_GROUPS + g, 0, 0)),
                  pl.BlockSpec((1, 1, nc, NSA_DIM), lambda b, g, i: (1, b * NSA_GROUPS + g, 0, 0)),
                  pl.BlockSpec((1, 1, tq, 3 * NSA_REP), lambda b, g, i: (b, g, i, 0))],
        out_specs=[pl.BlockSpec((tq, NSA_GW), lambda b, g, i: (b * nq + i, g)),
                   pl.BlockSpec((1, 1, nslc, tq), lambda b, g, i: (b, g, 0, i)),
                   pl.BlockSpec((1, 1, 1, nslc, sub), lambda b, g, i: (b, g, i, 0, 0))],
        out_shape=[jax.ShapeDtypeStruct((batch * seq, NSA_WIDTH), F32),
                   jax.ShapeDtypeStruct((batch, NSA_GROUPS, nslc, seq), F32),
                   jax.ShapeDtypeStruct((batch, NSA_GROUPS, nq, nslc, sub), F32)],
        scratch_shapes=[pltpu.VMEM((nslc, tq), F32)],
        compiler_params=_cparams("parallel", "parallel", "arbitrary"),
        name="nsa_cmp_select",
    )(_slope_table(), jnp.asarray(wmap_t), za, kvc, kvc, gates)


ATT_SUPER = 4
ALIBI_FEATS = 6


def _alibi_query_features():
    slopes = np.exp2(-8.0 * np.arange(1, NSA_HEADS + 1, dtype=np.float64) / NSA_HEADS).astype(np.float32)

    def top_bits(x):
        return (x.view(np.uint32) & np.uint32(0xFFFF0000)).view(np.float32)

    s1 = top_bits(slopes)
    r1 = slopes - s1
    s2 = top_bits(r1)
    s3 = r1 - s2
    rows = np.stack([-64.0 * s1, -64.0 * s2, -64.0 * s3, -s1, -s2, -s3]).astype(np.float32)
    feat = np.zeros((NSA_GROUPS, NSA_DIM, NSA_REP, ATT_T), np.float32)
    feat[:, :ALIBI_FEATS] = rows.reshape(ALIBI_FEATS, NSA_GROUPS, NSA_REP).transpose(1, 0, 2)[:, :, :, None]
    return jnp.asarray(feat.reshape(NSA_GROUPS, NSA_DIM, NSA_GW), dtype=BF16)


def _attend(k_ref, vt_ref, chunks, valids, qi, qt_aug, kf_static, kf_ind, m_ref, l_ref, acc_ref):
    t = ATT_T
    ks, vts, mbs = [], [], []
    for c, valid in zip(chunks, valids):
        k0 = pl.multiple_of(c * t, t)
        kfeat = (kf_static + (2 * (qi - c)).astype(F32) * kf_ind).astype(BF16)
        ks.append(jnp.concatenate([k_ref[pl.ds(k0, t), :].astype(BF16), kfeat], axis=1))
        vts.append(vt_ref[0, 0, c].astype(BF16))
        mbs.append(jnp.where(valid, 0.0, NEG))
    st = _dot(jnp.concatenate(ks, axis=0), qt_aug)
    mbias = jnp.concatenate(mbs, axis=0)
    ps, alphas = [], []
    for r in range(NSA_REP):
        cs = slice(r * t, (r + 1) * t)
        s = st[:, cs] + mbias
        m_old = m_ref[:, cs]
        m_new = jnp.maximum(m_old, jnp.max(s, axis=0, keepdims=True))
        alphas.append(jnp.exp(m_old - m_new))
        p = jnp.exp(s - m_new)
        l_ref[:, cs] = alphas[-1] * l_ref[:, cs] + jnp.sum(p, axis=0, keepdims=True)
        m_ref[:, cs] = m_new
        ps.append(p.astype(BF16))
    acc_ref[...] = (jnp.concatenate(alphas, axis=1) * acc_ref[...]
                    + _dot(jnp.concatenate(vts, axis=1), jnp.concatenate(ps, axis=1)))


def _slc_win_kernel(ids_ref, cnt_ref, qfeat_ref, qt_ref, ks_ref, vst_ref, kw_ref, vwt_ref, sel_ref, gate_ref,
                    ocmp_ref, o_ref, m_ref, l_ref, acc_ref, *, nq):
    t = ATT_T
    b, g, qi = pl.program_id(0), pl.program_id(1), pl.program_id(2)
    t0 = qi * t
    qt_aug = jnp.concatenate([(qt_ref[0, 0, 0] * (NSA_DIM ** -0.5)).astype(BF16), qfeat_ref[0]], axis=0)
    key_i = lax.broadcasted_iota(jnp.int32, (t, t), 0)
    qry_i = lax.broadcasted_iota(jnp.int32, (t, t), 1)
    back = (t - 1) - key_i
    kf_ind = (qry_i < 3).astype(F32)
    kf_static = jnp.where(qry_i < 3, jnp.right_shift(back, SLC_SHIFT),
                          jnp.where(qry_i < ALIBI_FEATS, jnp.bitwise_and(back, SLC_BLOCK - 1), 0)).astype(F32)

    def reset():
        m_ref[...] = jnp.full_like(m_ref, NEG)
        l_ref[...] = jnp.zeros_like(l_ref)
        acc_ref[...] = jnp.zeros_like(acc_ref)

    reset()
    tile = (b * NSA_GROUPS + g) * nq + qi
    cnt = cnt_ref[tile]
    half = SLC_BLOCK
    never = jnp.int32(2 ** 30)

    def slc_body(s, carry):
        chunks, valids = [], []
        for u in range(ATT_SUPER):
            e = s * ATT_SUPER + u
            c = ids_ref[tile * nq + jnp.minimum(e, cnt - 1)]
            lo = jnp.broadcast_to(sel_ref[0, 0, pl.ds(2 * c, 1), :], (half, t))
            hi = jnp.broadcast_to(sel_ref[0, 0, pl.ds(2 * c + 1, 1), :], (half, t))
            picked = jnp.concatenate([lo, hi], axis=0) > 0.5
            dist = (t0 + qry_i) - (c * t + key_i)
            chunks.append(c)
            valids.append(picked & (dist >= jnp.where(e < cnt, 0, never)))
        _attend(ks_ref, vst_ref, chunks, valids, qi, qt_aug, kf_static, kf_ind, m_ref, l_ref, acc_ref)
        return carry

    lax.fori_loop(0, jnp.right_shift(cnt + (ATT_SUPER - 1), 2), slc_body, 0)
    o_slc = acc_ref[...] * (1.0 / l_ref[...])

    reset()
    chunks, valids = [], []
    for i in range(WIN_SIZE // t + 1):
        c = jnp.maximum(qi - i, 0)
        dist = (t0 + qry_i) - (c * t + key_i)
        chunks.append(c)
        valids.append((dist >= jnp.where(qi - i >= 0, 0, never)) & (dist < WIN_SIZE))
    _attend(kw_ref, vwt_ref, chunks, valids, qi, qt_aug, kf_static, kf_ind, m_ref, l_ref, acc_ref)
    o_win = acc_ref[...] * (1.0 / l_ref[...])

    gates = _sigmoid(gate_ref[0, 0])
    for r in range(NSA_REP):
        cs = slice(r * t, (r + 1) * t)
        o_t = gates[3 * r + 1:3 * r + 2, :] * o_slc[:, cs] + gates[3 * r + 2:3 * r + 3, :] * o_win[:, cs]
        o_ref[:, cs] = (o_t.T + ocmp_ref[:, cs]).astype(o_ref.dtype)


def _slc_win_attention(ids, cnt, qt, za, vst, vwt, sel_t, gates_t, o_cmp, batch, seq):
    t = ATT_T
    nslc = seq // SLC_BLOCK
    nq = seq // t

    def kv(off):
        return pl.BlockSpec((seq, NSA_DIM), lambda b, g, i, *_: (b, off // NSA_DIM + g))

    vt_spec = pl.BlockSpec((1, 1, nq, NSA_DIM, t), lambda b, g, i, *_: (b, g, 0, 0, 0))
    grid_spec = pltpu.PrefetchScalarGridSpec(
        num_scalar_prefetch=2,
        grid=(batch, NSA_GROUPS, nq),
        in_specs=[pl.BlockSpec((1, NSA_DIM, NSA_GW), lambda b, g, i, *_: (g, 0, 0)),
                  pl.BlockSpec((1, 1, 1, NSA_DIM, NSA_GW), lambda b, g, i, *_: (b, g, i, 0, 0)),
                  kv(OFF_BKS), vt_spec, kv(OFF_BKW), vt_spec,
                  pl.BlockSpec((1, 1, nslc, t), lambda b, g, i, *_: (b, g, 0, i)),
                  pl.BlockSpec((1, 1, 3 * NSA_REP, t), lambda b, g, i, *_: (b, g, 0, i)),
                  pl.BlockSpec((t, NSA_GW), lambda b, g, i, *_: (b * nq + i, g))],
        out_specs=pl.BlockSpec((t, NSA_GW), lambda b, g, i, *_: (b * nq + i, g)),
        scratch_shapes=[pltpu.VMEM((1, NSA_GW), F32), pltpu.VMEM((1, NSA_GW), F32),
                        pltpu.VMEM((NSA_DIM, NSA_GW), F32)])
    return pl.pallas_call(
        functools.partial(_slc_win_kernel, nq=nq),
        grid_spec=grid_spec,
        out_shape=jax.ShapeDtypeStruct((batch * seq, NSA_WIDTH), BF16),
        compiler_params=_cparams("parallel", "parallel", "arbitrary"),
        name="nsa_slc_win",
    )(ids, cnt, _alibi_query_features(), qt, za, vst, za, vwt, sel_t, gates_t, o_cmp)


def _nsa(za, pe, w1, w2, batch, seq):
    nc = seq // CMP_STRIDE
    t = ATT_T
    nq = seq // t

    def grouped(off):
        return za[:, off:off + NSA_KV_WIDTH].reshape(batch, seq, NSA_GROUPS, NSA_DIM).transpose(0, 2, 1, 3)

    def blocks(off):
        return grouped(off).reshape(batch * NSA_GROUPS, nc, CMP_STRIDE * NSA_DIM)

    def chunks_t(off):
        return grouped(off).reshape(batch, NSA_GROUPS, nq, t, NSA_DIM).transpose(0, 1, 2, 4, 3)

    kvc = _compress(jnp.stack([blocks(OFF_BKC), blocks(OFF_BVC)]), pe, w1, w2)
    gates = (za[:, OFF_BGATE:OFF_BGATE + GATE_COLS].reshape(batch, seq, NSA_GROUPS, 3 * NSA_REP)
             .transpose(0, 2, 1, 3))
    o_cmp, sel_t, used = _cmp_attention(za, kvc, gates, batch, seq)
    nslc = seq // SLC_BLOCK
    used = used.transpose(0, 1, 2, 4, 3).reshape(batch * NSA_GROUPS * nq, nslc // 2, 2)
    unused = (jnp.max(used, axis=-1) < 0.5).astype(jnp.int32)
    ids = jnp.argsort(unused, axis=-1, stable=True).astype(jnp.int32).reshape(-1)
    cnt = (nslc // 2 - jnp.sum(unused, axis=-1)).astype(jnp.int32)
    qt = (za[:, OFF_BQ:OFF_BQ + NSA_WIDTH].reshape(batch, nq, t, NSA_GROUPS, NSA_REP, NSA_DIM)
          .transpose(0, 3, 1, 5, 4, 2).reshape(batch, NSA_GROUPS, nq, NSA_DIM, NSA_GW))
    return _slc_win_attention(ids, cnt, qt, za, chunks_t(OFF_BVS), chunks_t(OFF_BVW), sel_t,
                              gates.transpose(0, 1, 3, 2), o_cmp, batch, seq)


def _layer(x, layer, norm1_w, w_in, lb_table, hgrn_norm_w, pe, w1, w2, w_branch, w_out, norm2_w, w_ff1, w_ff2_bf,
           batch, seq):
    h = _rmsnorm(x, norm1_w, BF16)
    za = _ws_matmul(h, w_in, layer, F32, n_tiles=ZA_WIDTH // PROJ_TN, name="proj_in_a")
    zb = _ws_matmul(h, w_in, layer, F32, n_tiles=ZB_WIDTH // PROJ_TN, tile0=OFF_BGATE // PROJ_TN,
                    shift=GATE_COLS, name="proj_in_b")
    vt = za[:, OFF_AI:OFF_AI + HG_WIDTH].reshape(batch, seq, HG_HEADS, HG_DIM).transpose(0, 2, 3, 1)
    o_a = _hgrn2(za, vt, lb_table, hgrn_norm_w, layer, batch, seq)
    o_b = _nsa(za, pe, w1, w2, batch, seq)
    kt = zb[:, ZB_CK:ZB_CK + RET_QK_WIDTH].reshape(batch, seq, RET_HEADS, RET_QK).transpose(0, 2, 3, 1)
    o_c = _retention(zb, kt, batch, seq)
    merged = _branch_merge(o_a, o_b, o_c, w_branch, layer, zb)
    x = _ws_matmul(merged, w_out, layer, F32, n_tiles=D_MODEL // PROJ_TN, res=x, name="proj_out")
    h = _rmsnorm(x, norm2_w, BF16)
    u = _ws_matmul(h, w_ff1, layer, BF16, n_tiles=D_FF // PROJ_TN, relu2=True, name="ffn_up")
    return _matmul_residual(u, w_ff2_bf, layer, x, name="ffn_down")


def kernel(x, norm1_w, w_in, hgrn_lb_table, hgrn_norm_w, cmp_pe_k, cmp_pe_v, cmp_w1_k, cmp_w1_v, cmp_w2_k,
           cmp_w2_v, w_branch, w_out, norm2_w, w_ff1, w_ff2, final_norm_w):
    batch, seq, d = x.shape
    xf = x.reshape(batch * seq, d)
    w_ff2_bf = w_ff2.astype(BF16)
    for l in range(DEPTH):
        pe = jnp.stack([cmp_pe_k[l].reshape(1, -1), cmp_pe_v[l].reshape(1, -1)])
        w1 = jnp.stack([cmp_w1_k[l], cmp_w1_v[l]]).astype(BF16)
        w2 = jnp.stack([cmp_w2_k[l], cmp_w2_v[l]]).astype(BF16)
        xf = _layer(xf, l, norm1_w[l], w_in, hgrn_lb_table, hgrn_norm_w[l], pe, w1, w2, w_branch, w_out,
                    norm2_w[l], w_ff1, w_ff2_bf, batch, seq)
    return _rmsnorm(xf, final_norm_w, F32).reshape(batch, seq, d)
```

```python
import functools

import numpy as np
import jax
import jax.numpy as jnp
from jax import lax
from jax.experimental import pallas as pl
from jax.experimental.pallas import tpu as pltpu

F32 = jnp.float32
BF16 = jnp.bfloat16
HIGHEST = lax.Precision.HIGHEST

D_MODEL = 4096
DEPTH = 2
NORM_EPS = 1e-6
LANE = 128

HG_HEADS = 16
HG_DIM = 128
HG_WIDTH = HG_HEADS * HG_DIM
HG_CHUNK = 64
HG_SUB = 16
HG_HPB = 2
NSA_HEADS = 16
NSA_GROUPS = 4
NSA_REP = NSA_HEADS // NSA_GROUPS
NSA_DIM = 128
NSA_WIDTH = NSA_HEADS * NSA_DIM
NSA_KV_WIDTH = NSA_GROUPS * NSA_DIM
NSA_GW = NSA_REP * NSA_DIM
CMP_BLOCK = 32
CMP_STRIDE = 16
CMP_HIDDEN = 256
SLC_BLOCK = 64
SLC_SHIFT = 6
SLC_TOPK = 16
WIN_SIZE = 512
ATT_T = 128
RET_HEADS = 8
RET_QK = 128
RET_V = 256
RET_QK_WIDTH = RET_HEADS * RET_QK
RET_V_WIDTH = RET_HEADS * RET_V
RET_CHUNK = 128
MIX_WIDTH = HG_WIDTH + NSA_WIDTH + RET_V_WIDTH
D_FF = 4 * D_MODEL

PROJ_TN = 512
GATE_COLS = NSA_HEADS * 3
OFF_AQ = 0
OFF_AF = OFF_AQ + HG_WIDTH
OFF_AI = OFF_AF + HG_WIDTH
OFF_AG = OFF_AI + HG_WIDTH
OFF_BQ = OFF_AG + HG_WIDTH
OFF_BKC = OFF_BQ + NSA_WIDTH
OFF_BVC = OFF_BKC + NSA_KV_WIDTH
OFF_BKS = OFF_BVC + NSA_KV_WIDTH
OFF_BVS = OFF_BKS + NSA_KV_WIDTH
OFF_BKW = OFF_BVS + NSA_KV_WIDTH
OFF_BVW = OFF_BKW + NSA_KV_WIDTH
OFF_BGATE = OFF_BVW + NSA_KV_WIDTH
ZA_WIDTH = OFF_BGATE + PROJ_TN
ZB_CQ = 0
ZB_CK = ZB_CQ + RET_QK_WIDTH
ZB_CV = ZB_CK + RET_QK_WIDTH
ZB_CG = ZB_CV + RET_V_WIDTH
ZB_MA = ZB_CG + RET_V_WIDTH
ZB_MB = ZB_MA + D_MODEL
ZB_MC = ZB_MB + D_MODEL
ZB_WIDTH = ZB_MC + D_MODEL

NEG = -1e30
VMEM_LIMIT = 56 * 1024 * 1024


def _cparams(*sem):
    return pltpu.CompilerParams(dimension_semantics=sem, vmem_limit_bytes=VMEM_LIMIT)


def _sigmoid(x):
    return 1.0 / (1.0 + jnp.exp(-x))


def _silu(x):
    return x * _sigmoid(x)


def _dot(a, b, precision=None):
    return jnp.dot(a, b, preferred_element_type=F32, precision=precision)


def _dot_nt(a, b, precision=None):
    return lax.dot_general(a, b, (((1,), (1,)), ((), ())), preferred_element_type=F32, precision=precision)


def _dot_tn(a, b):
    return lax.dot_general(a, b, (((0,), (0,)), ((), ())), preferred_element_type=F32)


def _rmsnorm_kernel(x_ref, w_ref, o_ref):
    x = x_ref[...]
    y = x * lax.rsqrt(jnp.mean(x * x, axis=-1, keepdims=True) + NORM_EPS)
    o_ref[...] = (y * w_ref[...]).astype(o_ref.dtype)


def _rmsnorm(x, w, out_dtype, tm=256):
    m, d = x.shape
    return pl.pallas_call(
        _rmsnorm_kernel,
        grid=(m // tm,),
        in_specs=[pl.BlockSpec((tm, d), lambda i: (i, 0)), pl.BlockSpec((1, d), lambda i: (0, 0))],
        out_specs=pl.BlockSpec((tm, d), lambda i: (i, 0)),
        out_shape=jax.ShapeDtypeStruct((m, d), out_dtype),
        compiler_params=_cparams("parallel"),
        name="rmsnorm",
    )(x, w.reshape(1, d))


CAST_K = 512


def _slab_dot(a_ref, w_ref, wbf_ref, k0, transposed):
    sl = slice(k0, k0 + CAST_K)
    if transposed:
        w16 = w_ref[0, :, sl].astype(BF16)
        wbf_ref[:, sl] = w16
        return _dot_nt(a_ref[:, sl], w16)
    w16 = w_ref[0, sl, :].astype(BF16)
    wbf_ref[sl, :] = w16
    return _dot(a_ref[:, sl], w16)


def _ws_kernel(*refs, transposed, relu2, residual):
    a_ref, w_ref = refs[0], refs[1]
    res_ref = refs[2] if residual else None
    o_ref, wbf_ref = refs[-2], refs[-1]
    k = a_ref.shape[1]

    def finish(acc):
        if relu2:
            acc = jnp.maximum(acc, 0.0)
            acc = acc * acc
        if residual:
            acc = acc + res_ref[...]
        o_ref[...] = acc.astype(o_ref.dtype)

    @pl.when(pl.program_id(1) == 0)
    def _():
        acc = _slab_dot(a_ref, w_ref, wbf_ref, 0, transposed)
        for k0 in range(CAST_K, k, CAST_K):
            acc = acc + _slab_dot(a_ref, w_ref, wbf_ref, k0, transposed)
        finish(acc)

    @pl.when(pl.program_id(1) > 0)
    def _():
        finish(_dot_nt(a_ref[...], wbf_ref[...]) if transposed else _dot(a_ref[...], wbf_ref[...]))


def _ws_matmul(a, w, layer, out_dtype, *, n_tiles, col0=0, transposed=False, relu2=False, res=None, tm=1024,
               tn=PROJ_TN, name="ws_matmul"):
    m, k = a.shape
    if transposed:
        w_spec = pl.BlockSpec((pl.Element(1), pl.Element(tn), pl.Element(k)),
                              lambda j, i: (layer, pl.multiple_of(col0 + j * tn, 8), 0))
        scratch = pltpu.VMEM((tn, k), BF16)
    else:
        w_spec = pl.BlockSpec((1, k, tn), lambda j, i: (layer, 0, col0 // tn + j))
        scratch = pltpu.VMEM((k, tn), BF16)
    in_specs = [pl.BlockSpec((tm, k), lambda j, i: (i, 0)), w_spec]
    args = [a, w]
    if res is not None:
        in_specs.append(pl.BlockSpec((tm, tn), lambda j, i: (i, j)))
        args.append(res)
    return pl.pallas_call(
        functools.partial(_ws_kernel, transposed=transposed, relu2=relu2, residual=res is not None),
        grid=(n_tiles, m // tm),
        in_specs=in_specs,
        out_specs=pl.BlockSpec((tm, tn), lambda j, i: (i, j)),
        out_shape=jax.ShapeDtypeStruct((m, n_tiles * tn), out_dtype),
        scratch_shapes=[scratch],
        compiler_params=_cparams("arbitrary", "arbitrary"),
        name=name,
    )(*args)


def _mm_res_kernel(a_ref, b_ref, r_ref, o_ref, acc_ref, *, nk):
    kk = pl.program_id(2)
    part = _dot(a_ref[...], b_ref[0])

    @pl.when(kk == 0)
    def _():
        acc_ref[...] = part

    @pl.when(kk > 0)
    def _():
        acc_ref[...] += part

    @pl.when(kk == nk - 1)
    def _():
        o_ref[...] = r_ref[...] + acc_ref[...]


def _matmul_residual(a, b, layer, res, *, tm=1024, tn=512, tk=4096, name="matmul_res"):
    m, k = a.shape
    n = b.shape[2]
    nk = k // tk
    return pl.pallas_call(
        functools.partial(_mm_res_kernel, nk=nk),
        grid=(m // tm, n // tn, nk),
        in_specs=[pl.BlockSpec((tm, tk), lambda i, j, q: (i, q)),
                  pl.BlockSpec((1, tk, tn), lambda i, j, q: (layer, q, j)),
                  pl.BlockSpec((tm, tn), lambda i, j, q: (i, j))],
        out_specs=pl.BlockSpec((tm, tn), lambda i, j, q: (i, j)),
        out_shape=jax.ShapeDtypeStruct((m, n), F32),
        scratch_shapes=[pltpu.VMEM((tm, tn), F32)],
        compiler_params=_cparams("parallel", "arbitrary", "arbitrary"),
        name=name,
    )(a, b, res)


def _merge_kernel(oa_ref, ob_ref, oc_ref, wa_ref, wb_ref, wc_ref, ga_ref, gb_ref, gc_ref, o_ref,
                  wa16, wb16, wc16):
    branches = ((ga_ref, oa_ref, wa_ref, wa16), (gb_ref, ob_ref, wb_ref, wb16), (gc_ref, oc_ref, wc_ref, wc16))

    @pl.when(pl.program_id(1) == 0)
    def _():
        acc = None
        for g_ref, x_ref, w_ref, w16 in branches:
            prod = _slab_dot(x_ref, w_ref, w16, 0, False)
            for k0 in range(CAST_K, x_ref.shape[1], CAST_K):
                prod = prod + _slab_dot(x_ref, w_ref, w16, k0, False)
            term = _sigmoid(g_ref[...]) * prod
            acc = term if acc is None else acc + term
        o_ref[...] = acc.astype(o_ref.dtype)

    @pl.when(pl.program_id(1) > 0)
    def _():
        acc = None
        for g_ref, x_ref, w_ref, w16 in branches:
            term = _sigmoid(g_ref[...]) * _dot(x_ref[...], w16[...])
            acc = term if acc is None else acc + term
        o_ref[...] = acc.astype(o_ref.dtype)


def _branch_merge(o_a, o_b, o_c, w_branch, layer, zb, *, tm=512, tn=512):
    m, kb = o_a.shape
    o_spec = pl.BlockSpec((tm, kb), lambda j, i: (i, 0))

    def w_spec(r):
        return pl.BlockSpec((1, kb, tn), lambda j, i: (layer, r, j))

    def g_spec(off):
        return pl.BlockSpec((tm, tn), lambda j, i: (i, off // tn + j))

    return pl.pallas_call(
        _merge_kernel,
        grid=(D_MODEL // tn, m // tm),
        in_specs=[o_spec, o_spec, o_spec, w_spec(0), w_spec(1), w_spec(2),
                  g_spec(ZB_MA), g_spec(ZB_MB), g_spec(ZB_MC)],
        out_specs=pl.BlockSpec((tm, tn), lambda j, i: (i, j)),
        out_shape=jax.ShapeDtypeStruct((m, D_MODEL), BF16),
        scratch_shapes=[pltpu.VMEM((kb, tn), BF16)] * 3,
        compiler_params=_cparams("arbitrary", "arbitrary"),
        name="branch_merge",
    )(o_a, o_b, o_c, w_branch, w_branch, w_branch, zb, zb, zb)


def _hgrn_chunk(q_in, f_in, v, g, lb, nw, st, tri):
    c = HG_CHUNK
    qc = _silu(q_in) * (HG_DIM ** -0.5)
    f = lb + (1.0 - lb) * _sigmoid(f_in)
    kk = 1.0 - f
    b = _dot(tri, jnp.log(f), precision=HIGHEST)
    o = _dot_nt((qc * jnp.exp(b)).astype(BF16), st.astype(BF16))
    b_last = b[c - 1:c, :]
    kdec = kk * jnp.exp(b_last - b)
    v16 = v.astype(BF16)
    st_new = st * jnp.exp(b_last) + _dot_tn(v16, kdec.astype(BF16))
    row = lax.broadcasted_iota(jnp.int32, (HG_SUB, HG_DIM), 0)
    col = lax.broadcasted_iota(jnp.int32, (HG_SUB, c), 1)
    parts = []
    for a in range(c // HG_SUB):
        lo = a * HG_SUB
        ba = b[lo:lo + HG_SUB, :]
        qa = qc[lo:lo + HG_SUB, :]
        acc = jnp.zeros((HG_SUB, HG_DIM), F32)
        if a > 0:
            bref = b[lo:lo + 1, :]
            qn = qa * jnp.exp(ba - bref)
            kn = kk * jnp.exp(jnp.minimum(bref - b, 0.0))
            att = _dot_nt(qn.astype(BF16), kn.astype(BF16))
            acc = _dot(jnp.where(col < lo, att, 0.0).astype(BF16), v16)
        for j in range(HG_SUB):
            jj = lo + j
            d = jnp.where(row >= j, ba - b[jj:jj + 1, :], NEG)
            w = jnp.sum(qa * kk[jj:jj + 1, :] * jnp.exp(d), axis=-1, keepdims=True)
            acc = acc + w * v[jj:jj + 1, :]
        parts.append(acc)
    o = o + jnp.concatenate(parts, axis=0)
    o = o * lax.rsqrt(jnp.mean(o * o, axis=-1, keepdims=True) + NORM_EPS) * nw * _silu(g)
    return o, st_new


def _hgrn_kernel(tab_ref, nw_ref, q_ref, f_ref, v_ref, g_ref, o_ref, st_ref, *, layer, rows):
    @pl.when(pl.program_id(2) == 0)
    def _():
        st_ref[...] = jnp.zeros_like(st_ref)

    tab = tab_ref[...]
    e = jnp.exp(tab - jnp.max(tab, axis=0, keepdims=True))
    p = e / jnp.sum(e, axis=0, keepdims=True)
    lb = jnp.sum(p[:layer + 1, :], axis=0, keepdims=True) - p[0:1, :]
    nw = nw_ref[...]
    r_i = lax.broadcasted_iota(jnp.int32, (HG_CHUNK, HG_CHUNK), 0)
    c_i = lax.broadcasted_iota(jnp.int32, (HG_CHUNK, HG_CHUNK), 1)
    tri = (r_i >= c_i).astype(F32)
    sts = [st_ref[hh] for hh in range(HG_HPB)]
    for ch in range(rows // HG_CHUNK):
        sl = slice(ch * HG_CHUNK, (ch + 1) * HG_CHUNK)
        outs = []
        for hh in range(HG_HPB):
            hs = slice(hh * HG_DIM, (hh + 1) * HG_DIM)
            o, sts[hh] = _hgrn_chunk(q_ref[sl, hs], f_ref[sl, hs], v_ref[sl, hs], g_ref[sl, hs], lb[:, hs], nw,
                                     sts[hh], tri)
            outs.append(o)
        o_ref[sl, :] = jnp.concatenate(outs, axis=1).astype(o_ref.dtype)
    for hh in range(HG_HPB):
        st_ref[hh] = sts[hh]


def _hgrn2(za, lb_table, norm_w, layer, batch, seq, rows=256):
    nrow = seq // rows
    width = HG_HPB * HG_DIM

    def col(off):
        return pl.BlockSpec((rows, width), lambda b, h, c: (b * nrow + c, off // width + h))

    return pl.pallas_call(
        functools.partial(_hgrn_kernel, layer=layer, rows=rows),
        grid=(batch, HG_HEADS // HG_HPB, nrow),
        in_specs=[pl.BlockSpec((DEPTH, width), lambda b, h, c: (0, h)),
                  pl.BlockSpec((1, HG_DIM), lambda b, h, c: (0, 0)),
                  col(OFF_AQ), col(OFF_AF), col(OFF_AI), col(OFF_AG)],
        out_specs=pl.BlockSpec((rows, width), lambda b, h, c: (b * nrow + c, h)),
        out_shape=jax.ShapeDtypeStruct((batch * seq, HG_WIDTH), BF16),
        scratch_shapes=[pltpu.VMEM((HG_HPB, HG_DIM, HG_DIM), F32)],
        compiler_params=_cparams("parallel", "parallel", "arbitrary"),
        name="hgrn2",
    )(lb_table, norm_w.reshape(1, HG_DIM), za, za, za, za)


def _ret_kernel(lg_ref, q_ref, k_ref, v_ref, g_ref, o_ref, s_ref, *, rows):
    @pl.when(pl.program_id(2) == 0)
    def _():
        s_ref[...] = jnp.zeros_like(s_ref)

    c = RET_CHUNK
    lg_v = lg_ref[0]
    lg_k = lg_v[:, :c]
    pos_r = lax.broadcasted_iota(jnp.int32, (c, RET_V), 0).astype(F32)
    query_decay = jnp.exp(lg_v * (pos_r + 1.0))
    rel = (lax.broadcasted_iota(jnp.int32, (c, c), 0) - lax.broadcasted_iota(jnp.int32, (c, c), 1)).astype(F32)
    intra_decay = jnp.where(rel >= 0, jnp.exp(lg_k * jnp.maximum(rel, 0.0)), 0.0)
    key_decay = jnp.exp(lg_k * (c - 1.0 - pos_r[:, :RET_QK]))
    chunk_decay = jnp.exp(lg_v * float(c))
    s = s_ref[...]
    for ch in range(rows // c):
        sl = slice(ch * c, (ch + 1) * c)
        q = q_ref[sl, :].astype(BF16)
        k = k_ref[sl, :] * (RET_QK ** -0.5)
        v = v_ref[sl, :].astype(BF16)
        inter = _dot(q, s.astype(BF16)) * query_decay
        scores = _dot_nt(q, k.astype(BF16)) * intra_decay
        o = inter + _dot(scores.astype(BF16), v)
        s = chunk_decay * s + _dot_tn((k * key_decay).astype(BF16), v)
        mu = jnp.mean(o, axis=-1, keepdims=True)
        oc = o - mu
        o = oc * lax.rsqrt(jnp.mean(oc * oc, axis=-1, keepdims=True) + NORM_EPS)
        o_ref[sl, :] = (o * _silu(g_ref[sl, :])).astype(o_ref.dtype)
    s_ref[...] = s


def _retention(zb, batch, seq, rows=256):
    nrow = seq // rows
    log_gamma = jnp.log(1.0 - jnp.exp2(-5.0 - jnp.arange(RET_HEADS, dtype=F32)))
    lg = jnp.broadcast_to(log_gamma[:, None, None], (RET_HEADS, 1, RET_V))
    return pl.pallas_call(
        functools.partial(_ret_kernel, rows=rows),
        grid=(batch, RET_HEADS, nrow),
        in_specs=[pl.BlockSpec((1, 1, RET_V), lambda b, h, c: (h, 0, 0)),
                  pl.BlockSpec((rows, RET_QK), lambda b, h, c: (b * nrow + c, ZB_CQ // RET_QK + h)),
                  pl.BlockSpec((rows, RET_QK), lambda b, h, c: (b * nrow + c, ZB_CK // RET_QK + h)),
                  pl.BlockSpec((rows, RET_V), lambda b, h, c: (b * nrow + c, ZB_CV // RET_V + h)),
                  pl.BlockSpec((rows, RET_V), lambda b, h, c: (b * nrow + c, ZB_CG // RET_V + h))],
        out_specs=pl.BlockSpec((rows, RET_V), lambda b, h, c: (b * nrow + c, h)),
        out_shape=jax.ShapeDtypeStruct((batch * seq, RET_V_WIDTH), BF16),
        scratch_shapes=[pltpu.VMEM((RET_QK, RET_V), F32)],
        compiler_params=_cparams("parallel", "parallel", "arbitrary"),
        name="retention",
    )(lg, zb, zb, zb, zb)


def _compress_kernel(y_ref, pe_ref, w1_ref, w2_ref, o_ref):
    half = CMP_STRIDE * NSA_DIM
    y = y_ref[0, 0]
    pe = pe_ref[0]
    top = _dot((y + pe[:, :half]).astype(BF16), w1_ref[0, :half, :])
    bot = _dot((y + pe[:, half:]).astype(BF16), w1_ref[0, half:, :])
    nc = y.shape[0]
    h = top + pltpu.roll(bot, nc - 1, 0)
    o_ref[0, 0] = _dot(_silu(h).astype(BF16), w2_ref[0])


def _compress(y, pe, w1, w2):
    _, bg, nc, half = y.shape
    return pl.pallas_call(
        _compress_kernel,
        grid=(2, bg),
        in_specs=[pl.BlockSpec((1, 1, nc, half), lambda s, i: (s, i, 0, 0)),
                  pl.BlockSpec((1, 1, 2 * half), lambda s, i: (s, 0, 0)),
                  pl.BlockSpec((1, 2 * half, CMP_HIDDEN), lambda s, i: (s, 0, 0)),
                  pl.BlockSpec((1, CMP_HIDDEN, NSA_DIM), lambda s, i: (s, 0, 0))],
        out_specs=pl.BlockSpec((1, 1, nc, NSA_DIM), lambda s, i: (s, i, 0, 0)),
        out_shape=jax.ShapeDtypeStruct((2, bg, nc, NSA_DIM), F32),
        compiler_params=_cparams("parallel", "parallel"),
        name="nsa_compress",
    )(y, pe, w1, w2)


def _slope_table():
    slopes = np.exp2(-8.0 * np.arange(1, NSA_HEADS + 1, dtype=np.float64) / NSA_HEADS).astype(np.float32)
    tab = np.zeros((NSA_GROUPS, 8, 512), np.float32)
    tab[:, :NSA_REP, :] = slopes.reshape(NSA_GROUPS, NSA_REP)[:, :, None]
    return jnp.asarray(tab)


def _cmp_kernel(slope_ref, wmap_ref, q_ref, kc_ref, vc_ref, gate_ref, o_ref, sel_ref, used_ref, score_ref,
                *, tq, nc, nslc):
    t0 = pl.program_id(2) * tq
    kc = kc_ref[0, 0]
    vc = vc_ref[0, 0].astype(BF16)
    gates = _sigmoid(gate_ref[0, 0])
    t_i = t0 + lax.broadcasted_iota(jnp.int32, (tq, nc), 0)
    n_i = lax.broadcasted_iota(jnp.int32, (tq, nc), 1)
    dist = t_i - (n_i * CMP_STRIDE + CMP_BLOCK - 1)
    valid = (dist >= 0) & (n_i < nc - 1)
    distf = dist.astype(F32)
    imp = jnp.zeros((tq, nc), F32)
    for r in range(NSA_REP):
        q = q_ref[:, r * NSA_DIM:(r + 1) * NSA_DIM]
        slope = slope_ref[0, r:r + 1, :nc]
        s = _dot_nt(q, kc, precision=HIGHEST) * (NSA_DIM ** -0.5) - slope * distf
        s = jnp.where(valid, s, NEG)
        m = jnp.max(s, axis=-1, keepdims=True)
        e = jnp.where(valid, jnp.exp(s - m), 0.0)
        den = jnp.sum(e, axis=-1, keepdims=True)
        p = e / jnp.where(den > 0, den, 1.0)
        o_ref[:, r * NSA_DIM:(r + 1) * NSA_DIM] = gates[:, 3 * r:3 * r + 1] * _dot(p.astype(BF16), vc)
        imp = imp + p
    imp_t = _dot_nt(wmap_ref[...], imp, precision=HIGHEST)
    blk = lax.broadcasted_iota(jnp.int32, (nslc, tq), 0)
    cur = jnp.right_shift(t0 + lax.broadcasted_iota(jnp.int32, (nslc, tq), 1), SLC_SHIFT)
    forced = (blk == 0) | (blk == cur) | (blk == cur - 1)
    score = jnp.where(blk > cur, -jnp.inf, jnp.where(forced, jnp.inf, imp_t))
    score_ref[...] = score
    rank = jnp.zeros((nslc, tq), jnp.int32)
    for s_i in range(nslc):
        other = score_ref[s_i:s_i + 1, :]
        beats = (other > score) | ((other == score) & (blk > s_i))
        rank = rank + beats.astype(jnp.int32)
    sel = ((rank < min(SLC_TOPK, nslc)) & (blk <= cur)).astype(F32)
    sel_ref[0, 0] = sel
    for a in range(tq // ATT_T):
        used_ref[0, 0, 0, :, a:a + 1] = jnp.max(sel[:, a * ATT_T:(a + 1) * ATT_T], axis=1, keepdims=True)


def _cmp_attention(za, kvc, gates, batch, seq, tq=256):
    nc = seq // CMP_STRIDE
    nslc = seq // SLC_BLOCK
    nq = seq // tq
    c_start = np.arange(nc) * CMP_STRIDE
    s_start = np.arange(nslc) * SLC_BLOCK
    overlap = np.clip(np.minimum(c_start[:, None] + CMP_BLOCK, s_start[None, :] + SLC_BLOCK)
                      - np.maximum(c_start[:, None], s_start[None, :]), 0, None)
    wmap_t = (overlap.astype(np.float32) / CMP_STRIDE).T.copy()
    wmap_t[:, nc - 1] = 0.0
    sub = tq // ATT_T
    return pl.pallas_call(
        functools.partial(_cmp_kernel, tq=tq, nc=nc, nslc=nslc),
        grid=(batch, NSA_GROUPS, nq),
        in_specs=[pl.BlockSpec((1, 8, 512), lambda b, g, i: (g, 0, 0)),
                  pl.BlockSpec((nslc, nc), lambda b, g, i: (0, 0)),
                  pl.BlockSpec((tq, NSA_GW), lambda b, g, i: (b * nq + i, OFF_BQ // NSA_GW + g)),
                  pl.BlockSpec((1, 1, nc, NSA_DIM), lambda b, g, i: (0, b * NSA_GROUPS + g, 0, 0)),
                  pl.BlockSpec((1, 1, nc, NSA_DIM), lambda b, g, i: (1, b * NSA_GROUPS + g, 0, 0)),
                  pl.BlockSpec((1, 1, tq, 3 * NSA_REP), lambda b, g, i: (b, g, i, 0))],
        out_specs=[pl.BlockSpec((tq, NSA_GW), lambda b, g, i: (b * nq + i, g)),
                   pl.BlockSpec((1, 1, nslc, tq), lambda b, g, i: (b, g, 0, i)),
                   pl.BlockSpec((1, 1, 1, nslc, sub), lambda b, g, i: (b, g, i, 0, 0))],
        out_shape=[jax.ShapeDtypeStruct((batch * seq, NSA_WIDTH), F32),
                   jax.ShapeDtypeStruct((batch, NSA_GROUPS, nslc, seq), F32),
                   jax.ShapeDtypeStruct((batch, NSA_GROUPS, nq, nslc, sub), F32)],
        scratch_shapes=[pltpu.VMEM((nslc, tq), F32)],
        compiler_params=_cparams("parallel", "parallel", "arbitrary"),
        name="nsa_cmp_select",
    )(_slope_table(), jnp.asarray(wmap_t), za, kvc, kvc, gates)


ATT_SUPER = 4
ALIBI_FEATS = 6


def _alibi_query_features():
    slopes = np.exp2(-8.0 * np.arange(1, NSA_HEADS + 1, dtype=np.float64) / NSA_HEADS).astype(np.float32)

    def top_bits(x):
        return (x.view(np.uint32) & np.uint32(0xFFFF0000)).view(np.float32)

    s1 = top_bits(slopes)
    r1 = slopes - s1
    s2 = top_bits(r1)
    s3 = r1 - s2
    rows = np.stack([-64.0 * s1, -64.0 * s2, -64.0 * s3, -s1, -s2, -s3]).astype(np.float32)
    feat = np.zeros((NSA_GROUPS, NSA_DIM, NSA_REP, ATT_T), np.float32)
    feat[:, :ALIBI_FEATS] = rows.reshape(ALIBI_FEATS, NSA_GROUPS, NSA_REP).transpose(1, 0, 2)[:, :, :, None]
    return jnp.asarray(feat.reshape(NSA_GROUPS, NSA_DIM, NSA_GW), dtype=BF16)


def _attend(k_ref, v_ref, chunks, valids, qi, qt_aug, kf_static, kf_ind, m_ref, l_ref, acc_ref):
    t = ATT_T
    ks, vs, mbs = [], [], []
    for c, valid in zip(chunks, valids):
        k0 = pl.multiple_of(c * t, t)
        kfeat = (kf_static + (2 * (qi - c)).astype(F32) * kf_ind).astype(BF16)
        ks.append(jnp.concatenate([k_ref[pl.ds(k0, t), :].astype(BF16), kfeat], axis=1))
        vs.append(v_ref[pl.ds(k0, t), :].astype(BF16))
        mbs.append(jnp.where(valid, 0.0, NEG))
    st = _dot(jnp.concatenate(ks, axis=0), qt_aug)
    mbias = jnp.concatenate(mbs, axis=0)
    ps, alphas = [], []
    for r in range(NSA_REP):
        cs = slice(r * t, (r + 1) * t)
        s = st[:, cs] + mbias
        m_old = m_ref[:, cs]
        m_new = jnp.maximum(m_old, jnp.max(s, axis=0, keepdims=True))
        alphas.append(jnp.exp(m_old - m_new))
        p = jnp.exp(s - m_new)
        l_ref[:, cs] = alphas[-1] * l_ref[:, cs] + jnp.sum(p, axis=0, keepdims=True)
        m_ref[:, cs] = m_new
        ps.append(p.astype(BF16))
    acc_ref[...] = (jnp.concatenate(alphas, axis=1) * acc_ref[...]
                    + _dot_tn(jnp.concatenate(vs, axis=0), jnp.concatenate(ps, axis=1)))


def _slc_win_kernel(ids_ref, cnt_ref, qfeat_ref, q_ref, ks_ref, vs_ref, kw_ref, vw_ref, sel_ref, gate_ref,
                    ocmp_ref, o_ref, m_ref, l_ref, acc_ref, *, nq):
    t = ATT_T
    b, g, qi = pl.program_id(0), pl.program_id(1), pl.program_id(2)
    t0 = qi * t
    qt = jnp.concatenate([q_ref[:, r * NSA_DIM:(r + 1) * NSA_DIM].T for r in range(NSA_REP)], axis=1)
    qt_aug = jnp.concatenate([(qt * (NSA_DIM ** -0.5)).astype(BF16), qfeat_ref[0]], axis=0)
    key_i = lax.broadcasted_iota(jnp.int32, (t, t), 0)
    qry_i = lax.broadcasted_iota(jnp.int32, (t, t), 1)
    back = (t - 1) - key_i
    kf_ind = (qry_i < 3).astype(F32)
    kf_static = jnp.where(qry_i < 3, jnp.right_shift(back, SLC_SHIFT),
                          jnp.where(qry_i < ALIBI_FEATS, jnp.bitwise_and(back, SLC_BLOCK - 1), 0)).astype(F32)

    def reset():
        m_ref[...] = jnp.full_like(m_ref, NEG)
        l_ref[...] = jnp.zeros_like(l_ref)
        acc_ref[...] = jnp.zeros_like(acc_ref)

    reset()
    tile = (b * NSA_GROUPS + g) * nq + qi
    cnt = cnt_ref[tile]
    half = SLC_BLOCK
    never = jnp.int32(2 ** 30)

    def slc_body(s, carry):
        chunks, valids = [], []
        for u in range(ATT_SUPER):
            e = s * ATT_SUPER + u
            c = ids_ref[tile * nq + jnp.minimum(e, cnt - 1)]
            lo = jnp.broadcast_to(sel_ref[0, 0, pl.ds(2 * c, 1), :], (half, t))
            hi = jnp.broadcast_to(sel_ref[0, 0, pl.ds(2 * c + 1, 1), :], (half, t))
            picked = jnp.concatenate([lo, hi], axis=0) > 0.5
            dist = (t0 + qry_i) - (c * t + key_i)
            chunks.append(c)
            valids.append(picked & (dist >= jnp.where(e < cnt, 0, never)))
        _attend(ks_ref, vs_ref, chunks, valids, qi, qt_aug, kf_static, kf_ind, m_ref, l_ref, acc_ref)
        return carry

    lax.fori_loop(0, jnp.right_shift(cnt + (ATT_SUPER - 1), 2), slc_body, 0)
    o_slc = acc_ref[...] * (1.0 / l_ref[...])

    reset()
    chunks, valids = [], []
    for i in range(WIN_SIZE // t + 1):
        c = jnp.maximum(qi - i, 0)
        dist = (t0 + qry_i) - (c * t + key_i)
        chunks.append(c)
        valids.append((dist >= jnp.where(qi - i >= 0, 0, never)) & (dist < WIN_SIZE))
    _attend(kw_ref, vw_ref, chunks, valids, qi, qt_aug, kf_static, kf_ind, m_ref, l_ref, acc_ref)
    o_win = acc_ref[...] * (1.0 / l_ref[...])

    gates = _sigmoid(gate_ref[0, 0])
    for r in range(NSA_REP):
        cs = slice(r * t, (r + 1) * t)
        o_t = gates[3 * r + 1:3 * r + 2, :] * o_slc[:, cs] + gates[3 * r + 2:3 * r + 3, :] * o_win[:, cs]
        o_ref[:, cs] = (o_t.T + ocmp_ref[:, cs]).astype(o_ref.dtype)


def _slc_win_attention(ids, cnt, za, sel_t, gates_t, o_cmp, batch, seq):
    t = ATT_T
    nslc = seq // SLC_BLOCK
    nq = seq // t

    def kv(off):
        return pl.BlockSpec((seq, NSA_DIM), lambda b, g, i, *_: (b, off // NSA_DIM + g))

    grid_spec = pltpu.PrefetchScalarGridSpec(
        num_scalar_prefetch=2,
        grid=(batch, NSA_GROUPS, nq),
        in_specs=[pl.BlockSpec((1, NSA_DIM, NSA_GW), lambda b, g, i, *_: (g, 0, 0)),
                  pl.BlockSpec((t, NSA_GW), lambda b, g, i, *_: (b * nq + i, OFF_BQ // NSA_GW + g)),
                  kv(OFF_BKS), kv(OFF_BVS), kv(OFF_BKW), kv(OFF_BVW),
                  pl.BlockSpec((1, 1, nslc, t), lambda b, g, i, *_: (b, g, 0, i)),
                  pl.BlockSpec((1, 1, 3 * NSA_REP, t), lambda b, g, i, *_: (b, g, 0, i)),
                  pl.BlockSpec((t, NSA_GW), lambda b, g, i, *_: (b * nq + i, g))],
        out_specs=pl.BlockSpec((t, NSA_GW), lambda b, g, i, *_: (b * nq + i, g)),
        scratch_shapes=[pltpu.VMEM((1, NSA_GW), F32), pltpu.VMEM((1, NSA_GW), F32),
                        pltpu.VMEM((NSA_DIM, NSA_GW), F32)])
    return pl.pallas_call(
        functools.partial(_slc_win_kernel, nq=nq),
        grid_spec=grid_spec,
        out_shape=jax.ShapeDtypeStruct((batch * seq, NSA_WIDTH), BF16),
        compiler_params=_cparams("parallel", "parallel", "arbitrary"),
        name="nsa_slc_win",
    )(ids, cnt, _alibi_query_features(), za, za, za, za, za, sel_t, gates_t, o_cmp)


def _nsa(za, pe, w1, w2, batch, seq):
    nc = seq // CMP_STRIDE
    t = ATT_T
    nq = seq // t

    def grouped(off):
        return za[:, off:off + NSA_KV_WIDTH].reshape(batch, seq, NSA_GROUPS, NSA_DIM).transpose(0, 2, 1, 3)

    def blocks(off):
        return grouped(off).reshape(batch * NSA_GROUPS, nc, CMP_STRIDE * NSA_DIM)

    kvc = _compress(jnp.stack([blocks(OFF_BKC), blocks(OFF_BVC)]), pe, w1, w2)
    gates = (za[:, OFF_BGATE:OFF_BGATE + GATE_COLS].reshape(batch, seq, NSA_GROUPS, 3 * NSA_REP)
             .transpose(0, 2, 1, 3))
    o_cmp, sel_t, used = _cmp_attention(za, kvc, gates, batch, seq)
    nslc = seq // SLC_BLOCK
    used = used.transpose(0, 1, 2, 4, 3).reshape(batch * NSA_GROUPS * nq, nslc // 2, 2)
    unused = (jnp.max(used, axis=-1) < 0.5).astype(jnp.int32)
    ids = jnp.argsort(unused, axis=-1, stable=True).astype(jnp.int32).reshape(-1)
    cnt = (nslc // 2 - jnp.sum(unused, axis=-1)).astype(jnp.int32)
    return _slc_win_attention(ids, cnt, za, sel_t, gates.transpose(0, 1, 3, 2), o_cmp, batch, seq)


def _layer(x, layer, norm1_w, w_in_t, lb_table, hgrn_norm_w, pe, w1, w2, w_branch, w_out, norm2_w, w_ff1, w_ff2_bf,
           batch, seq):
    h = _rmsnorm(x, norm1_w, BF16)
    za = _ws_matmul(h, w_in_t, layer, F32, n_tiles=ZA_WIDTH // PROJ_TN, transposed=True, name="proj_in_a")
    zb = _ws_matmul(h, w_in_t, layer, F32, n_tiles=ZB_WIDTH // PROJ_TN, col0=OFF_BGATE + GATE_COLS,
                    transposed=True, name="proj_in_b")
    o_a = _hgrn2(za, lb_table, hgrn_norm_w, layer, batch, seq)
    o_b = _nsa(za, pe, w1, w2, batch, seq)
    o_c = _retention(zb, batch, seq)
    merged = _branch_merge(o_a, o_b, o_c, w_branch, layer, zb)
    x = _ws_matmul(merged, w_out, layer, F32, n_tiles=D_MODEL // PROJ_TN, res=x, name="proj_out")
    h = _rmsnorm(x, norm2_w, BF16)
    u = _ws_matmul(h, w_ff1, layer, BF16, n_tiles=D_FF // PROJ_TN, relu2=True, name="ffn_up")
    return _matmul_residual(u, w_ff2_bf, layer, x, name="ffn_down")


def kernel(x, norm1_w, w_in, hgrn_lb_table, hgrn_norm_w, cmp_pe_k, cmp_pe_v, cmp_w1_k, cmp_w1_v, cmp_w2_k,
           cmp_w2_v, w_branch, w_out, norm2_w, w_ff1, w_ff2, final_norm_w):
    batch, seq, d = x.shape
    xf = x.reshape(batch * seq, d)
    w_ff2_bf = w_ff2.astype(BF16)
    w_in_t = jnp.swapaxes(w_in, 1, 2)
    for l in range(DEPTH):
        pe = jnp.stack([cmp_pe_k[l].reshape(1, -1), cmp_pe_v[l].reshape(1, -1)])
        w1 = jnp.stack([cmp_w1_k[l], cmp_w1_v[l]]).astype(BF16)
        w2 = jnp.stack([cmp_w2_k[l], cmp_w2_v[l]]).astype(BF16)
        xf = _layer(xf, l, norm1_w[l], w_in_t, hgrn_lb_table, hgrn_norm_w[l], pe, w1, w2, w_branch, w_out,
                    norm2_w[l], w_ff1, w_ff2_bf, batch, seq)
    return _rmsnorm(xf, final_norm_w, F32).reshape(batch, seq, d)
```

```python
import functools

import numpy as np
import jax
import jax.numpy as jnp
from jax import lax
from jax.experimental import pallas as pl
from jax.experimental.pallas import tpu as pltpu

F32 = jnp.float32
BF16 = jnp.bfloat16
HIGHEST = lax.Precision.HIGHEST

D_MODEL = 4096
DEPTH = 2
NORM_EPS = 1e-6
LANE = 128

HG_HEADS = 16
HG_DIM = 128
HG_WIDTH = HG_HEADS * HG_DIM
HG_CHUNK = 64
HG_SUB = 16
HG_HPB = 2
NSA_HEADS = 16
NSA_GROUPS = 4
NSA_REP = NSA_HEADS // NSA_GROUPS
NSA_DIM = 128
NSA_WIDTH = NSA_HEADS * NSA_DIM
NSA_KV_WIDTH = NSA_GROUPS * NSA_DIM
NSA_GW = NSA_REP * NSA_DIM
CMP_BLOCK = 32
CMP_STRIDE = 16
CMP_HIDDEN = 256
SLC_BLOCK = 64
SLC_SHIFT = 6
SLC_TOPK = 16
WIN_SIZE = 512
ATT_T = 128
RET_HEADS = 8
RET_QK = 128
RET_V = 256
RET_QK_WIDTH = RET_HEADS * RET_QK
RET_V_WIDTH = RET_HEADS * RET_V
RET_CHUNK = 128
MIX_WIDTH = HG_WIDTH + NSA_WIDTH + RET_V_WIDTH
D_FF = 4 * D_MODEL

PROJ_TN = 512
GATE_COLS = NSA_HEADS * 3
OFF_AQ = 0
OFF_AF = OFF_AQ + HG_WIDTH
OFF_AI = OFF_AF + HG_WIDTH
OFF_AG = OFF_AI + HG_WIDTH
OFF_BQ = OFF_AG + HG_WIDTH
OFF_BKC = OFF_BQ + NSA_WIDTH
OFF_BVC = OFF_BKC + NSA_KV_WIDTH
OFF_BKS = OFF_BVC + NSA_KV_WIDTH
OFF_BVS = OFF_BKS + NSA_KV_WIDTH
OFF_BKW = OFF_BVS + NSA_KV_WIDTH
OFF_BVW = OFF_BKW + NSA_KV_WIDTH
OFF_BGATE = OFF_BVW + NSA_KV_WIDTH
ZA_WIDTH = OFF_BGATE + PROJ_TN
ZB_CQ = 0
ZB_CK = ZB_CQ + RET_QK_WIDTH
ZB_CV = ZB_CK + RET_QK_WIDTH
ZB_CG = ZB_CV + RET_V_WIDTH
ZB_MA = ZB_CG + RET_V_WIDTH
ZB_MB = ZB_MA + D_MODEL
ZB_MC = ZB_MB + D_MODEL
ZB_WIDTH = ZB_MC + D_MODEL

NEG = -1e30
VMEM_LIMIT = 56 * 1024 * 1024


def _cparams(*sem):
    return pltpu.CompilerParams(dimension_semantics=sem, vmem_limit_bytes=VMEM_LIMIT)


def _sigmoid(x):
    return 1.0 / (1.0 + jnp.exp(-x))


def _silu(x):
    return x * _sigmoid(x)


def _dot(a, b, precision=None):
    return jnp.dot(a, b, preferred_element_type=F32, precision=precision)


def _dot_nt(a, b, precision=None):
    return lax.dot_general(a, b, (((1,), (1,)), ((), ())), preferred_element_type=F32, precision=precision)


def _dot_tn(a, b):
    return lax.dot_general(a, b, (((0,), (0,)), ((), ())), preferred_element_type=F32)


def _rmsnorm_kernel(x_ref, w_ref, o_ref):
    x = x_ref[...]
    y = x * lax.rsqrt(jnp.mean(x * x, axis=-1, keepdims=True) + NORM_EPS)
    o_ref[...] = (y * w_ref[...]).astype(o_ref.dtype)


def _rmsnorm(x, w, out_dtype, tm=256):
    m, d = x.shape
    return pl.pallas_call(
        _rmsnorm_kernel,
        grid=(m // tm,),
        in_specs=[pl.BlockSpec((tm, d), lambda i: (i, 0)), pl.BlockSpec((1, d), lambda i: (0, 0))],
        out_specs=pl.BlockSpec((tm, d), lambda i: (i, 0)),
        out_shape=jax.ShapeDtypeStruct((m, d), out_dtype),
        compiler_params=_cparams("parallel"),
        name="rmsnorm",
    )(x, w.reshape(1, d))


CAST_K = 512


def _cast_dot(a_ref, w_ref, transposed):
    if w_ref.dtype == BF16:
        return _dot_nt(a_ref[...], w_ref[0]) if transposed else _dot(a_ref[...], w_ref[0])
    acc = None
    for k0 in range(0, a_ref.shape[1], CAST_K):
        sl = slice(k0, k0 + CAST_K)
        if transposed:
            part = _dot_nt(a_ref[:, sl], w_ref[0, :, sl].astype(BF16))
        else:
            part = _dot(a_ref[:, sl], w_ref[0, sl, :].astype(BF16))
        acc = part if acc is None else acc + part
    return acc


def _ar_kernel(*refs, transposed, relu2, residual):
    a_ref, w_ref, o_ref = refs[0], refs[1], refs[-1]
    acc = _cast_dot(a_ref, w_ref, transposed)
    if relu2:
        acc = jnp.maximum(acc, 0.0)
        acc = acc * acc
    if residual:
        acc = acc + refs[2][...]
    o_ref[...] = acc.astype(o_ref.dtype)


def _ar_matmul(a, w, layer, out_dtype, *, n_cols, col0=0, transposed=False, relu2=False, res=None, tm=2048,
               tn=256, name="ar_matmul"):
    m, k = a.shape
    if transposed:
        w_spec = pl.BlockSpec((pl.Element(1), pl.Element(tn), pl.Element(k)),
                              lambda i, j: (layer, pl.multiple_of(col0 + j * tn, 8), 0))
    else:
        w_spec = pl.BlockSpec((1, k, tn), lambda i, j: (layer, 0, col0 // tn + j))
    in_specs = [pl.BlockSpec((tm, k), lambda i, j: (i, 0)), w_spec]
    args = [a, w]
    if res is not None:
        in_specs.append(pl.BlockSpec((tm, tn), lambda i, j: (i, j)))
        args.append(res)
    return pl.pallas_call(
        functools.partial(_ar_kernel, transposed=transposed, relu2=relu2, residual=res is not None),
        grid=(m // tm, n_cols // tn),
        in_specs=in_specs,
        out_specs=pl.BlockSpec((tm, tn), lambda i, j: (i, j)),
        out_shape=jax.ShapeDtypeStruct((m, n_cols), out_dtype),
        compiler_params=_cparams("parallel", "arbitrary"),
        name=name,
    )(*args)


def _mm_res_kernel(a_ref, b_ref, r_ref, o_ref, acc_ref, *, nk):
    kk = pl.program_id(2)
    part = _dot(a_ref[...], b_ref[0])

    @pl.when(kk == 0)
    def _():
        acc_ref[...] = part

    @pl.when(kk > 0)
    def _():
        acc_ref[...] += part

    @pl.when(kk == nk - 1)
    def _():
        o_ref[...] = r_ref[...] + acc_ref[...]


def _matmul_residual(a, b, layer, res, *, tm=1024, tn=1024, tk=2048, name="matmul_res"):
    m, k = a.shape
    n = b.shape[2]
    nk = k // tk
    return pl.pallas_call(
        functools.partial(_mm_res_kernel, nk=nk),
        grid=(m // tm, n // tn, nk),
        in_specs=[pl.BlockSpec((tm, tk), lambda i, j, q: (i, q)),
                  pl.BlockSpec((1, tk, tn), lambda i, j, q: (layer, q, j)),
                  pl.BlockSpec((tm, tn), lambda i, j, q: (i, j))],
        out_specs=pl.BlockSpec((tm, tn), lambda i, j, q: (i, j)),
        out_shape=jax.ShapeDtypeStruct((m, n), F32),
        scratch_shapes=[pltpu.VMEM((tm, tn), F32)],
        compiler_params=_cparams("parallel", "arbitrary", "arbitrary"),
        name=name,
    )(a, b, res)


def _merge_kernel(oa_ref, ob_ref, oc_ref, wa_ref, wb_ref, wc_ref, ga_ref, gb_ref, gc_ref, o_ref):
    acc = _sigmoid(ga_ref[...]) * _cast_dot(oa_ref, wa_ref, False)
    acc += _sigmoid(gb_ref[...]) * _cast_dot(ob_ref, wb_ref, False)
    acc += _sigmoid(gc_ref[...]) * _cast_dot(oc_ref, wc_ref, False)
    o_ref[...] = acc.astype(o_ref.dtype)


def _branch_merge(o_a, o_b, o_c, w_branch, layer, zb, *, tm=1024, tn=256):
    m, kb = o_a.shape
    o_spec = pl.BlockSpec((tm, kb), lambda i, j: (i, 0))

    def w_spec(r):
        return pl.BlockSpec((1, kb, tn), lambda i, j: (layer, r, j))

    def g_spec(off):
        return pl.BlockSpec((tm, tn), lambda i, j: (i, off // tn + j))

    return pl.pallas_call(
        _merge_kernel,
        grid=(m // tm, D_MODEL // tn),
        in_specs=[o_spec, o_spec, o_spec, w_spec(0), w_spec(1), w_spec(2),
                  g_spec(ZB_MA), g_spec(ZB_MB), g_spec(ZB_MC)],
        out_specs=pl.BlockSpec((tm, tn), lambda i, j: (i, j)),
        out_shape=jax.ShapeDtypeStruct((m, D_MODEL), BF16),
        compiler_params=_cparams("parallel", "arbitrary"),
        name="branch_merge",
    )(o_a, o_b, o_c, w_branch, w_branch, w_branch, zb, zb, zb)


def _hgrn_chunk(q_in, f_in, v, g, lb, nw, st, tri):
    c = HG_CHUNK
    qc = _silu(q_in) * (HG_DIM ** -0.5)
    f = lb + (1.0 - lb) * _sigmoid(f_in)
    kk = 1.0 - f
    b = _dot(tri, jnp.log(f), precision=HIGHEST)
    o = _dot_nt((qc * jnp.exp(b)).astype(BF16), st.astype(BF16))
    b_last = b[c - 1:c, :]
    kdec = kk * jnp.exp(b_last - b)
    v16 = v.astype(BF16)
    st_new = st * jnp.exp(b_last) + _dot_tn(v16, kdec.astype(BF16))
    row = lax.broadcasted_iota(jnp.int32, (HG_SUB, HG_DIM), 0)
    col = lax.broadcasted_iota(jnp.int32, (HG_SUB, c), 1)
    parts = []
    for a in range(c // HG_SUB):
        lo = a * HG_SUB
        ba = b[lo:lo + HG_SUB, :]
        qa = qc[lo:lo + HG_SUB, :]
        acc = jnp.zeros((HG_SUB, HG_DIM), F32)
        if a > 0:
            bref = b[lo:lo + 1, :]
            qn = qa * jnp.exp(ba - bref)
            kn = kk * jnp.exp(jnp.minimum(bref - b, 0.0))
            att = _dot_nt(qn.astype(BF16), kn.astype(BF16))
            acc = _dot(jnp.where(col < lo, att, 0.0).astype(BF16), v16)
        for j in range(HG_SUB):
            jj = lo + j
            d = jnp.where(row >= j, ba - b[jj:jj + 1, :], NEG)
            w = jnp.sum(qa * kk[jj:jj + 1, :] * jnp.exp(d), axis=-1, keepdims=True)
            acc = acc + w * v[jj:jj + 1, :]
        parts.append(acc)
    o = o + jnp.concatenate(parts, axis=0)
    o = o * lax.rsqrt(jnp.mean(o * o, axis=-1, keepdims=True) + NORM_EPS) * nw * _silu(g)
    return o, st_new


def _hgrn_kernel(tab_ref, nw_ref, q_ref, f_ref, v_ref, g_ref, o_ref, st_ref, *, layer, rows):
    @pl.when(pl.program_id(2) == 0)
    def _():
        st_ref[...] = jnp.zeros_like(st_ref)

    tab = tab_ref[...]
    e = jnp.exp(tab - jnp.max(tab, axis=0, keepdims=True))
    p = e / jnp.sum(e, axis=0, keepdims=True)
    lb = jnp.sum(p[:layer + 1, :], axis=0, keepdims=True) - p[0:1, :]
    nw = nw_ref[...]
    r_i = lax.broadcasted_iota(jnp.int32, (HG_CHUNK, HG_CHUNK), 0)
    c_i = lax.broadcasted_iota(jnp.int32, (HG_CHUNK, HG_CHUNK), 1)
    tri = (r_i >= c_i).astype(F32)
    sts = [st_ref[hh] for hh in range(HG_HPB)]
    for ch in range(rows // HG_CHUNK):
        sl = slice(ch * HG_CHUNK, (ch + 1) * HG_CHUNK)
        outs = []
        for hh in range(HG_HPB):
            hs = slice(hh * HG_DIM, (hh + 1) * HG_DIM)
            o, sts[hh] = _hgrn_chunk(q_ref[sl, hs], f_ref[sl, hs], v_ref[sl, hs], g_ref[sl, hs], lb[:, hs], nw,
                                     sts[hh], tri)
            outs.append(o)
        o_ref[sl, :] = jnp.concatenate(outs, axis=1).astype(o_ref.dtype)
    for hh in range(HG_HPB):
        st_ref[hh] = sts[hh]


def _hgrn2(za, lb_table, norm_w, layer, batch, seq, rows=256):
    nrow = seq // rows
    width = HG_HPB * HG_DIM

    def col(off):
        return pl.BlockSpec((rows, width), lambda b, h, c: (b * nrow + c, off // width + h))

    return pl.pallas_call(
        functools.partial(_hgrn_kernel, layer=layer, rows=rows),
        grid=(batch, HG_HEADS // HG_HPB, nrow),
        in_specs=[pl.BlockSpec((DEPTH, width), lambda b, h, c: (0, h)),
                  pl.BlockSpec((1, HG_DIM), lambda b, h, c: (0, 0)),
                  col(OFF_AQ), col(OFF_AF), col(OFF_AI), col(OFF_AG)],
        out_specs=pl.BlockSpec((rows, width), lambda b, h, c: (b * nrow + c, h)),
        out_shape=jax.ShapeDtypeStruct((batch * seq, HG_WIDTH), BF16),
        scratch_shapes=[pltpu.VMEM((HG_HPB, HG_DIM, HG_DIM), F32)],
        compiler_params=_cparams("parallel", "parallel", "arbitrary"),
        name="hgrn2",
    )(lb_table, norm_w.reshape(1, HG_DIM), za, za, za, za)


def _ret_kernel(lg_ref, q_ref, k_ref, v_ref, g_ref, o_ref, s_ref, *, rows):
    @pl.when(pl.program_id(2) == 0)
    def _():
        s_ref[...] = jnp.zeros_like(s_ref)

    c = RET_CHUNK
    lg_v = lg_ref[0]
    lg_k = lg_v[:, :c]
    pos_r = lax.broadcasted_iota(jnp.int32, (c, RET_V), 0).astype(F32)
    query_decay = jnp.exp(lg_v * (pos_r + 1.0))
    rel = (lax.broadcasted_iota(jnp.int32, (c, c), 0) - lax.broadcasted_iota(jnp.int32, (c, c), 1)).astype(F32)
    intra_decay = jnp.where(rel >= 0, jnp.exp(lg_k * jnp.maximum(rel, 0.0)), 0.0)
    key_decay = jnp.exp(lg_k * (c - 1.0 - pos_r[:, :RET_QK]))
    chunk_decay = jnp.exp(lg_v * float(c))
    s = s_ref[...]
    for ch in range(rows // c):
        sl = slice(ch * c, (ch + 1) * c)
        q = q_ref[sl, :].astype(BF16)
        k = k_ref[sl, :] * (RET_QK ** -0.5)
        v = v_ref[sl, :].astype(BF16)
        inter = _dot(q, s.astype(BF16)) * query_decay
        scores = _dot_nt(q, k.astype(BF16)) * intra_decay
        o = inter + _dot(scores.astype(BF16), v)
        s = chunk_decay * s + _dot_tn((k * key_decay).astype(BF16), v)
        mu = jnp.mean(o, axis=-1, keepdims=True)
        oc = o - mu
        o = oc * lax.rsqrt(jnp.mean(oc * oc, axis=-1, keepdims=True) + NORM_EPS)
        o_ref[sl, :] = (o * _silu(g_ref[sl, :])).astype(o_ref.dtype)
    s_ref[...] = s


def _retention(zb, batch, seq, rows=256):
    nrow = seq // rows
    log_gamma = jnp.log(1.0 - jnp.exp2(-5.0 - jnp.arange(RET_HEADS, dtype=F32)))
    lg = jnp.broadcast_to(log_gamma[:, None, None], (RET_HEADS, 1, RET_V))
    return pl.pallas_call(
        functools.partial(_ret_kernel, rows=rows),
        grid=(batch, RET_HEADS, nrow),
        in_specs=[pl.BlockSpec((1, 1, RET_V), lambda b, h, c: (h, 0, 0)),
                  pl.BlockSpec((rows, RET_QK), lambda b, h, c: (b * nrow + c, ZB_CQ // RET_QK + h)),
                  pl.BlockSpec((rows, RET_QK), lambda b, h, c: (b * nrow + c, ZB_CK // RET_QK + h)),
                  pl.BlockSpec((rows, RET_V), lambda b, h, c: (b * nrow + c, ZB_CV // RET_V + h)),
                  pl.BlockSpec((rows, RET_V), lambda b, h, c: (b * nrow + c, ZB_CG // RET_V + h))],
        out_specs=pl.BlockSpec((rows, RET_V), lambda b, h, c: (b * nrow + c, h)),
        out_shape=jax.ShapeDtypeStruct((batch * seq, RET_V_WIDTH), BF16),
        scratch_shapes=[pltpu.VMEM((RET_QK, RET_V), F32)],
        compiler_params=_cparams("parallel", "parallel", "arbitrary"),
        name="retention",
    )(lg, zb, zb, zb, zb)


def _compress_kernel(y_ref, pe_ref, w1_ref, w2_ref, o_ref):
    half = CMP_STRIDE * NSA_DIM
    y = y_ref[0, 0]
    pe = pe_ref[0]
    top = _dot((y + pe[:, :half]).astype(BF16), w1_ref[0, :half, :])
    bot = _dot((y + pe[:, half:]).astype(BF16), w1_ref[0, half:, :])
    nc = y.shape[0]
    h = top + pltpu.roll(bot, nc - 1, 0)
    o_ref[0, 0] = _dot(_silu(h).astype(BF16), w2_ref[0])


def _compress(y, pe, w1, w2):
    _, bg, nc, half = y.shape
    return pl.pallas_call(
        _compress_kernel,
        grid=(2, bg),
        in_specs=[pl.BlockSpec((1, 1, nc, half), lambda s, i: (s, i, 0, 0)),
                  pl.BlockSpec((1, 1, 2 * half), lambda s, i: (s, 0, 0)),
                  pl.BlockSpec((1, 2 * half, CMP_HIDDEN), lambda s, i: (s, 0, 0)),
                  pl.BlockSpec((1, CMP_HIDDEN, NSA_DIM), lambda s, i: (s, 0, 0))],
        out_specs=pl.BlockSpec((1, 1, nc, NSA_DIM), lambda s, i: (s, i, 0, 0)),
        out_shape=jax.ShapeDtypeStruct((2, bg, nc, NSA_DIM), F32),
        compiler_params=_cparams("parallel", "parallel"),
        name="nsa_compress",
    )(y, pe, w1, w2)


def _slope_table():
    slopes = np.exp2(-8.0 * np.arange(1, NSA_HEADS + 1, dtype=np.float64) / NSA_HEADS).astype(np.float32)
    tab = np.zeros((NSA_GROUPS, 8, 512), np.float32)
    tab[:, :NSA_REP, :] = slopes.reshape(NSA_GROUPS, NSA_REP)[:, :, None]
    return jnp.asarray(tab)


def _cmp_kernel(slope_ref, wmap_ref, q_ref, kc_ref, vc_ref, gate_ref, o_ref, sel_ref, used_ref, score_ref,
                *, tq, nc, nslc):
    t0 = pl.program_id(2) * tq
    kc = kc_ref[0, 0]
    vc = vc_ref[0, 0].astype(BF16)
    gates = _sigmoid(gate_ref[0, 0])
    t_i = t0 + lax.broadcasted_iota(jnp.int32, (tq, nc), 0)
    n_i = lax.broadcasted_iota(jnp.int32, (tq, nc), 1)
    dist = t_i - (n_i * CMP_STRIDE + CMP_BLOCK - 1)
    valid = (dist >= 0) & (n_i < nc - 1)
    distf = dist.astype(F32)
    imp = jnp.zeros((tq, nc), F32)
    for r in range(NSA_REP):
        q = q_ref[:, r * NSA_DIM:(r + 1) * NSA_DIM]
        slope = slope_ref[0, r:r + 1, :nc]
        s = _dot_nt(q, kc, precision=HIGHEST) * (NSA_DIM ** -0.5) - slope * distf
        s = jnp.where(valid, s, NEG)
        m = jnp.max(s, axis=-1, keepdims=True)
        e = jnp.where(valid, jnp.exp(s - m), 0.0)
        den = jnp.sum(e, axis=-1, keepdims=True)
        p = e / jnp.where(den > 0, den, 1.0)
        o_ref[:, r * NSA_DIM:(r + 1) * NSA_DIM] = gates[:, 3 * r:3 * r + 1] * _dot(p.astype(BF16), vc)
        imp = imp + p
    imp_t = _dot_nt(wmap_ref[...], imp, precision=HIGHEST)
    blk = lax.broadcasted_iota(jnp.int32, (nslc, tq), 0)
    cur = jnp.right_shift(t0 + lax.broadcasted_iota(jnp.int32, (nslc, tq), 1), SLC_SHIFT)
    forced = (blk == 0) | (blk == cur) | (blk == cur - 1)
    score = jnp.where(blk > cur, -jnp.inf, jnp.where(forced, jnp.inf, imp_t))
    score_ref[...] = score
    sub = 8
    groups = [score_ref[g0:g0 + sub, :] for g0 in range(0, nslc, sub)]
    ranks = [jnp.zeros((sub, tq), F32) for _ in groups]
    row8 = lax.broadcasted_iota(jnp.int32, (sub, tq), 0)
    for s_i in range(nslc):
        other = jnp.broadcast_to(score_ref[s_i:s_i + 1, :], (sub, tq))
        for gi, sc in enumerate(groups):
            g0 = gi * sub
            if g0 > s_i:
                beats = other >= sc
            elif g0 + sub - 1 <= s_i:
                beats = other > sc
            else:
                beats = (other > sc) | ((other == sc) & (row8 > s_i - g0))
            ranks[gi] = ranks[gi] + jnp.where(beats, 1.0, 0.0)
    rank = jnp.concatenate(ranks, axis=0)
    sel = ((rank < min(SLC_TOPK, nslc)) & (blk <= cur)).astype(F32)
    sel_ref[0, 0] = sel
    for a in range(tq // ATT_T):
        used_ref[0, 0, 0, :, a:a + 1] = jnp.max(sel[:, a * ATT_T:(a + 1) * ATT_T], axis=1, keepdims=True)


def _cmp_attention(za, kvc, gates, batch, seq, tq=256):
    nc = seq // CMP_STRIDE
    nslc = seq // SLC_BLOCK
    nq = seq // tq
    c_start = np.arange(nc) * CMP_STRIDE
    s_start = np.arange(nslc) * SLC_BLOCK
    overlap = np.clip(np.minimum(c_start[:, None] + CMP_BLOCK, s_start[None, :] + SLC_BLOCK)
                      - np.maximum(c_start[:, None], s_start[None, :]), 0, None)
    wmap_t = (overlap.astype(np.float32) / CMP_STRIDE).T.copy()
    wmap_t[:, nc - 1] = 0.0
    sub = tq // ATT_T
    return pl.pallas_call(
        functools.partial(_cmp_kernel, tq=tq, nc=nc, nslc=nslc),
        grid=(batch, NSA_GROUPS, nq),
        in_specs=[pl.BlockSpec((1, 8, 512), lambda b, g, i: (g, 0, 0)),
                  pl.BlockSpec((nslc, nc), lambda b, g, i: (0, 0)),
                  pl.BlockSpec((tq, NSA_GW), lambda b, g, i: (b * nq + i, OFF_BQ // NSA_GW + g)),
                  pl.BlockSpec((1, 1, nc, NSA_DIM), lambda b, g, i: (0, b * NSA_GROUPS + g, 0, 0)),
                  pl.BlockSpec((1, 1, nc, NSA_DIM), lambda b, g, i: (1, b * NSA_GROUPS + g, 0, 0)),
                  pl.BlockSpec((1, 1, tq, 3 * NSA_REP), lambda b, g, i: (b, g, i, 0))],
        out_specs=[pl.BlockSpec((tq, NSA_GW), lambda b, g, i: (b * nq + i, g)),
                   pl.BlockSpec((1, 1, nslc, tq), lambda b, g, i: (b, g, 0, i)),
                   pl.BlockSpec((1, 1, 1, nslc, sub), lambda b, g, i: (b, g, i, 0, 0))],
        out_shape=[jax.ShapeDtypeStruct((batch * seq, NSA_WIDTH), F32),
                   jax.ShapeDtypeStruct((batch, NSA_GROUPS, nslc, seq), F32),
                   jax.ShapeDtypeStruct((batch, NSA_GROUPS, nq, nslc, sub), F32)],
        scratch_shapes=[pltpu.VMEM((nslc, tq), F32)],
        compiler_params=_cparams("parallel", "parallel", "arbitrary"),
        name="nsa_cmp_select",
    )(_slope_table(), jnp.asarray(wmap_t), za, kvc, kvc, gates)


ATT_SUPER = 4
ALIBI_FEATS = 6


def _alibi_query_features():
    slopes = np.exp2(-8.0 * np.arange(1, NSA_HEADS + 1, dtype=np.float64) / NSA_HEADS).astype(np.float32)

    def top_bits(x):
        return (x.view(np.uint32) & np.uint32(0xFFFF0000)).view(np.float32)

    s1 = top_bits(slopes)
    r1 = slopes - s1
    s2 = top_bits(r1)
    s3 = r1 - s2
    rows = np.stack([-64.0 * s1, -64.0 * s2, -64.0 * s3, -s1, -s2, -s3]).astype(np.float32)
    feat = np.zeros((NSA_GROUPS, NSA_DIM, NSA_REP, ATT_T), np.float32)
    feat[:, :ALIBI_FEATS] = rows.reshape(ALIBI_FEATS, NSA_GROUPS, NSA_REP).transpose(1, 0, 2)[:, :, :, None]
    return jnp.asarray(feat.reshape(NSA_GROUPS, NSA_DIM, NSA_GW), dtype=BF16)


def _attend(k_ref, v_ref, chunks, valids, qi, qt_aug, kf_static, kf_ind, m_ref, l_ref, acc_ref):
    t = ATT_T
    ks, vs, mbs = [], [], []
    for c, valid in zip(chunks, valids):
        k0 = pl.multiple_of(c * t, t)
        kfeat = (kf_static + (2 * (qi - c)).astype(F32) * kf_ind).astype(BF16)
        ks.append(jnp.concatenate([k_ref[pl.ds(k0, t), :].astype(BF16), kfeat], axis=1))
        vs.append(v_ref[pl.ds(k0, t), :].astype(BF16))
        mbs.append(jnp.where(valid, 0.0, NEG))
    st = _dot(jnp.concatenate(ks, axis=0), qt_aug)
    mbias = jnp.concatenate(mbs, axis=0)
    ps, alphas = [], []
    for r in range(NSA_REP):
        cs = slice(r * t, (r + 1) * t)
        s = st[:, cs] + mbias
        m_old = m_ref[:, cs]
        m_new = jnp.maximum(m_old, jnp.max(s, axis=0, keepdims=True))
        alphas.append(jnp.exp(m_old - m_new))
        p = jnp.exp(s - m_new)
        l_ref[:, cs] = alphas[-1] * l_ref[:, cs] + jnp.sum(p, axis=0, keepdims=True)
        m_ref[:, cs] = m_new
        ps.append(p.astype(BF16))
    acc_ref[...] = (jnp.concatenate(alphas, axis=1) * acc_ref[...]
                    + _dot_tn(jnp.concatenate(vs, axis=0), jnp.concatenate(ps, axis=1)))


def _slc_win_kernel(ids_ref, cnt_ref, qfeat_ref, q_ref, ks_ref, vs_ref, kw_ref, vw_ref, sel_ref, gate_ref,
                    ocmp_ref, o_ref, m_ref, l_ref, acc_ref, *, nq):
    t = ATT_T
    b, g, qi = pl.program_id(0), pl.program_id(1), pl.program_id(2)
    t0 = qi * t
    qt = jnp.concatenate([q_ref[:, r * NSA_DIM:(r + 1) * NSA_DIM].T for r in range(NSA_REP)], axis=1)
    qt_aug = jnp.concatenate([(qt * (NSA_DIM ** -0.5)).astype(BF16), qfeat_ref[0]], axis=0)
    key_i = lax.broadcasted_iota(jnp.int32, (t, t), 0)
    qry_i = lax.broadcasted_iota(jnp.int32, (t, t), 1)
    back = (t - 1) - key_i
    kf_ind = (qry_i < 3).astype(F32)
    kf_static = jnp.where(qry_i < 3, jnp.right_shift(back, SLC_SHIFT),
                          jnp.where(qry_i < ALIBI_FEATS, jnp.bitwise_and(back, SLC_BLOCK - 1), 0)).astype(F32)

    def reset():
        m_ref[...] = jnp.full_like(m_ref, NEG)
        l_ref[...] = jnp.zeros_like(l_ref)
        acc_ref[...] = jnp.zeros_like(acc_ref)

    reset()
    tile = (b * NSA_GROUPS + g) * nq + qi
    cnt = cnt_ref[tile]
    half = SLC_BLOCK
    never = jnp.int32(2 ** 30)

    def slc_body(s, carry):
        chunks, valids = [], []
        for u in range(ATT_SUPER):
            e = s * ATT_SUPER + u
            c = ids_ref[tile * nq + jnp.minimum(e, cnt - 1)]
            lo = jnp.broadcast_to(sel_ref[0, 0, pl.ds(2 * c, 1), :], (half, t))
            hi = jnp.broadcast_to(sel_ref[0, 0, pl.ds(2 * c + 1, 1), :], (half, t))
            picked = jnp.concatenate([lo, hi], axis=0) > 0.5
            dist = (t0 + qry_i) - (c * t + key_i)
            chunks.append(c)
            valids.append(picked & (dist >= jnp.where(e < cnt, 0, never)))
        _attend(ks_ref, vs_ref, chunks, valids, qi, qt_aug, kf_static, kf_ind, m_ref, l_ref, acc_ref)
        return carry

    lax.fori_loop(0, jnp.right_shift(cnt + (ATT_SUPER - 1), 2), slc_body, 0)
    o_slc = acc_ref[...] * (1.0 / l_ref[...])

    reset()
    chunks, valids = [], []
    for i in range(WIN_SIZE // t + 1):
        c = jnp.maximum(qi - i, 0)
        dist = (t0 + qry_i) - (c * t + key_i)
        chunks.append(c)
        valids.append((dist >= jnp.where(qi - i >= 0, 0, never)) & (dist < WIN_SIZE))
    _attend(kw_ref, vw_ref, chunks, valids, qi, qt_aug, kf_static, kf_ind, m_ref, l_ref, acc_ref)
    o_win = acc_ref[...] * (1.0 / l_ref[...])

    gates = _sigmoid(gate_ref[0, 0])
    for r in range(NSA_REP):
        cs = slice(r * t, (r + 1) * t)
        o_t = gates[3 * r + 1:3 * r + 2, :] * o_slc[:, cs] + gates[3 * r + 2:3 * r + 3, :] * o_win[:, cs]
        o_ref[:, cs] = (o_t.T + ocmp_ref[:, cs]).astype(o_ref.dtype)


def _slc_win_attention(ids, cnt, za, sel_t, gates_t, o_cmp, batch, seq):
    t = ATT_T
    nslc = seq // SLC_BLOCK
    nq = seq // t

    def kv(off):
        return pl.BlockSpec((seq, NSA_DIM), lambda b, g, i, *_: (b, off // NSA_DIM + g))

    grid_spec = pltpu.PrefetchScalarGridSpec(
        num_scalar_prefetch=2,
        grid=(batch, NSA_GROUPS, nq),
        in_specs=[pl.BlockSpec((1, NSA_DIM, NSA_GW), lambda b, g, i, *_: (g, 0, 0)),
                  pl.BlockSpec((t, NSA_GW), lambda b, g, i, *_: (b * nq + i, OFF_BQ // NSA_GW + g)),
                  kv(OFF_BKS), kv(OFF_BVS), kv(OFF_BKW), kv(OFF_BVW),
                  pl.BlockSpec((1, 1, nslc, t), lambda b, g, i, *_: (b, g, 0, i)),
                  pl.BlockSpec((1, 1, 3 * NSA_REP, t), lambda b, g, i, *_: (b, g, 0, i)),
                  pl.BlockSpec((t, NSA_GW), lambda b, g, i, *_: (b * nq + i, g))],
        out_specs=pl.BlockSpec((t, NSA_GW), lambda b, g, i, *_: (b * nq + i, g)),
        scratch_shapes=[pltpu.VMEM((1, NSA_GW), F32), pltpu.VMEM((1, NSA_GW), F32),
                        pltpu.VMEM((NSA_DIM, NSA_GW), F32)])
    return pl.pallas_call(
        functools.partial(_slc_win_kernel, nq=nq),
        grid_spec=grid_spec,
        out_shape=jax.ShapeDtypeStruct((batch * seq, NSA_WIDTH), BF16),
        compiler_params=_cparams("parallel", "parallel", "arbitrary"),
        name="nsa_slc_win",
    )(ids, cnt, _alibi_query_features(), za, za, za, za, za, sel_t, gates_t, o_cmp)


def _nsa(za, pe, w1, w2, batch, seq):
    nc = seq // CMP_STRIDE
    t = ATT_T
    nq = seq // t

    def grouped(off):
        return za[:, off:off + NSA_KV_WIDTH].reshape(batch, seq, NSA_GROUPS, NSA_DIM).transpose(0, 2, 1, 3)

    def blocks(off):
        return grouped(off).reshape(batch * NSA_GROUPS, nc, CMP_STRIDE * NSA_DIM)

    kvc = _compress(jnp.stack([blocks(OFF_BKC), blocks(OFF_BVC)]), pe, w1, w2)
    gates = (za[:, OFF_BGATE:OFF_BGATE + GATE_COLS].reshape(batch, seq, NSA_GROUPS, 3 * NSA_REP)
             .transpose(0, 2, 1, 3))
    o_cmp, sel_t, used = _cmp_attention(za, kvc, gates, batch, seq)
    nslc = seq // SLC_BLOCK
    used = used.transpose(0, 1, 2, 4, 3).reshape(batch * NSA_GROUPS * nq, nslc // 2, 2)
    unused = (jnp.max(used, axis=-1) < 0.5).astype(jnp.int32)
    ids = jnp.argsort(unused, axis=-1, stable=True).astype(jnp.int32).reshape(-1)
    cnt = (nslc // 2 - jnp.sum(unused, axis=-1)).astype(jnp.int32)
    return _slc_win_attention(ids, cnt, za, sel_t, gates.transpose(0, 1, 3, 2), o_cmp, batch, seq)


def _layer(x, layer, norm1_w, w_in_t, lb_table, hgrn_norm_w, pe, w1, w2, w_branch, w_out, norm2_w, w_ff1, w_ff2_bf,
           batch, seq):
    h = _rmsnorm(x, norm1_w, BF16)
    za = _ar_matmul(h, w_in_t, layer, F32, n_cols=ZA_WIDTH, transposed=True, name="proj_in_a")
    zb = _ar_matmul(h, w_in_t, layer, F32, n_cols=ZB_WIDTH, col0=OFF_BGATE + GATE_COLS, transposed=True,
                    name="proj_in_b")
    o_a = _hgrn2(za, lb_table, hgrn_norm_w, layer, batch, seq)
    o_b = _nsa(za, pe, w1, w2, batch, seq)
    o_c = _retention(zb, batch, seq)
    merged = _branch_merge(o_a, o_b, o_c, w_branch, layer, zb)
    x = _ar_matmul(merged, w_out, layer, F32, n_cols=D_MODEL, res=x, tm=1024, name="proj_out")
    h = _rmsnorm(x, norm2_w, BF16)
    u = _ar_matmul(h, w_ff1, layer, BF16, n_cols=D_FF, relu2=True, name="ffn_up")
    return _ar_matmul(u, w_ff2_bf, layer, F32, n_cols=D_MODEL, res=x, tm=512, name="ffn_down")


def kernel(x, norm1_w, w_in, hgrn_lb_table, hgrn_norm_w, cmp_pe_k, cmp_pe_v, cmp_w1_k, cmp_w1_v, cmp_w2_k,
           cmp_w2_v, w_branch, w_out, norm2_w, w_ff1, w_ff2, final_norm_w):
    batch, seq, d = x.shape
    xf = x.reshape(batch * seq, d)
    w_ff2_bf = w_ff2.astype(BF16)
    w_in_t = jnp.swapaxes(w_in, 1, 2)
    for l in range(DEPTH):
        pe = jnp.stack([cmp_pe_k[l].reshape(1, -1), cmp_pe_v[l].reshape(1, -1)])
        w1 = jnp.stack([cmp_w1_k[l], cmp_w1_v[l]]).astype(BF16)
        w2 = jnp.stack([cmp_w2_k[l], cmp_w2_v[l]]).astype(BF16)
        xf = _layer(xf, l, norm1_w[l], w_in_t, hgrn_lb_table, hgrn_norm_w[l], pe, w1, w2, w_branch, w_out,
                    norm2_w[l], w_ff1, w_ff2_bf, batch, seq)
    return _rmsnorm(xf, final_norm_w, F32).reshape(batch, seq, d)
```

```python
import functools

import numpy as np
import jax
import jax.numpy as jnp
from jax import lax
from jax.experimental import pallas as pl
from jax.experimental.pallas import tpu as pltpu

F32 = jnp.float32
BF16 = jnp.bfloat16
HIGHEST = lax.Precision.HIGHEST

D_MODEL = 4096
DEPTH = 2
NORM_EPS = 1e-6
LANE = 128

HG_HEADS = 16
HG_DIM = 128
HG_WIDTH = HG_HEADS * HG_DIM
HG_CHUNK = 64
HG_SUB = 16
HG_HPB = 2
NSA_HEADS = 16
NSA_GROUPS = 4
NSA_REP = NSA_HEADS // NSA_GROUPS
NSA_DIM = 128
NSA_WIDTH = NSA_HEADS * NSA_DIM
NSA_KV_WIDTH = NSA_GROUPS * NSA_DIM
NSA_GW = NSA_REP * NSA_DIM
CMP_BLOCK = 32
CMP_STRIDE = 16
CMP_HIDDEN = 256
SLC_BLOCK = 64
SLC_SHIFT = 6
SLC_TOPK = 16
WIN_SIZE = 512
ATT_T = 128
RET_HEADS = 8
RET_QK = 128
RET_V = 256
RET_QK_WIDTH = RET_HEADS * RET_QK
RET_V_WIDTH = RET_HEADS * RET_V
RET_CHUNK = 128
MIX_WIDTH = HG_WIDTH + NSA_WIDTH + RET_V_WIDTH
D_FF = 4 * D_MODEL

PROJ_TN = 512
GATE_COLS = NSA_HEADS * 3
OFF_AQ = 0
OFF_AF = OFF_AQ + HG_WIDTH
OFF_AI = OFF_AF + HG_WIDTH
OFF_AG = OFF_AI + HG_WIDTH
OFF_BQ = OFF_AG + HG_WIDTH
OFF_BKC = OFF_BQ + NSA_WIDTH
OFF_BVC = OFF_BKC + NSA_KV_WIDTH
OFF_BKS = OFF_BVC + NSA_KV_WIDTH
OFF_BVS = OFF_BKS + NSA_KV_WIDTH
OFF_BKW = OFF_BVS + NSA_KV_WIDTH
OFF_BVW = OFF_BKW + NSA_KV_WIDTH
OFF_BGATE = OFF_BVW + NSA_KV_WIDTH
ZA_WIDTH = OFF_BGATE + PROJ_TN
ZB_CQ = 0
ZB_CK = ZB_CQ + RET_QK_WIDTH
ZB_CV = ZB_CK + RET_QK_WIDTH
ZB_CG = ZB_CV + RET_V_WIDTH
ZB_MA = ZB_CG + RET_V_WIDTH
ZB_MB = ZB_MA + D_MODEL
ZB_MC = ZB_MB + D_MODEL
ZB_WIDTH = ZB_MC + D_MODEL

NEG = -1e30
VMEM_LIMIT = 56 * 1024 * 1024


def _cparams(*sem):
    return pltpu.CompilerParams(dimension_semantics=sem, vmem_limit_bytes=VMEM_LIMIT)


def _sigmoid(x):
    return 1.0 / (1.0 + jnp.exp(-x))


def _silu(x):
    return x * _sigmoid(x)


def _dot(a, b, precision=None):
    return jnp.dot(a, b, preferred_element_type=F32, precision=precision)


def _dot_nt(a, b, precision=None):
    return lax.dot_general(a, b, (((1,), (1,)), ((), ())), preferred_element_type=F32, precision=precision)


def _dot_tn(a, b):
    return lax.dot_general(a, b, (((0,), (0,)), ((), ())), preferred_element_type=F32)


def _rmsnorm_kernel(x_ref, w_ref, o_ref):
    x = x_ref[...]
    y = x * lax.rsqrt(jnp.mean(x * x, axis=-1, keepdims=True) + NORM_EPS)
    o_ref[...] = (y * w_ref[...]).astype(o_ref.dtype)


def _rmsnorm(x, w, out_dtype, tm=256):
    m, d = x.shape
    return pl.pallas_call(
        _rmsnorm_kernel,
        grid=(m // tm,),
        in_specs=[pl.BlockSpec((tm, d), lambda i: (i, 0)), pl.BlockSpec((1, d), lambda i: (0, 0))],
        out_specs=pl.BlockSpec((tm, d), lambda i: (i, 0)),
        out_shape=jax.ShapeDtypeStruct((m, d), out_dtype),
        compiler_params=_cparams("parallel"),
        name="rmsnorm",
    )(x, w.reshape(1, d))


CAST_K = 512


def _cast_dot(a_ref, w_ref, transposed):
    if w_ref.dtype == BF16:
        return _dot_nt(a_ref[...], w_ref[0]) if transposed else _dot(a_ref[...], w_ref[0])
    acc = None
    for k0 in range(0, a_ref.shape[1], CAST_K):
        sl = slice(k0, k0 + CAST_K)
        if transposed:
            part = _dot_nt(a_ref[:, sl], w_ref[0, :, sl].astype(BF16))
        else:
            part = _dot(a_ref[:, sl], w_ref[0, sl, :].astype(BF16))
        acc = part if acc is None else acc + part
    return acc


def _ar_kernel(*refs, transposed, relu2, residual):
    a_ref, w_ref, o_ref = refs[0], refs[1], refs[-1]
    acc = _cast_dot(a_ref, w_ref, transposed)
    if relu2:
        acc = jnp.maximum(acc, 0.0)
        acc = acc * acc
    if residual:
        acc = acc + refs[2][...]
    o_ref[...] = acc.astype(o_ref.dtype)


def _ar_matmul(a, w, layer, out_dtype, *, n_cols, col0=0, transposed=False, relu2=False, res=None, tm=2048,
               tn=256, a_single_buffer=False, name="ar_matmul"):
    m, k = a.shape
    a_mode = dict(pipeline_mode=pl.Buffered(1)) if a_single_buffer else {}
    if transposed:
        w_spec = pl.BlockSpec((pl.Element(1), pl.Element(tn), pl.Element(k)),
                              lambda i, j: (layer, pl.multiple_of(col0 + j * tn, 8), 0))
    else:
        w_spec = pl.BlockSpec((1, k, tn), lambda i, j: (layer, 0, col0 // tn + j))
    in_specs = [pl.BlockSpec((tm, k), lambda i, j: (i, 0), **a_mode), w_spec]
    args = [a, w]
    if res is not None:
        in_specs.append(pl.BlockSpec((tm, tn), lambda i, j: (i, j)))
        args.append(res)
    return pl.pallas_call(
        functools.partial(_ar_kernel, transposed=transposed, relu2=relu2, residual=res is not None),
        grid=(m // tm, n_cols // tn),
        in_specs=in_specs,
        out_specs=pl.BlockSpec((tm, tn), lambda i, j: (i, j)),
        out_shape=jax.ShapeDtypeStruct((m, n_cols), out_dtype),
        compiler_params=_cparams("parallel", "arbitrary"),
        name=name,
    )(*args)


def _mm_res_kernel(a_ref, b_ref, r_ref, o_ref, acc_ref, *, nk):
    kk = pl.program_id(2)
    part = _dot(a_ref[...], b_ref[0])

    @pl.when(kk == 0)
    def _():
        acc_ref[...] = part

    @pl.when(kk > 0)
    def _():
        acc_ref[...] += part

    @pl.when(kk == nk - 1)
    def _():
        o_ref[...] = r_ref[...] + acc_ref[...]


def _matmul_residual(a, b, layer, res, *, tm=1024, tn=1024, tk=2048, name="matmul_res"):
    m, k = a.shape
    n = b.shape[2]
    nk = k // tk
    return pl.pallas_call(
        functools.partial(_mm_res_kernel, nk=nk),
        grid=(m // tm, n // tn, nk),
        in_specs=[pl.BlockSpec((tm, tk), lambda i, j, q: (i, q)),
                  pl.BlockSpec((1, tk, tn), lambda i, j, q: (layer, q, j)),
                  pl.BlockSpec((tm, tn), lambda i, j, q: (i, j))],
        out_specs=pl.BlockSpec((tm, tn), lambda i, j, q: (i, j)),
        out_shape=jax.ShapeDtypeStruct((m, n), F32),
        scratch_shapes=[pltpu.VMEM((tm, tn), F32)],
        compiler_params=_cparams("parallel", "arbitrary", "arbitrary"),
        name=name,
    )(a, b, res)


def _merge_kernel(oa_ref, ob_ref, oc_ref, wa_ref, wb_ref, wc_ref, ga_ref, gb_ref, gc_ref, o_ref):
    acc = _sigmoid(ga_ref[...]) * _cast_dot(oa_ref, wa_ref, False)
    acc += _sigmoid(gb_ref[...]) * _cast_dot(ob_ref, wb_ref, False)
    acc += _sigmoid(gc_ref[...]) * _cast_dot(oc_ref, wc_ref, False)
    o_ref[...] = acc.astype(o_ref.dtype)


def _branch_merge(o_a, o_b, o_c, w_branch, layer, zb, *, tm=1024, tn=256):
    m, kb = o_a.shape
    o_spec = pl.BlockSpec((tm, kb), lambda i, j: (i, 0))

    def w_spec(r):
        return pl.BlockSpec((1, kb, tn), lambda i, j: (layer, r, j))

    def g_spec(off):
        return pl.BlockSpec((tm, tn), lambda i, j: (i, off // tn + j))

    return pl.pallas_call(
        _merge_kernel,
        grid=(m // tm, D_MODEL // tn),
        in_specs=[o_spec, o_spec, o_spec, w_spec(0), w_spec(1), w_spec(2),
                  g_spec(ZB_MA), g_spec(ZB_MB), g_spec(ZB_MC)],
        out_specs=pl.BlockSpec((tm, tn), lambda i, j: (i, j)),
        out_shape=jax.ShapeDtypeStruct((m, D_MODEL), BF16),
        compiler_params=_cparams("parallel", "arbitrary"),
        name="branch_merge",
    )(o_a, o_b, o_c, w_branch, w_branch, w_branch, zb, zb, zb)


def _hgrn_chunk(q_in, f_in, v, g, lb, nw, st, tri):
    c = HG_CHUNK
    qc = _silu(q_in) * (HG_DIM ** -0.5)
    f = lb + (1.0 - lb) * _sigmoid(f_in)
    kk = 1.0 - f
    b = _dot(tri, jnp.log(f), precision=HIGHEST)
    o = _dot_nt((qc * jnp.exp(b)).astype(BF16), st.astype(BF16))
    b_last = b[c - 1:c, :]
    kdec = kk * jnp.exp(b_last - b)
    v16 = v.astype(BF16)
    st_new = st * jnp.exp(b_last) + _dot_tn(v16, kdec.astype(BF16))
    row = lax.broadcasted_iota(jnp.int32, (HG_SUB, HG_DIM), 0)
    col = lax.broadcasted_iota(jnp.int32, (HG_SUB, c), 1)
    parts = []
    for a in range(c // HG_SUB):
        lo = a * HG_SUB
        ba = b[lo:lo + HG_SUB, :]
        qa = qc[lo:lo + HG_SUB, :]
        acc = jnp.zeros((HG_SUB, HG_DIM), F32)
        if a > 0:
            bref = b[lo:lo + 1, :]
            qn = qa * jnp.exp(ba - bref)
            kn = kk * jnp.exp(jnp.minimum(bref - b, 0.0))
            att = _dot_nt(qn.astype(BF16), kn.astype(BF16))
            acc = _dot(jnp.where(col < lo, att, 0.0).astype(BF16), v16)
        for j in range(HG_SUB):
            jj = lo + j
            d = jnp.where(row >= j, ba - b[jj:jj + 1, :], NEG)
            w = jnp.sum(qa * kk[jj:jj + 1, :] * jnp.exp(d), axis=-1, keepdims=True)
            acc = acc + w * v[jj:jj + 1, :]
        parts.append(acc)
    o = o + jnp.concatenate(parts, axis=0)
    o = o * lax.rsqrt(jnp.mean(o * o, axis=-1, keepdims=True) + NORM_EPS) * nw * _silu(g)
    return o, st_new


def _hgrn_kernel(tab_ref, nw_ref, q_ref, f_ref, v_ref, g_ref, o_ref, st_ref, *, layer, rows):
    @pl.when(pl.program_id(2) == 0)
    def _():
        st_ref[...] = jnp.zeros_like(st_ref)

    tab = tab_ref[...]
    e = jnp.exp(tab - jnp.max(tab, axis=0, keepdims=True))
    p = e / jnp.sum(e, axis=0, keepdims=True)
    lb = jnp.sum(p[:layer + 1, :], axis=0, keepdims=True) - p[0:1, :]
    nw = nw_ref[...]
    r_i = lax.broadcasted_iota(jnp.int32, (HG_CHUNK, HG_CHUNK), 0)
    c_i = lax.broadcasted_iota(jnp.int32, (HG_CHUNK, HG_CHUNK), 1)
    tri = (r_i >= c_i).astype(F32)
    sts = [st_ref[hh] for hh in range(HG_HPB)]
    for ch in range(rows // HG_CHUNK):
        sl = slice(ch * HG_CHUNK, (ch + 1) * HG_CHUNK)
        outs = []
        for hh in range(HG_HPB):
            hs = slice(hh * HG_DIM, (hh + 1) * HG_DIM)
            o, sts[hh] = _hgrn_chunk(q_ref[sl, hs], f_ref[sl, hs], v_ref[sl, hs], g_ref[sl, hs], lb[:, hs], nw,
                                     sts[hh], tri)
            outs.append(o)
        o_ref[sl, :] = jnp.concatenate(outs, axis=1).astype(o_ref.dtype)
    for hh in range(HG_HPB):
        st_ref[hh] = sts[hh]


def _hgrn2(za, lb_table, norm_w, layer, batch, seq, rows=256):
    nrow = seq // rows
    width = HG_HPB * HG_DIM

    def col(off):
        return pl.BlockSpec((rows, width), lambda b, h, c: (b * nrow + c, off // width + h))

    return pl.pallas_call(
        functools.partial(_hgrn_kernel, layer=layer, rows=rows),
        grid=(batch, HG_HEADS // HG_HPB, nrow),
        in_specs=[pl.BlockSpec((DEPTH, width), lambda b, h, c: (0, h)),
                  pl.BlockSpec((1, HG_DIM), lambda b, h, c: (0, 0)),
                  col(OFF_AQ), col(OFF_AF), col(OFF_AI), col(OFF_AG)],
        out_specs=pl.BlockSpec((rows, width), lambda b, h, c: (b * nrow + c, h)),
        out_shape=jax.ShapeDtypeStruct((batch * seq, HG_WIDTH), BF16),
        scratch_shapes=[pltpu.VMEM((HG_HPB, HG_DIM, HG_DIM), F32)],
        compiler_params=_cparams("parallel", "parallel", "arbitrary"),
        name="hgrn2",
    )(lb_table, norm_w.reshape(1, HG_DIM), za, za, za, za)


def _ret_kernel(lg_ref, q_ref, k_ref, v_ref, g_ref, o_ref, s_ref, *, rows):
    @pl.when(pl.program_id(2) == 0)
    def _():
        s_ref[...] = jnp.zeros_like(s_ref)

    c = RET_CHUNK
    lg_v = lg_ref[0]
    lg_k = lg_v[:, :c]
    pos_r = lax.broadcasted_iota(jnp.int32, (c, RET_V), 0).astype(F32)
    query_decay = jnp.exp(lg_v * (pos_r + 1.0))
    rel = (lax.broadcasted_iota(jnp.int32, (c, c), 0) - lax.broadcasted_iota(jnp.int32, (c, c), 1)).astype(F32)
    intra_decay = jnp.where(rel >= 0, jnp.exp(lg_k * jnp.maximum(rel, 0.0)), 0.0)
    key_decay = jnp.exp(lg_k * (c - 1.0 - pos_r[:, :RET_QK]))
    chunk_decay = jnp.exp(lg_v * float(c))
    s = s_ref[...]
    for ch in range(rows // c):
        sl = slice(ch * c, (ch + 1) * c)
        q = q_ref[sl, :].astype(BF16)
        k = k_ref[sl, :] * (RET_QK ** -0.5)
        v = v_ref[sl, :].astype(BF16)
        inter = _dot(q, s.astype(BF16)) * query_decay
        scores = _dot_nt(q, k.astype(BF16)) * intra_decay
        o = inter + _dot(scores.astype(BF16), v)
        s = chunk_decay * s + _dot_tn((k * key_decay).astype(BF16), v)
        mu = jnp.mean(o, axis=-1, keepdims=True)
        oc = o - mu
        o = oc * lax.rsqrt(jnp.mean(oc * oc, axis=-1, keepdims=True) + NORM_EPS)
        o_ref[sl, :] = (o * _silu(g_ref[sl, :])).astype(o_ref.dtype)
    s_ref[...] = s


def _retention(zb, batch, seq, rows=1024):
    rows = min(rows, seq)
    nrow = seq // rows
    log_gamma = jnp.log(1.0 - jnp.exp2(-5.0 - jnp.arange(RET_HEADS, dtype=F32)))
    lg = jnp.broadcast_to(log_gamma[:, None, None], (RET_HEADS, 1, RET_V))
    return pl.pallas_call(
        functools.partial(_ret_kernel, rows=rows),
        grid=(batch, RET_HEADS, nrow),
        in_specs=[pl.BlockSpec((1, 1, RET_V), lambda b, h, c: (h, 0, 0)),
                  pl.BlockSpec((rows, RET_QK), lambda b, h, c: (b * nrow + c, ZB_CQ // RET_QK + h)),
                  pl.BlockSpec((rows, RET_QK), lambda b, h, c: (b * nrow + c, ZB_CK // RET_QK + h)),
                  pl.BlockSpec((rows, RET_V), lambda b, h, c: (b * nrow + c, ZB_CV // RET_V + h)),
                  pl.BlockSpec((rows, RET_V), lambda b, h, c: (b * nrow + c, ZB_CG // RET_V + h))],
        out_specs=pl.BlockSpec((rows, RET_V), lambda b, h, c: (b * nrow + c, h)),
        out_shape=jax.ShapeDtypeStruct((batch * seq, RET_V_WIDTH), BF16),
        scratch_shapes=[pltpu.VMEM((RET_QK, RET_V), F32)],
        compiler_params=_cparams("parallel", "parallel", "arbitrary"),
        name="retention",
    )(lg, zb, zb, zb, zb)


def _compress_kernel(y_ref, pe_ref, w1_ref, w2_ref, o_ref):
    half = CMP_STRIDE * NSA_DIM
    y = y_ref[0, 0]
    pe = pe_ref[0]
    top = _dot((y + pe[:, :half]).astype(BF16), w1_ref[0, :half, :])
    bot = _dot((y + pe[:, half:]).astype(BF16), w1_ref[0, half:, :])
    nc = y.shape[0]
    h = top + pltpu.roll(bot, nc - 1, 0)
    o_ref[0, 0] = _dot(_silu(h).astype(BF16), w2_ref[0])


def _compress(y, pe, w1, w2):
    _, bg, nc, half = y.shape
    return pl.pallas_call(
        _compress_kernel,
        grid=(2, bg),
        in_specs=[pl.BlockSpec((1, 1, nc, half), lambda s, i: (s, i, 0, 0)),
                  pl.BlockSpec((1, 1, 2 * half), lambda s, i: (s, 0, 0)),
                  pl.BlockSpec((1, 2 * half, CMP_HIDDEN), lambda s, i: (s, 0, 0)),
                  pl.BlockSpec((1, CMP_HIDDEN, NSA_DIM), lambda s, i: (s, 0, 0))],
        out_specs=pl.BlockSpec((1, 1, nc, NSA_DIM), lambda s, i: (s, i, 0, 0)),
        out_shape=jax.ShapeDtypeStruct((2, bg, nc, NSA_DIM), F32),
        compiler_params=_cparams("parallel", "parallel"),
        name="nsa_compress",
    )(y, pe, w1, w2)


def _slope_table():
    slopes = np.exp2(-8.0 * np.arange(1, NSA_HEADS + 1, dtype=np.float64) / NSA_HEADS).astype(np.float32)
    tab = np.zeros((NSA_GROUPS, 8, 512), np.float32)
    tab[:, :NSA_REP, :] = slopes.reshape(NSA_GROUPS, NSA_REP)[:, :, None]
    return jnp.asarray(tab)


def _cmp_kernel(slope_ref, wmap_ref, q_ref, kc_ref, vc_ref, gate_ref, o_ref, sel_ref, used_ref, score_ref,
                *, tq, nc, nslc):
    t0 = pl.program_id(2) * tq
    kc = kc_ref[0, 0]
    vc = vc_ref[0, 0].astype(BF16)
    gates = _sigmoid(gate_ref[0, 0])
    t_i = t0 + lax.broadcasted_iota(jnp.int32, (tq, nc), 0)
    n_i = lax.broadcasted_iota(jnp.int32, (tq, nc), 1)
    dist = t_i - (n_i * CMP_STRIDE + CMP_BLOCK - 1)
    valid = (dist >= 0) & (n_i < nc - 1)
    distf = dist.astype(F32)
    imp = jnp.zeros((tq, nc), F32)
    for r in range(NSA_REP):
        q = q_ref[:, r * NSA_DIM:(r + 1) * NSA_DIM]
        slope = slope_ref[0, r:r + 1, :nc]
        s = _dot_nt(q, kc, precision=HIGHEST) * (NSA_DIM ** -0.5) - slope * distf
        s = jnp.where(valid, s, NEG)
        m = jnp.max(s, axis=-1, keepdims=True)
        e = jnp.where(valid, jnp.exp(s - m), 0.0)
        den = jnp.sum(e, axis=-1, keepdims=True)
        p = e / jnp.where(den > 0, den, 1.0)
        o_ref[:, r * NSA_DIM:(r + 1) * NSA_DIM] = gates[:, 3 * r:3 * r + 1] * _dot(p.astype(BF16), vc)
        imp = imp + p
    imp_t = _dot_nt(wmap_ref[...], imp, precision=HIGHEST)
    blk = lax.broadcasted_iota(jnp.int32, (nslc, tq), 0)
    cur = jnp.right_shift(t0 + lax.broadcasted_iota(jnp.int32, (nslc, tq), 1), SLC_SHIFT)
    forced = (blk == 0) | (blk == cur) | (blk == cur - 1)
    score = jnp.where(blk > cur, -jnp.inf, jnp.where(forced, jnp.inf, imp_t))
    score_ref[...] = score
    sub = 8
    groups = [score_ref[g0:g0 + sub, :] for g0 in range(0, nslc, sub)]
    ranks = [jnp.zeros((sub, tq), F32) for _ in groups]
    row8 = lax.broadcasted_iota(jnp.int32, (sub, tq), 0)
    for s_i in range(nslc):
        other = jnp.broadcast_to(score_ref[s_i:s_i + 1, :], (sub, tq))
        for gi, sc in enumerate(groups):
            g0 = gi * sub
            if g0 > s_i:
                beats = other >= sc
            elif g0 + sub - 1 <= s_i:
                beats = other > sc
            else:
                beats = (other > sc) | ((other == sc) & (row8 > s_i - g0))
            ranks[gi] = ranks[gi] + jnp.where(beats, 1.0, 0.0)
    rank = jnp.concatenate(ranks, axis=0)
    sel = ((rank < min(SLC_TOPK, nslc)) & (blk <= cur)).astype(F32)
    sel_ref[0, 0] = sel
    for a in range(tq // ATT_T):
        used_ref[0, 0, 0, :, a:a + 1] = jnp.max(sel[:, a * ATT_T:(a + 1) * ATT_T], axis=1, keepdims=True)


def _cmp_attention(za, kvc, gates, batch, seq, tq=256):
    nc = seq // CMP_STRIDE
    nslc = seq // SLC_BLOCK
    nq = seq // tq
    c_start = np.arange(nc) * CMP_STRIDE
    s_start = np.arange(nslc) * SLC_BLOCK
    overlap = np.clip(np.minimum(c_start[:, None] + CMP_BLOCK, s_start[None, :] + SLC_BLOCK)
                      - np.maximum(c_start[:, None], s_start[None, :]), 0, None)
    wmap_t = (overlap.astype(np.float32) / CMP_STRIDE).T.copy()
    wmap_t[:, nc - 1] = 0.0
    sub = tq // ATT_T
    return pl.pallas_call(
        functools.partial(_cmp_kernel, tq=tq, nc=nc, nslc=nslc),
        grid=(batch, NSA_GROUPS, nq),
        in_specs=[pl.BlockSpec((1, 8, 512), lambda b, g, i: (g, 0, 0)),
                  pl.BlockSpec((nslc, nc), lambda b, g, i: (0, 0)),
                  pl.BlockSpec((tq, NSA_GW), lambda b, g, i: (b * nq + i, OFF_BQ // NSA_GW + g)),
                  pl.BlockSpec((1, 1, nc, NSA_DIM), lambda b, g, i: (0, b * NSA_GROUPS + g, 0, 0)),
                  pl.BlockSpec((1, 1, nc, NSA_DIM), lambda b, g, i: (1, b * NSA_GROUPS + g, 0, 0)),
                  pl.BlockSpec((1, 1, tq, 3 * NSA_REP), lambda b, g, i: (b, g, i, 0))],
        out_specs=[pl.BlockSpec((tq, NSA_GW), lambda b, g, i: (b * nq + i, g)),
                   pl.BlockSpec((1, 1, nslc, tq), lambda b, g, i: (b, g, 0, i)),
                   pl.BlockSpec((1, 1, 1, nslc, sub), lambda b, g, i: (b, g, i, 0, 0))],
        out_shape=[jax.ShapeDtypeStruct((batch * seq, NSA_WIDTH), F32),
                   jax.ShapeDtypeStruct((batch, NSA_GROUPS, nslc, seq), F32),
                   jax.ShapeDtypeStruct((batch, NSA_GROUPS, nq, nslc, sub), F32)],
        scratch_shapes=[pltpu.VMEM((nslc, tq), F32)],
        compiler_params=_cparams("parallel", "parallel", "arbitrary"),
        name="nsa_cmp_select",
    )(_slope_table(), jnp.asarray(wmap_t), za, kvc, kvc, gates)


ATT_SUPER_SHIFT = 3
ATT_SUPER = 1 << ATT_SUPER_SHIFT
ALIBI_FEATS = 6


def _alibi_query_features():
    slopes = np.exp2(-8.0 * np.arange(1, NSA_HEADS + 1, dtype=np.float64) / NSA_HEADS).astype(np.float32)

    def top_bits(x):
        return (x.view(np.uint32) & np.uint32(0xFFFF0000)).view(np.float32)

    s1 = top_bits(slopes)
    r1 = slopes - s1
    s2 = top_bits(r1)
    s3 = r1 - s2
    rows = np.stack([-64.0 * s1, -64.0 * s2, -64.0 * s3, -s1, -s2, -s3]).astype(np.float32)
    feat = np.zeros((NSA_GROUPS, NSA_DIM, NSA_REP, ATT_T), np.float32)
    feat[:, :ALIBI_FEATS] = rows.reshape(ALIBI_FEATS, NSA_GROUPS, NSA_REP).transpose(1, 0, 2)[:, :, :, None]
    return jnp.asarray(feat.reshape(NSA_GROUPS, NSA_DIM, NSA_GW), dtype=BF16)


def _attend(k_ref, v_ref, chunks, valids, qi, qt_aug, kf_static, kf_ind, m_ref, l_ref, acc_ref):
    t = ATT_T
    ks, vs, mbs = [], [], []
    for c, valid in zip(chunks, valids):
        k0 = pl.multiple_of(c * t, t)
        kfeat = (kf_static + (2 * (qi - c)).astype(F32) * kf_ind).astype(BF16)
        ks.append(jnp.concatenate([k_ref[pl.ds(k0, t), :].astype(BF16), kfeat], axis=1))
        vs.append(v_ref[pl.ds(k0, t), :].astype(BF16))
        mbs.append(jnp.where(valid, 0.0, NEG))
    st = _dot(jnp.concatenate(ks, axis=0), qt_aug)
    mbias = jnp.concatenate(mbs, axis=0)
    ps, alphas = [], []
    for r in range(NSA_REP):
        cs = slice(r * t, (r + 1) * t)
        s = st[:, cs] + mbias
        m_old = m_ref[:, cs]
        m_new = jnp.maximum(m_old, jnp.max(s, axis=0, keepdims=True))
        alphas.append(jnp.exp(m_old - m_new))
        p = jnp.exp(s - m_new)
        l_ref[:, cs] = alphas[-1] * l_ref[:, cs] + jnp.sum(p, axis=0, keepdims=True)
        m_ref[:, cs] = m_new
        ps.append(p.astype(BF16))
    acc_ref[...] = (jnp.concatenate(alphas, axis=1) * acc_ref[...]
                    + _dot_tn(jnp.concatenate(vs, axis=0), jnp.concatenate(ps, axis=1)))


def _slc_win_kernel(ids_ref, cnt_ref, qfeat_ref, q_ref, ks_ref, vs_ref, kw_ref, vw_ref, sel_ref, gate_ref,
                    ocmp_ref, o_ref, m_ref, l_ref, acc_ref, *, nq):
    t = ATT_T
    b, g, qi = pl.program_id(0), pl.program_id(1), pl.program_id(2)
    t0 = qi * t
    qt = jnp.concatenate([q_ref[:, r * NSA_DIM:(r + 1) * NSA_DIM].T for r in range(NSA_REP)], axis=1)
    qt_aug = jnp.concatenate([(qt * (NSA_DIM ** -0.5)).astype(BF16), qfeat_ref[0]], axis=0)
    key_i = lax.broadcasted_iota(jnp.int32, (t, t), 0)
    qry_i = lax.broadcasted_iota(jnp.int32, (t, t), 1)
    back = (t - 1) - key_i
    kf_ind = (qry_i < 3).astype(F32)
    kf_static = jnp.where(qry_i < 3, jnp.right_shift(back, SLC_SHIFT),
                          jnp.where(qry_i < ALIBI_FEATS, jnp.bitwise_and(back, SLC_BLOCK - 1), 0)).astype(F32)

    def reset():
        m_ref[...] = jnp.full_like(m_ref, NEG)
        l_ref[...] = jnp.zeros_like(l_ref)
        acc_ref[...] = jnp.zeros_like(acc_ref)

    reset()
    tile = (b * NSA_GROUPS + g) * nq + qi
    cnt = cnt_ref[tile]
    half = SLC_BLOCK
    never = jnp.int32(2 ** 30)

    def slc_body(s, carry):
        chunks, valids = [], []
        for u in range(ATT_SUPER):
            e = s * ATT_SUPER + u
            c = ids_ref[tile * nq + jnp.maximum(jnp.minimum(e, cnt - 1), 0)]
            lo = jnp.broadcast_to(sel_ref[0, 0, pl.ds(2 * c, 1), :], (half, t))
            hi = jnp.broadcast_to(sel_ref[0, 0, pl.ds(2 * c + 1, 1), :], (half, t))
            picked = jnp.concatenate([lo, hi], axis=0) > 0.5
            dist = (t0 + qry_i) - (c * t + key_i)
            chunks.append(c)
            valids.append(picked & (dist >= jnp.where(e < cnt, 0, never)))
        _attend(ks_ref, vs_ref, chunks, valids, qi, qt_aug, kf_static, kf_ind, m_ref, l_ref, acc_ref)
        return carry

    lax.fori_loop(0, jnp.right_shift(cnt + (ATT_SUPER - 1), ATT_SUPER_SHIFT), slc_body, 0)
    o_slc = acc_ref[...] * (1.0 / l_ref[...])

    reset()
    chunks, valids = [], []
    for i in range(WIN_SIZE // t + 1):
        c = jnp.maximum(qi - i, 0)
        dist = (t0 + qry_i) - (c * t + key_i)
        chunks.append(c)
        valids.append((dist >= jnp.where(qi - i >= 0, 0, never)) & (dist < WIN_SIZE))
    _attend(kw_ref, vw_ref, chunks, valids, qi, qt_aug, kf_static, kf_ind, m_ref, l_ref, acc_ref)
    o_win = acc_ref[...] * (1.0 / l_ref[...])

    gates = _sigmoid(gate_ref[0, 0])
    for r in range(NSA_REP):
        cs = slice(r * t, (r + 1) * t)
        o_t = gates[3 * r + 1:3 * r + 2, :] * o_slc[:, cs] + gates[3 * r + 2:3 * r + 3, :] * o_win[:, cs]
        o_ref[:, cs] = (o_t.T + ocmp_ref[:, cs]).astype(o_ref.dtype)


def _slc_win_attention(ids, cnt, za, sel_t, gates_t, o_cmp, batch, seq):
    t = ATT_T
    nslc = seq // SLC_BLOCK
    nq = seq // t

    def kv(off):
        return pl.BlockSpec((seq, NSA_DIM), lambda b, g, i, *_: (b, off // NSA_DIM + g))

    grid_spec = pltpu.PrefetchScalarGridSpec(
        num_scalar_prefetch=2,
        grid=(batch, NSA_GROUPS, nq),
        in_specs=[pl.BlockSpec((1, NSA_DIM, NSA_GW), lambda b, g, i, *_: (g, 0, 0)),
                  pl.BlockSpec((t, NSA_GW), lambda b, g, i, *_: (b * nq + i, OFF_BQ // NSA_GW + g)),
                  kv(OFF_BKS), kv(OFF_BVS), kv(OFF_BKW), kv(OFF_BVW),
                  pl.BlockSpec((1, 1, nslc, t), lambda b, g, i, *_: (b, g, 0, i)),
                  pl.BlockSpec((1, 1, 3 * NSA_REP, t), lambda b, g, i, *_: (b, g, 0, i)),
                  pl.BlockSpec((t, NSA_GW), lambda b, g, i, *_: (b * nq + i, g))],
        out_specs=pl.BlockSpec((t, NSA_GW), lambda b, g, i, *_: (b * nq + i, g)),
        scratch_shapes=[pltpu.VMEM((1, NSA_GW), F32), pltpu.VMEM((1, NSA_GW), F32),
                        pltpu.VMEM((NSA_DIM, NSA_GW), F32)])
    return pl.pallas_call(
        functools.partial(_slc_win_kernel, nq=nq),
        grid_spec=grid_spec,
        out_shape=jax.ShapeDtypeStruct((batch * seq, NSA_WIDTH), BF16),
        compiler_params=_cparams("parallel", "parallel", "arbitrary"),
        name="nsa_slc_win",
    )(ids, cnt, _alibi_query_features(), za, za, za, za, za, sel_t, gates_t, o_cmp)


def _nsa(za, pe, w1, w2, batch, seq):
    nc = seq // CMP_STRIDE
    t = ATT_T
    nq = seq // t

    def grouped(off):
        return za[:, off:off + NSA_KV_WIDTH].reshape(batch, seq, NSA_GROUPS, NSA_DIM).transpose(0, 2, 1, 3)

    def blocks(off):
        return grouped(off).reshape(batch * NSA_GROUPS, nc, CMP_STRIDE * NSA_DIM)

    kvc = _compress(jnp.stack([blocks(OFF_BKC), blocks(OFF_BVC)]), pe, w1, w2)
    gates = (za[:, OFF_BGATE:OFF_BGATE + GATE_COLS].reshape(batch, seq, NSA_GROUPS, 3 * NSA_REP)
             .transpose(0, 2, 1, 3))
    o_cmp, sel_t, used = _cmp_attention(za, kvc, gates, batch, seq)
    nslc = seq // SLC_BLOCK
    used = used.transpose(0, 1, 2, 4, 3).reshape(batch * NSA_GROUPS * nq, nslc // 2, 2)
    unused = (jnp.max(used, axis=-1) < 0.5).astype(jnp.int32)
    ids = jnp.argsort(unused, axis=-1, stable=True).astype(jnp.int32).reshape(-1)
    cnt = (nslc // 2 - jnp.sum(unused, axis=-1)).astype(jnp.int32)
    return _slc_win_attention(ids, cnt, za, sel_t, gates.transpose(0, 1, 3, 2), o_cmp, batch, seq)


def _layer(x, layer, norm1_w, w_in_t, lb_table, hgrn_norm_w, pe, w1, w2, w_branch, w_out, norm2_w, w_ff1, w_ff2_bf,
           batch, seq):
    h = _rmsnorm(x, norm1_w, BF16)
    wide = dict(tn=PROJ_TN, a_single_buffer=True)
    za = _ar_matmul(h, w_in_t, layer, F32, n_cols=ZA_WIDTH, transposed=True, name="proj_in_a", **wide)
    zb = _ar_matmul(h, w_in_t, layer, F32, n_cols=ZB_WIDTH, col0=OFF_BGATE + GATE_COLS, transposed=True,
                    name="proj_in_b", **wide)
    o_a = _hgrn2(za, lb_table, hgrn_norm_w, layer, batch, seq)
    o_b = _nsa(za, pe, w1, w2, batch, seq)
    o_c = _retention(zb, batch, seq)
    merged = _branch_merge(o_a, o_b, o_c, w_branch, layer, zb)
    x = _ar_matmul(merged, w_out, layer, F32, n_cols=D_MODEL, res=x, tm=1024, name="proj_out")
    h = _rmsnorm(x, norm2_w, BF16)
    u = _ar_matmul(h, w_ff1, layer, BF16, n_cols=D_FF, relu2=True, name="ffn_up", **wide)
    return _ar_matmul(u, w_ff2_bf, layer, F32, n_cols=D_MODEL, res=x, tm=512, name="ffn_down")


def kernel(x, norm1_w, w_in, hgrn_lb_table, hgrn_norm_w, cmp_pe_k, cmp_pe_v, cmp_w1_k, cmp_w1_v, cmp_w2_k,
           cmp_w2_v, w_branch, w_out, norm2_w, w_ff1, w_ff2, final_norm_w):
    batch, seq, d = x.shape
    xf = x.reshape(batch * seq, d)
    w_ff2_bf = w_ff2.astype(BF16)
    w_in_t = jnp.swapaxes(w_in, 1, 2)
    for l in range(DEPTH):
        pe = jnp.stack([cmp_pe_k[l].reshape(1, -1), cmp_pe_v[l].reshape(1, -1)])
        w1 = jnp.stack([cmp_w1_k[l], cmp_w1_v[l]]).astype(BF16)
        w2 = jnp.stack([cmp_w2_k[l], cmp_w2_v[l]]).astype(BF16)
        xf = _layer(xf, l, norm1_w[l], w_in_t, hgrn_lb_table, hgrn_norm_w[l], pe, w1, w2, w_branch, w_out,
                    norm2_w[l], w_ff1, w_ff2_bf, batch, seq)
    return _rmsnorm(xf, final_norm_w, F32).reshape(batch, seq, d)
```

```python
import functools

import numpy as np
import jax
import jax.numpy as jnp
from jax import lax
from jax.experimental import pallas as pl
from jax.experimental.pallas import tpu as pltpu

F32 = jnp.float32
BF16 = jnp.bfloat16
HIGHEST = lax.Precision.HIGHEST

D_MODEL = 4096
DEPTH = 2
NORM_EPS = 1e-6
LANE = 128

HG_HEADS = 16
HG_DIM = 128
HG_WIDTH = HG_HEADS * HG_DIM
HG_CHUNK = 64
HG_SUB = 16
HG_HPB = 2
NSA_HEADS = 16
NSA_GROUPS = 4
NSA_REP = NSA_HEADS // NSA_GROUPS
NSA_DIM = 128
NSA_WIDTH = NSA_HEADS * NSA_DIM
NSA_KV_WIDTH = NSA_GROUPS * NSA_DIM
NSA_GW = NSA_REP * NSA_DIM
CMP_BLOCK = 32
CMP_STRIDE = 16
CMP_HIDDEN = 256
SLC_BLOCK = 64
SLC_SHIFT = 6
SLC_TOPK = 16
WIN_SIZE = 512
ATT_T = 128
RET_HEADS = 8
RET_QK = 128
RET_V = 256
RET_QK_WIDTH = RET_HEADS * RET_QK
RET_V_WIDTH = RET_HEADS * RET_V
RET_CHUNK = 128
MIX_WIDTH = HG_WIDTH + NSA_WIDTH + RET_V_WIDTH
D_FF = 4 * D_MODEL

PROJ_TN = 512
GATE_COLS = NSA_HEADS * 3
OFF_AQ = 0
OFF_AF = OFF_AQ + HG_WIDTH
OFF_AI = OFF_AF + HG_WIDTH
OFF_AG = OFF_AI + HG_WIDTH
OFF_BQ = OFF_AG + HG_WIDTH
OFF_BKC = OFF_BQ + NSA_WIDTH
OFF_BVC = OFF_BKC + NSA_KV_WIDTH
OFF_BKS = OFF_BVC + NSA_KV_WIDTH
OFF_BVS = OFF_BKS + NSA_KV_WIDTH
OFF_BKW = OFF_BVS + NSA_KV_WIDTH
OFF_BVW = OFF_BKW + NSA_KV_WIDTH
OFF_BGATE = OFF_BVW + NSA_KV_WIDTH
ZA_WIDTH = OFF_BGATE + PROJ_TN
ZB_CQ = 0
ZB_CK = ZB_CQ + RET_QK_WIDTH
ZB_CV = ZB_CK + RET_QK_WIDTH
ZB_CG = ZB_CV + RET_V_WIDTH
ZB_MA = ZB_CG + RET_V_WIDTH
ZB_MB = ZB_MA + D_MODEL
ZB_MC = ZB_MB + D_MODEL
ZB_WIDTH = ZB_MC + D_MODEL

NEG = -1e30
VMEM_LIMIT = 56 * 1024 * 1024


def _cparams(*sem):
    return pltpu.CompilerParams(dimension_semantics=sem, vmem_limit_bytes=VMEM_LIMIT)


def _sigmoid(x):
    return 1.0 / (1.0 + jnp.exp(-x))


def _silu(x):
    return x * _sigmoid(x)


def _dot(a, b, precision=None):
    return jnp.dot(a, b, preferred_element_type=F32, precision=precision)


def _dot_nt(a, b, precision=None):
    return lax.dot_general(a, b, (((1,), (1,)), ((), ())), preferred_element_type=F32, precision=precision)


def _dot_tn(a, b):
    return lax.dot_general(a, b, (((0,), (0,)), ((), ())), preferred_element_type=F32)


def _rmsnorm_kernel(x_ref, w_ref, o_ref):
    x = x_ref[...]
    y = x * lax.rsqrt(jnp.mean(x * x, axis=-1, keepdims=True) + NORM_EPS)
    o_ref[...] = (y * w_ref[...]).astype(o_ref.dtype)


def _rmsnorm(x, w, out_dtype, tm=256):
    m, d = x.shape
    return pl.pallas_call(
        _rmsnorm_kernel,
        grid=(m // tm,),
        in_specs=[pl.BlockSpec((tm, d), lambda i: (i, 0)), pl.BlockSpec((1, d), lambda i: (0, 0))],
        out_specs=pl.BlockSpec((tm, d), lambda i: (i, 0)),
        out_shape=jax.ShapeDtypeStruct((m, d), out_dtype),
        compiler_params=_cparams("parallel"),
        name="rmsnorm",
    )(x, w.reshape(1, d))


CAST_K = 512


def _cast_dot(a_ref, w_ref, transposed):
    if w_ref.dtype == BF16:
        return _dot_nt(a_ref[...], w_ref[0]) if transposed else _dot(a_ref[...], w_ref[0])
    acc = None
    for k0 in range(0, a_ref.shape[1], CAST_K):
        sl = slice(k0, k0 + CAST_K)
        if transposed:
            part = _dot_nt(a_ref[:, sl], w_ref[0, :, sl].astype(BF16))
        else:
            part = _dot(a_ref[:, sl], w_ref[0, sl, :].astype(BF16))
        acc = part if acc is None else acc + part
    return acc


def _ar_kernel(*refs, transposed, relu2, residual):
    a_ref, w_ref, o_ref = refs[0], refs[1], refs[-1]
    acc = _cast_dot(a_ref, w_ref, transposed)
    if relu2:
        acc = jnp.maximum(acc, 0.0)
        acc = acc * acc
    if residual:
        acc = acc + refs[2][...]
    o_ref[...] = acc.astype(o_ref.dtype)


def _ar_matmul(a, w, layer, out_dtype, *, n_cols, col0=0, transposed=False, relu2=False, res=None, tm=2048,
               tn=256, a_single_buffer=False, name="ar_matmul"):
    m, k = a.shape
    a_mode = dict(pipeline_mode=pl.Buffered(1)) if a_single_buffer else {}
    if transposed:
        w_spec = pl.BlockSpec((pl.Element(1), pl.Element(tn), pl.Element(k)),
                              lambda i, j: (layer, pl.multiple_of(col0 + j * tn, 8), 0))
    else:
        w_spec = pl.BlockSpec((1, k, tn), lambda i, j: (layer, 0, col0 // tn + j))
    in_specs = [pl.BlockSpec((tm, k), lambda i, j: (i, 0), **a_mode), w_spec]
    args = [a, w]
    if res is not None:
        in_specs.append(pl.BlockSpec((tm, tn), lambda i, j: (i, j)))
        args.append(res)
    return pl.pallas_call(
        functools.partial(_ar_kernel, transposed=transposed, relu2=relu2, residual=res is not None),
        grid=(m // tm, n_cols // tn),
        in_specs=in_specs,
        out_specs=pl.BlockSpec((tm, tn), lambda i, j: (i, j)),
        out_shape=jax.ShapeDtypeStruct((m, n_cols), out_dtype),
        compiler_params=_cparams("parallel", "arbitrary"),
        name=name,
    )(*args)


def _mm_res_kernel(a_ref, b_ref, r_ref, o_ref, acc_ref, *, nk):
    kk = pl.program_id(2)
    part = _dot(a_ref[...], b_ref[0])

    @pl.when(kk == 0)
    def _():
        acc_ref[...] = part

    @pl.when(kk > 0)
    def _():
        acc_ref[...] += part

    @pl.when(kk == nk - 1)
    def _():
        o_ref[...] = r_ref[...] + acc_ref[...]


def _matmul_residual(a, b, layer, res, *, tm=1024, tn=1024, tk=2048, name="matmul_res"):
    m, k = a.shape
    n = b.shape[2]
    nk = k // tk
    return pl.pallas_call(
        functools.partial(_mm_res_kernel, nk=nk),
        grid=(m // tm, n // tn, nk),
        in_specs=[pl.BlockSpec((tm, tk), lambda i, j, q: (i, q)),
                  pl.BlockSpec((1, tk, tn), lambda i, j, q: (layer, q, j)),
                  pl.BlockSpec((tm, tn), lambda i, j, q: (i, j))],
        out_specs=pl.BlockSpec((tm, tn), lambda i, j, q: (i, j)),
        out_shape=jax.ShapeDtypeStruct((m, n), F32),
        scratch_shapes=[pltpu.VMEM((tm, tn), F32)],
        compiler_params=_cparams("parallel", "arbitrary", "arbitrary"),
        name=name,
    )(a, b, res)


def _merge_kernel(oa_ref, ob_ref, oc_ref, wa_ref, wb_ref, wc_ref, ga_ref, gb_ref, gc_ref, o_ref):
    acc = _sigmoid(ga_ref[...]) * _cast_dot(oa_ref, wa_ref, False)
    acc += _sigmoid(gb_ref[...]) * _cast_dot(ob_ref, wb_ref, False)
    acc += _sigmoid(gc_ref[...]) * _cast_dot(oc_ref, wc_ref, False)
    o_ref[...] = acc.astype(o_ref.dtype)


def _branch_merge(o_a, o_b, o_c, w_branch, layer, zb, *, tm=1024, tn=256):
    m, kb = o_a.shape
    o_spec = pl.BlockSpec((tm, kb), lambda i, j: (i, 0))

    def w_spec(r):
        return pl.BlockSpec((1, kb, tn), lambda i, j: (layer, r, j))

    def g_spec(off):
        return pl.BlockSpec((tm, tn), lambda i, j: (i, off // tn + j))

    return pl.pallas_call(
        _merge_kernel,
        grid=(m // tm, D_MODEL // tn),
        in_specs=[o_spec, o_spec, o_spec, w_spec(0), w_spec(1), w_spec(2),
                  g_spec(ZB_MA), g_spec(ZB_MB), g_spec(ZB_MC)],
        out_specs=pl.BlockSpec((tm, tn), lambda i, j: (i, j)),
        out_shape=jax.ShapeDtypeStruct((m, D_MODEL), BF16),
        compiler_params=_cparams("parallel", "arbitrary"),
        name="branch_merge",
    )(o_a, o_b, o_c, w_branch, w_branch, w_branch, zb, zb, zb)


def _hgrn_chunk(q_in, f_in, v, g, lb, nw, st, tri):
    c = HG_CHUNK
    qc = _silu(q_in) * (HG_DIM ** -0.5)
    f = lb + (1.0 - lb) * _sigmoid(f_in)
    kk = 1.0 - f
    b = _dot(tri, jnp.log(f), precision=HIGHEST)
    o = _dot_nt((qc * jnp.exp(b)).astype(BF16), st.astype(BF16))
    b_last = b[c - 1:c, :]
    kdec = kk * jnp.exp(b_last - b)
    v16 = v.astype(BF16)
    st_new = st * jnp.exp(b_last) + _dot_tn(v16, kdec.astype(BF16))
    row = lax.broadcasted_iota(jnp.int32, (HG_SUB, HG_DIM), 0)
    col = lax.broadcasted_iota(jnp.int32, (HG_SUB, c), 1)
    parts = []
    for a in range(c // HG_SUB):
        lo = a * HG_SUB
        ba = b[lo:lo + HG_SUB, :]
        qa = qc[lo:lo + HG_SUB, :]
        acc = jnp.zeros((HG_SUB, HG_DIM), F32)
        if a > 0:
            bref = b[lo:lo + 1, :]
            qn = qa * jnp.exp(ba - bref)
            kn = kk * jnp.exp(jnp.minimum(bref - b, 0.0))
            att = _dot_nt(qn.astype(BF16), kn.astype(BF16))
            acc = _dot(jnp.where(col < lo, att, 0.0).astype(BF16), v16)
        for j in range(HG_SUB):
            jj = lo + j
            d = jnp.where(row >= j, ba - b[jj:jj + 1, :], NEG)
            w = jnp.sum(qa * kk[jj:jj + 1, :] * jnp.exp(d), axis=-1, keepdims=True)
            acc = acc + w * v[jj:jj + 1, :]
        parts.append(acc)
    o = o + jnp.concatenate(parts, axis=0)
    o = o * lax.rsqrt(jnp.mean(o * o, axis=-1, keepdims=True) + NORM_EPS) * nw * _silu(g)
    return o, st_new


def _hgrn_kernel(tab_ref, nw_ref, q_ref, f_ref, v_ref, g_ref, o_ref, st_ref, *, layer, rows):
    @pl.when(pl.program_id(2) == 0)
    def _():
        st_ref[...] = jnp.zeros_like(st_ref)

    tab = tab_ref[...]
    e = jnp.exp(tab - jnp.max(tab, axis=0, keepdims=True))
    p = e / jnp.sum(e, axis=0, keepdims=True)
    lb = jnp.sum(p[:layer + 1, :], axis=0, keepdims=True) - p[0:1, :]
    nw = nw_ref[...]
    r_i = lax.broadcasted_iota(jnp.int32, (HG_CHUNK, HG_CHUNK), 0)
    c_i = lax.broadcasted_iota(jnp.int32, (HG_CHUNK, HG_CHUNK), 1)
    tri = (r_i >= c_i).astype(F32)
    sts = [st_ref[hh] for hh in range(HG_HPB)]
    for ch in range(rows // HG_CHUNK):
        sl = slice(ch * HG_CHUNK, (ch + 1) * HG_CHUNK)
        outs = []
        for hh in range(HG_HPB):
            hs = slice(hh * HG_DIM, (hh + 1) * HG_DIM)
            o, sts[hh] = _hgrn_chunk(q_ref[sl, hs], f_ref[sl, hs], v_ref[sl, hs], g_ref[sl, hs], lb[:, hs], nw,
                                     sts[hh], tri)
            outs.append(o)
        o_ref[sl, :] = jnp.concatenate(outs, axis=1).astype(o_ref.dtype)
    for hh in range(HG_HPB):
        st_ref[hh] = sts[hh]


def _hgrn2(za, lb_table, norm_w, layer, batch, seq, rows=256):
    nrow = seq // rows
    width = HG_HPB * HG_DIM

    def col(off):
        return pl.BlockSpec((rows, width), lambda b, h, c: (b * nrow + c, off // width + h))

    return pl.pallas_call(
        functools.partial(_hgrn_kernel, layer=layer, rows=rows),
        grid=(batch, HG_HEADS // HG_HPB, nrow),
        in_specs=[pl.BlockSpec((DEPTH, width), lambda b, h, c: (0, h)),
                  pl.BlockSpec((1, HG_DIM), lambda b, h, c: (0, 0)),
                  col(OFF_AQ), col(OFF_AF), col(OFF_AI), col(OFF_AG)],
        out_specs=pl.BlockSpec((rows, width), lambda b, h, c: (b * nrow + c, h)),
        out_shape=jax.ShapeDtypeStruct((batch * seq, HG_WIDTH), BF16),
        scratch_shapes=[pltpu.VMEM((HG_HPB, HG_DIM, HG_DIM), F32)],
        compiler_params=_cparams("parallel", "parallel", "arbitrary"),
        name="hgrn2",
    )(lb_table, norm_w.reshape(1, HG_DIM), za, za, za, za)


def _ret_kernel(lg_ref, q_ref, k_ref, v_ref, g_ref, o_ref, s_ref, *, rows):
    @pl.when(pl.program_id(2) == 0)
    def _():
        s_ref[...] = jnp.zeros_like(s_ref)

    c = RET_CHUNK
    lg_v = lg_ref[0]
    lg_k = lg_v[:, :c]
    pos_r = lax.broadcasted_iota(jnp.int32, (c, RET_V), 0).astype(F32)
    query_decay = jnp.exp(lg_v * (pos_r + 1.0))
    rel = (lax.broadcasted_iota(jnp.int32, (c, c), 0) - lax.broadcasted_iota(jnp.int32, (c, c), 1)).astype(F32)
    intra_decay = jnp.where(rel >= 0, jnp.exp(lg_k * jnp.maximum(rel, 0.0)), 0.0)
    key_decay = jnp.exp(lg_k * (c - 1.0 - pos_r[:, :RET_QK]))
    chunk_decay = jnp.exp(lg_v * float(c))
    s = s_ref[...]
    for ch in range(rows // c):
        sl = slice(ch * c, (ch + 1) * c)
        q = q_ref[sl, :].astype(BF16)
        k = k_ref[sl, :] * (RET_QK ** -0.5)
        v = v_ref[sl, :].astype(BF16)
        inter = _dot(q, s.astype(BF16)) * query_decay
        scores = _dot_nt(q, k.astype(BF16)) * intra_decay
        o = inter + _dot(scores.astype(BF16), v)
        s = chunk_decay * s + _dot_tn((k * key_decay).astype(BF16), v)
        mu = jnp.mean(o, axis=-1, keepdims=True)
        oc = o - mu
        o = oc * lax.rsqrt(jnp.mean(oc * oc, axis=-1, keepdims=True) + NORM_EPS)
        o_ref[sl, :] = (o * _silu(g_ref[sl, :])).astype(o_ref.dtype)
    s_ref[...] = s


def _retention(zb, batch, seq, rows=1024):
    rows = min(rows, seq)
    nrow = seq // rows
    log_gamma = jnp.log(1.0 - jnp.exp2(-5.0 - jnp.arange(RET_HEADS, dtype=F32)))
    lg = jnp.broadcast_to(log_gamma[:, None, None], (RET_HEADS, 1, RET_V))
    return pl.pallas_call(
        functools.partial(_ret_kernel, rows=rows),
        grid=(batch, RET_HEADS, nrow),
        in_specs=[pl.BlockSpec((1, 1, RET_V), lambda b, h, c: (h, 0, 0)),
                  pl.BlockSpec((rows, RET_QK), lambda b, h, c: (b * nrow + c, ZB_CQ // RET_QK + h)),
                  pl.BlockSpec((rows, RET_QK), lambda b, h, c: (b * nrow + c, ZB_CK // RET_QK + h)),
                  pl.BlockSpec((rows, RET_V), lambda b, h, c: (b * nrow + c, ZB_CV // RET_V + h)),
                  pl.BlockSpec((rows, RET_V), lambda b, h, c: (b * nrow + c, ZB_CG // RET_V + h))],
        out_specs=pl.BlockSpec((rows, RET_V), lambda b, h, c: (b * nrow + c, h)),
        out_shape=jax.ShapeDtypeStruct((batch * seq, RET_V_WIDTH), BF16),
        scratch_shapes=[pltpu.VMEM((RET_QK, RET_V), F32)],
        compiler_params=_cparams("parallel", "parallel", "arbitrary"),
        name="retention",
    )(lg, zb, zb, zb, zb)


def _compress_kernel(kv_ref, pe_ref, w1_ref, w2_ref, o_ref, *, nc):
    d = NSA_DIM
    top = bot = None
    for l in range(CMP_STRIDE):
        x = kv_ref[pl.ds(l, nc, stride=CMP_STRIDE), :]
        pt = _dot((x + pe_ref[0, l:l + 1, :]).astype(BF16), w1_ref[0, l * d:(l + 1) * d, :])
        lb = CMP_STRIDE + l
        pb = _dot((x + pe_ref[0, lb:lb + 1, :]).astype(BF16), w1_ref[0, lb * d:(lb + 1) * d, :])
        top = pt if top is None else top + pt
        bot = pb if bot is None else bot + pb
    h = top + pltpu.roll(bot, nc - 1, 0)
    o_ref[0, 0] = _dot(_silu(h).astype(BF16), w2_ref[0])


def _compress(za, pe, w1, w2, batch, seq):
    nc = seq // CMP_STRIDE
    kv_blocks = NSA_KV_WIDTH // NSA_DIM
    return pl.pallas_call(
        functools.partial(_compress_kernel, nc=nc),
        grid=(2, batch, NSA_GROUPS),
        in_specs=[pl.BlockSpec((seq, NSA_DIM), lambda s, b, g: (b, OFF_BKC // NSA_DIM + s * kv_blocks + g)),
                  pl.BlockSpec((1, CMP_BLOCK, NSA_DIM), lambda s, b, g: (s, 0, 0)),
                  pl.BlockSpec((1, CMP_BLOCK * NSA_DIM, CMP_HIDDEN), lambda s, b, g: (s, 0, 0)),
                  pl.BlockSpec((1, CMP_HIDDEN, NSA_DIM), lambda s, b, g: (s, 0, 0))],
        out_specs=pl.BlockSpec((1, 1, nc, NSA_DIM), lambda s, b, g: (s, b * NSA_GROUPS + g, 0, 0)),
        out_shape=jax.ShapeDtypeStruct((2, batch * NSA_GROUPS, nc, NSA_DIM), F32),
        compiler_params=_cparams("parallel", "parallel", "parallel"),
        name="nsa_compress",
    )(za, pe, w1, w2)


def _slope_table():
    slopes = np.exp2(-8.0 * np.arange(1, NSA_HEADS + 1, dtype=np.float64) / NSA_HEADS).astype(np.float32)
    tab = np.zeros((NSA_GROUPS, 8, 512), np.float32)
    tab[:, :NSA_REP, :] = slopes.reshape(NSA_GROUPS, NSA_REP)[:, :, None]
    return jnp.asarray(tab)


def _cmp_body(slope_ref, wmap_ref, q_ref, kc_ref, vc_ref, gate_ref, o_ref, sel_ref, used_ref, score_ref,
              t0, ncols, nblk, *, tq, nc, nslc):
    kc = kc_ref[0, 0, :ncols, :]
    vc = vc_ref[0, 0, :ncols, :].astype(BF16)
    gates = _sigmoid(gate_ref[0, 0])
    t_i = t0 + lax.broadcasted_iota(jnp.int32, (tq, ncols), 0)
    n_i = lax.broadcasted_iota(jnp.int32, (tq, ncols), 1)
    dist = t_i - (n_i * CMP_STRIDE + CMP_BLOCK - 1)
    valid = (dist >= 0) & (n_i < nc - 1)
    distf = dist.astype(F32)
    imp = jnp.zeros((tq, ncols), F32)
    for r in range(NSA_REP):
        q = q_ref[:, r * NSA_DIM:(r + 1) * NSA_DIM]
        slope = slope_ref[0, r:r + 1, :ncols]
        s = _dot_nt(q, kc, precision=HIGHEST) * (NSA_DIM ** -0.5) - slope * distf
        s = jnp.where(valid, s, NEG)
        m = jnp.max(s, axis=-1, keepdims=True)
        e = jnp.where(valid, jnp.exp(s - m), 0.0)
        den = jnp.sum(e, axis=-1, keepdims=True)
        p = e / jnp.where(den > 0, den, 1.0)
        o_ref[:, r * NSA_DIM:(r + 1) * NSA_DIM] = gates[:, 3 * r:3 * r + 1] * _dot(p.astype(BF16), vc)
        imp = imp + p
    imp_t = _dot_nt(wmap_ref[:nblk, :ncols], imp, precision=HIGHEST)
    blk = lax.broadcasted_iota(jnp.int32, (nblk, tq), 0)
    cur = jnp.right_shift(t0 + lax.broadcasted_iota(jnp.int32, (nblk, tq), 1), SLC_SHIFT)
    forced = (blk == 0) | (blk == cur) | (blk == cur - 1)
    score = jnp.where(blk > cur, -jnp.inf, jnp.where(forced, jnp.inf, imp_t))
    score_ref[:nblk, :] = score
    sub = 8
    groups = [score_ref[g0:g0 + sub, :] for g0 in range(0, nblk, sub)]
    ranks = [jnp.zeros((sub, tq), F32) for _ in groups]
    row8 = lax.broadcasted_iota(jnp.int32, (sub, tq), 0)
    for s_i in range(nblk):
        other = jnp.broadcast_to(score_ref[s_i:s_i + 1, :], (sub, tq))
        for gi, sc in enumerate(groups):
            g0 = gi * sub
            if g0 > s_i:
                beats = other >= sc
            elif g0 + sub - 1 <= s_i:
                beats = other > sc
            else:
                beats = (other > sc) | ((other == sc) & (row8 > s_i - g0))
            ranks[gi] = ranks[gi] + jnp.where(beats, 1.0, 0.0)
    rank = jnp.concatenate(ranks, axis=0)
    sel = ((rank < min(SLC_TOPK, nslc)) & (blk <= cur)).astype(F32)
    if nblk < nslc:
        sel = jnp.concatenate([sel, jnp.zeros((nslc - nblk, tq), F32)], axis=0)
    sel_ref[0, 0] = sel
    for a in range(tq // ATT_T):
        used_ref[0, 0, 0, :, a:a + 1] = jnp.max(sel[:, a * ATT_T:(a + 1) * ATT_T], axis=1, keepdims=True)


def _cmp_kernel(*refs, tq, nc, nslc):
    t0 = pl.program_id(2) * tq
    early = t0 + tq <= (nslc // 2) * SLC_BLOCK

    @pl.when(early)
    def _():
        _cmp_body(*refs, t0, nc // 2, nslc // 2, tq=tq, nc=nc, nslc=nslc)

    @pl.when(jnp.logical_not(early))
    def _():
        _cmp_body(*refs, t0, nc, nslc, tq=tq, nc=nc, nslc=nslc)


def _cmp_attention(za, kvc, gates, batch, seq, tq=256):
    nc = seq // CMP_STRIDE
    nslc = seq // SLC_BLOCK
    nq = seq // tq
    c_start = np.arange(nc) * CMP_STRIDE
    s_start = np.arange(nslc) * SLC_BLOCK
    overlap = np.clip(np.minimum(c_start[:, None] + CMP_BLOCK, s_start[None, :] + SLC_BLOCK)
                      - np.maximum(c_start[:, None], s_start[None, :]), 0, None)
    wmap_t = (overlap.astype(np.float32) / CMP_STRIDE).T.copy()
    wmap_t[:, nc - 1] = 0.0
    sub = tq // ATT_T
    return pl.pallas_call(
        functools.partial(_cmp_kernel, tq=tq, nc=nc, nslc=nslc),
        grid=(batch, NSA_GROUPS, nq),
        in_specs=[pl.BlockSpec((1, 8, 512), lambda b, g, i: (g, 0, 0)),
                  pl.BlockSpec((nslc, nc), lambda b, g, i: (0, 0)),
                  pl.BlockSpec((tq, NSA_GW), lambda b, g, i: (b * nq + i, OFF_BQ // NSA_GW + g)),
                  pl.BlockSpec((1, 1, nc, NSA_DIM), lambda b, g, i: (0, b * NSA_GROUPS + g, 0, 0)),
                  pl.BlockSpec((1, 1, nc, NSA_DIM), lambda b, g, i: (1, b * NSA_GROUPS + g, 0, 0)),
                  pl.BlockSpec((1, 1, tq, 3 * NSA_REP), lambda b, g, i: (b, g, i, 0))],
        out_specs=[pl.BlockSpec((tq, NSA_GW), lambda b, g, i: (b * nq + i, g)),
                   pl.BlockSpec((1, 1, nslc, tq), lambda b, g, i: (b, g, 0, i)),
                   pl.BlockSpec((1, 1, 1, nslc, sub), lambda b, g, i: (b, g, i, 0, 0))],
        out_shape=[jax.ShapeDtypeStruct((batch * seq, NSA_WIDTH), F32),
                   jax.ShapeDtypeStruct((batch, NSA_GROUPS, nslc, seq), F32),
                   jax.ShapeDtypeStruct((batch, NSA_GROUPS, nq, nslc, sub), F32)],
        scratch_shapes=[pltpu.VMEM((nslc, tq), F32)],
        compiler_params=_cparams("parallel", "parallel", "arbitrary"),
        name="nsa_cmp_select",
    )(_slope_table(), jnp.asarray(wmap_t), za, kvc, kvc, gates)


ATT_SUPER_SHIFT = 3
ATT_SUPER = 1 << ATT_SUPER_SHIFT
ALIBI_FEATS = 6


def _alibi_query_features():
    slopes = np.exp2(-8.0 * np.arange(1, NSA_HEADS + 1, dtype=np.float64) / NSA_HEADS).astype(np.float32)

    def top_bits(x):
        return (x.view(np.uint32) & np.uint32(0xFFFF0000)).view(np.float32)

    s1 = top_bits(slopes)
    r1 = slopes - s1
    s2 = top_bits(r1)
    s3 = r1 - s2
    rows = np.stack([-64.0 * s1, -64.0 * s2, -64.0 * s3, -s1, -s2, -s3]).astype(np.float32)
    feat = np.zeros((NSA_GROUPS, NSA_DIM, NSA_REP, ATT_T), np.float32)
    feat[:, :ALIBI_FEATS] = rows.reshape(ALIBI_FEATS, NSA_GROUPS, NSA_REP).transpose(1, 0, 2)[:, :, :, None]
    return jnp.asarray(feat.reshape(NSA_GROUPS, NSA_DIM, NSA_GW), dtype=BF16)


def _attend(k_ref, v_ref, chunks, valids, qi, qt_aug, kf_static, kf_ind, m_ref, l_ref, acc_ref):
    t = ATT_T
    ks, vs, mbs = [], [], []
    for c, valid in zip(chunks, valids):
        k0 = pl.multiple_of(c * t, t)
        kfeat = (kf_static + (2 * (qi - c)).astype(F32) * kf_ind).astype(BF16)
        ks.append(jnp.concatenate([k_ref[pl.ds(k0, t), :].astype(BF16), kfeat], axis=1))
        vs.append(v_ref[pl.ds(k0, t), :].astype(BF16))
        mbs.append(jnp.where(valid, 0.0, NEG))
    st = _dot(jnp.concatenate(ks, axis=0), qt_aug)
    mbias = jnp.concatenate(mbs, axis=0)
    ps, alphas = [], []
    for r in range(NSA_REP):
        cs = slice(r * t, (r + 1) * t)
        s = st[:, cs] + mbias
        m_old = m_ref[:, cs]
        m_new = jnp.maximum(m_old, jnp.max(s, axis=0, keepdims=True))
        alphas.append(jnp.exp(m_old - m_new))
        p = jnp.exp(s - m_new)
        l_ref[:, cs] = alphas[-1] * l_ref[:, cs] + jnp.sum(p, axis=0, keepdims=True)
        m_ref[:, cs] = m_new
        ps.append(p.astype(BF16))
    acc_ref[...] = (jnp.concatenate(alphas, axis=1) * acc_ref[...]
                    + _dot_tn(jnp.concatenate(vs, axis=0), jnp.concatenate(ps, axis=1)))


def _slc_win_kernel(ids_ref, cnt_ref, qfeat_ref, q_ref, ks_ref, vs_ref, kw_ref, vw_ref, sel_ref, gate_ref,
                    ocmp_ref, o_ref, m_ref, l_ref, acc_ref, *, nq):
    t = ATT_T
    b, g, qi = pl.program_id(0), pl.program_id(1), pl.program_id(2)
    t0 = qi * t
    qt = jnp.concatenate([q_ref[:, r * NSA_DIM:(r + 1) * NSA_DIM].T for r in range(NSA_REP)], axis=1)
    qt_aug = jnp.concatenate([(qt * (NSA_DIM ** -0.5)).astype(BF16), qfeat_ref[0]], axis=0)
    key_i = lax.broadcasted_iota(jnp.int32, (t, t), 0)
    qry_i = lax.broadcasted_iota(jnp.int32, (t, t), 1)
    back = (t - 1) - key_i
    kf_ind = (qry_i < 3).astype(F32)
    kf_static = jnp.where(qry_i < 3, jnp.right_shift(back, SLC_SHIFT),
                          jnp.where(qry_i < ALIBI_FEATS, jnp.bitwise_and(back, SLC_BLOCK - 1), 0)).astype(F32)

    def reset():
        m_ref[...] = jnp.full_like(m_ref, NEG)
        l_ref[...] = jnp.zeros_like(l_ref)
        acc_ref[...] = jnp.zeros_like(acc_ref)

    reset()
    tile = (b * NSA_GROUPS + g) * nq + qi
    cnt = cnt_ref[tile]
    half = SLC_BLOCK
    never = jnp.int32(2 ** 30)

    def slc_body(s, carry):
        chunks, valids = [], []
        for u in range(ATT_SUPER):
            e = s * ATT_SUPER + u
            c = ids_ref[tile * nq + jnp.maximum(jnp.minimum(e, cnt - 1), 0)]
            lo = jnp.broadcast_to(sel_ref[0, 0, pl.ds(2 * c, 1), :], (half, t))
            hi = jnp.broadcast_to(sel_ref[0, 0, pl.ds(2 * c + 1, 1), :], (half, t))
            picked = jnp.concatenate([lo, hi], axis=0) > 0.5
            dist = (t0 + qry_i) - (c * t + key_i)
            chunks.append(c)
            valids.append(picked & (dist >= jnp.where(e < cnt, 0, never)))
        _attend(ks_ref, vs_ref, chunks, valids, qi, qt_aug, kf_static, kf_ind, m_ref, l_ref, acc_ref)
        return carry

    lax.fori_loop(0, jnp.right_shift(cnt + (ATT_SUPER - 1), ATT_SUPER_SHIFT), slc_body, 0)
    o_slc = acc_ref[...] * (1.0 / l_ref[...])

    reset()
    chunks, valids = [], []
    for i in range(WIN_SIZE // t + 1):
        c = jnp.maximum(qi - i, 0)
        dist = (t0 + qry_i) - (c * t + key_i)
        chunks.append(c)
        valids.append((dist >= jnp.where(qi - i >= 0, 0, never)) & (dist < WIN_SIZE))
    _attend(kw_ref, vw_ref, chunks, valids, qi, qt_aug, kf_static, kf_ind, m_ref, l_ref, acc_ref)
    o_win = acc_ref[...] * (1.0 / l_ref[...])

    gates = _sigmoid(gate_ref[0, 0])
    for r in range(NSA_REP):
        cs = slice(r * t, (r + 1) * t)
        o_t = gates[3 * r + 1:3 * r + 2, :] * o_slc[:, cs] + gates[3 * r + 2:3 * r + 3, :] * o_win[:, cs]
        o_ref[:, cs] = (o_t.T + ocmp_ref[:, cs]).astype(o_ref.dtype)


def _slc_win_attention(ids, cnt, za, sel_t, gates_t, o_cmp, batch, seq):
    t = ATT_T
    nslc = seq // SLC_BLOCK
    nq = seq // t

    def kv(off):
        return pl.BlockSpec((seq, NSA_DIM), lambda b, g, i, *_: (b, off // NSA_DIM + g))

    grid_spec = pltpu.PrefetchScalarGridSpec(
        num_scalar_prefetch=2,
        grid=(batch, NSA_GROUPS, nq),
        in_specs=[pl.BlockSpec((1, NSA_DIM, NSA_GW), lambda b, g, i, *_: (g, 0, 0)),
                  pl.BlockSpec((t, NSA_GW), lambda b, g, i, *_: (b * nq + i, OFF_BQ // NSA_GW + g)),
                  kv(OFF_BKS), kv(OFF_BVS), kv(OFF_BKW), kv(OFF_BVW),
                  pl.BlockSpec((1, 1, nslc, t), lambda b, g, i, *_: (b, g, 0, i)),
                  pl.BlockSpec((1, 1, 3 * NSA_REP, t), lambda b, g, i, *_: (b, g, 0, i)),
                  pl.BlockSpec((t, NSA_GW), lambda b, g, i, *_: (b * nq + i, g))],
        out_specs=pl.BlockSpec((t, NSA_GW), lambda b, g, i, *_: (b * nq + i, g)),
        scratch_shapes=[pltpu.VMEM((1, NSA_GW), F32), pltpu.VMEM((1, NSA_GW), F32),
                        pltpu.VMEM((NSA_DIM, NSA_GW), F32)])
    return pl.pallas_call(
        functools.partial(_slc_win_kernel, nq=nq),
        grid_spec=grid_spec,
        out_shape=jax.ShapeDtypeStruct((batch * seq, NSA_WIDTH), BF16),
        compiler_params=_cparams("parallel", "parallel", "arbitrary"),
        name="nsa_slc_win",
    )(ids, cnt, _alibi_query_features(), za, za, za, za, za, sel_t, gates_t, o_cmp)


def _nsa(za, pe, w1, w2, batch, seq):
    nq = seq // ATT_T
    kvc = _compress(za, pe, w1, w2, batch, seq)
    gates = (za[:, OFF_BGATE:OFF_BGATE + GATE_COLS].reshape(batch, seq, NSA_GROUPS, 3 * NSA_REP)
             .transpose(0, 2, 1, 3))
    o_cmp, sel_t, used = _cmp_attention(za, kvc, gates, batch, seq)
    nslc = seq // SLC_BLOCK
    used = used.transpose(0, 1, 2, 4, 3).reshape(batch * NSA_GROUPS * nq, nslc // 2, 2)
    unused = (jnp.max(used, axis=-1) < 0.5).astype(jnp.int32)
    ids = jnp.argsort(unused, axis=-1, stable=True).astype(jnp.int32).reshape(-1)
    cnt = (nslc // 2 - jnp.sum(unused, axis=-1)).astype(jnp.int32)
    return _slc_win_attention(ids, cnt, za, sel_t, gates.transpose(0, 1, 3, 2), o_cmp, batch, seq)


def _layer(x, layer, norm1_w, w_in_t, lb_table, hgrn_norm_w, pe, w1, w2, w_branch, w_out, norm2_w, w_ff1, w_ff2_bf,
           batch, seq):
    h = _rmsnorm(x, norm1_w, BF16)
    wide = dict(tn=PROJ_TN, a_single_buffer=True)
    za = _ar_matmul(h, w_in_t, layer, F32, n_cols=ZA_WIDTH, transposed=True, name="proj_in_a", **wide)
    zb = _ar_matmul(h, w_in_t, layer, F32, n_cols=ZB_WIDTH, col0=OFF_BGATE + GATE_COLS, transposed=True,
                    name="proj_in_b", **wide)
    o_a = _hgrn2(za, lb_table, hgrn_norm_w, layer, batch, seq)
    o_b = _nsa(za, pe, w1, w2, batch, seq)
    o_c = _retention(zb, batch, seq)
    merged = _branch_merge(o_a, o_b, o_c, w_branch, layer, zb)
    x = _ar_matmul(merged, w_out, layer, F32, n_cols=D_MODEL, res=x, tm=1024, name="proj_out")
    h = _rmsnorm(x, norm2_w, BF16)
    u = _ar_matmul(h, w_ff1, layer, BF16, n_cols=D_FF, relu2=True, name="ffn_up", **wide)
    return _ar_matmul(u, w_ff2_bf, layer, F32, n_cols=D_MODEL, res=x, tm=512, name="ffn_down")


def kernel(x, norm1_w, w_in, hgrn_lb_table, hgrn_norm_w, cmp_pe_k, cmp_pe_v, cmp_w1_k, cmp_w1_v, cmp_w2_k,
           cmp_w2_v, w_branch, w_out, norm2_w, w_ff1, w_ff2, final_norm_w):
    batch, seq, d = x.shape
    xf = x.reshape(batch * seq, d)
    w_ff2_bf = w_ff2.astype(BF16)
    w_in_t = jnp.swapaxes(w_in, 1, 2)
    for l in range(DEPTH):
        pe = jnp.stack([cmp_pe_k[l], cmp_pe_v[l]])
        w1 = jnp.stack([cmp_w1_k[l], cmp_w1_v[l]]).astype(BF16)
        w2 = jnp.stack([cmp_w2_k[l], cmp_w2_v[l]]).astype(BF16)
        xf = _layer(xf, l, norm1_w[l], w_in_t, hgrn_lb_table, hgrn_norm_w[l], pe, w1, w2, w_branch, w_out,
                    norm2_w[l], w_ff1, w_ff2_bf, batch, seq)
    return _rmsnorm(xf, final_norm_w, F32).reshape(batch, seq, d)
```

```python
import functools

import numpy as np
import jax
import jax.numpy as jnp
from jax import lax
from jax.experimental import pallas as pl
from jax.experimental.pallas import tpu as pltpu

F32 = jnp.float32
BF16 = jnp.bfloat16
HIGHEST = lax.Precision.HIGHEST

D_MODEL = 4096
DEPTH = 2
NORM_EPS = 1e-6
LANE = 128

HG_HEADS = 16
HG_DIM = 128
HG_WIDTH = HG_HEADS * HG_DIM
HG_CHUNK = 64
HG_SUB = 16
HG_HPB = 2
NSA_HEADS = 16
NSA_GROUPS = 4
NSA_REP = NSA_HEADS // NSA_GROUPS
NSA_DIM = 128
NSA_WIDTH = NSA_HEADS * NSA_DIM
NSA_KV_WIDTH = NSA_GROUPS * NSA_DIM
NSA_GW = NSA_REP * NSA_DIM
CMP_BLOCK = 32
CMP_STRIDE = 16
CMP_HIDDEN = 256
SLC_BLOCK = 64
SLC_SHIFT = 6
SLC_TOPK = 16
WIN_SIZE = 512
ATT_T = 128
RET_HEADS = 8
RET_QK = 128
RET_V = 256
RET_QK_WIDTH = RET_HEADS * RET_QK
RET_V_WIDTH = RET_HEADS * RET_V
RET_CHUNK = 128
MIX_WIDTH = HG_WIDTH + NSA_WIDTH + RET_V_WIDTH
D_FF = 4 * D_MODEL

PROJ_TN = 512
GATE_COLS = NSA_HEADS * 3
OFF_AQ = 0
OFF_AF = OFF_AQ + HG_WIDTH
OFF_AI = OFF_AF + HG_WIDTH
OFF_AG = OFF_AI + HG_WIDTH
OFF_BQ = OFF_AG + HG_WIDTH
OFF_BKC = OFF_BQ + NSA_WIDTH
OFF_BVC = OFF_BKC + NSA_KV_WIDTH
OFF_BKS = OFF_BVC + NSA_KV_WIDTH
OFF_BVS = OFF_BKS + NSA_KV_WIDTH
OFF_BKW = OFF_BVS + NSA_KV_WIDTH
OFF_BVW = OFF_BKW + NSA_KV_WIDTH
OFF_BGATE = OFF_BVW + NSA_KV_WIDTH
ZA_WIDTH = OFF_BGATE + PROJ_TN
ZB_CQ = 0
ZB_CK = ZB_CQ + RET_QK_WIDTH
ZB_CV = ZB_CK + RET_QK_WIDTH
ZB_CG = ZB_CV + RET_V_WIDTH
ZB_MA = ZB_CG + RET_V_WIDTH
ZB_MB = ZB_MA + D_MODEL
ZB_MC = ZB_MB + D_MODEL
ZB_WIDTH = ZB_MC + D_MODEL

NEG = -1e30
VMEM_LIMIT = 56 * 1024 * 1024


def _cparams(*sem):
    return pltpu.CompilerParams(dimension_semantics=sem, vmem_limit_bytes=VMEM_LIMIT)


def _sigmoid(x):
    return 1.0 / (1.0 + jnp.exp(-x))


def _silu(x):
    return x * _sigmoid(x)


def _dot(a, b, precision=None):
    return jnp.dot(a, b, preferred_element_type=F32, precision=precision)


def _dot_nt(a, b, precision=None):
    return lax.dot_general(a, b, (((1,), (1,)), ((), ())), preferred_element_type=F32, precision=precision)


def _dot_tn(a, b):
    return lax.dot_general(a, b, (((0,), (0,)), ((), ())), preferred_element_type=F32)


def _rmsnorm_kernel(x_ref, w_ref, o_ref):
    x = x_ref[...]
    y = x * lax.rsqrt(jnp.mean(x * x, axis=-1, keepdims=True) + NORM_EPS)
    o_ref[...] = (y * w_ref[...]).astype(o_ref.dtype)


def _rmsnorm(x, w, out_dtype, tm=256):
    m, d = x.shape
    return pl.pallas_call(
        _rmsnorm_kernel,
        grid=(m // tm,),
        in_specs=[pl.BlockSpec((tm, d), lambda i: (i, 0)), pl.BlockSpec((1, d), lambda i: (0, 0))],
        out_specs=pl.BlockSpec((tm, d), lambda i: (i, 0)),
        out_shape=jax.ShapeDtypeStruct((m, d), out_dtype),
        compiler_params=_cparams("parallel"),
        name="rmsnorm",
    )(x, w.reshape(1, d))


CAST_K = 512


def _cast_dot(a_ref, w_ref, transposed):
    if w_ref.dtype == BF16:
        return _dot_nt(a_ref[...], w_ref[0]) if transposed else _dot(a_ref[...], w_ref[0])
    acc = None
    for k0 in range(0, a_ref.shape[1], CAST_K):
        sl = slice(k0, k0 + CAST_K)
        if transposed:
            part = _dot_nt(a_ref[:, sl], w_ref[0, :, sl].astype(BF16))
        else:
            part = _dot(a_ref[:, sl], w_ref[0, sl, :].astype(BF16))
        acc = part if acc is None else acc + part
    return acc


def _ar_kernel(*refs, transposed, relu2, residual):
    a_ref, w_ref, o_ref = refs[0], refs[1], refs[-1]
    acc = _cast_dot(a_ref, w_ref, transposed)
    if relu2:
        acc = jnp.maximum(acc, 0.0)
        acc = acc * acc
    if residual:
        acc = acc + refs[2][...]
    o_ref[...] = acc.astype(o_ref.dtype)


def _ar_matmul(a, w, layer, out_dtype, *, n_cols, col0=0, transposed=False, relu2=False, res=None, tm=2048,
               tn=256, a_single_buffer=False, name="ar_matmul"):
    m, k = a.shape
    a_mode = dict(pipeline_mode=pl.Buffered(1)) if a_single_buffer else {}
    if transposed:
        w_spec = pl.BlockSpec((pl.Element(1), pl.Element(tn), pl.Element(k)),
                              lambda i, j: (layer, pl.multiple_of(col0 + j * tn, 8), 0))
    else:
        w_spec = pl.BlockSpec((1, k, tn), lambda i, j: (layer, 0, col0 // tn + j))
    in_specs = [pl.BlockSpec((tm, k), lambda i, j: (i, 0), **a_mode), w_spec]
    args = [a, w]
    if res is not None:
        in_specs.append(pl.BlockSpec((tm, tn), lambda i, j: (i, j)))
        args.append(res)
    return pl.pallas_call(
        functools.partial(_ar_kernel, transposed=transposed, relu2=relu2, residual=res is not None),
        grid=(m // tm, n_cols // tn),
        in_specs=in_specs,
        out_specs=pl.BlockSpec((tm, tn), lambda i, j: (i, j)),
        out_shape=jax.ShapeDtypeStruct((m, n_cols), out_dtype),
        compiler_params=_cparams("parallel", "arbitrary"),
        name=name,
    )(*args)


def _mm_res_kernel(a_ref, b_ref, r_ref, o_ref, acc_ref, *, nk):
    kk = pl.program_id(2)
    part = _dot(a_ref[...], b_ref[0])

    @pl.when(kk == 0)
    def _():
        acc_ref[...] = part

    @pl.when(kk > 0)
    def _():
        acc_ref[...] += part

    @pl.when(kk == nk - 1)
    def _():
        o_ref[...] = r_ref[...] + acc_ref[...]


def _matmul_residual(a, b, layer, res, *, tm=1024, tn=1024, tk=2048, name="matmul_res"):
    m, k = a.shape
    n = b.shape[2]
    nk = k // tk
    return pl.pallas_call(
        functools.partial(_mm_res_kernel, nk=nk),
        grid=(m // tm, n // tn, nk),
        in_specs=[pl.BlockSpec((tm, tk), lambda i, j, q: (i, q)),
                  pl.BlockSpec((1, tk, tn), lambda i, j, q: (layer, q, j)),
                  pl.BlockSpec((tm, tn), lambda i, j, q: (i, j))],
        out_specs=pl.BlockSpec((tm, tn), lambda i, j, q: (i, j)),
        out_shape=jax.ShapeDtypeStruct((m, n), F32),
        scratch_shapes=[pltpu.VMEM((tm, tn), F32)],
        compiler_params=_cparams("parallel", "arbitrary", "arbitrary"),
        name=name,
    )(a, b, res)


def _merge_kernel(oa_ref, ob_ref, oc_ref, wa_ref, wb_ref, wc_ref, ga_ref, gb_ref, gc_ref, o_ref):
    acc = _sigmoid(ga_ref[...]) * _cast_dot(oa_ref, wa_ref, False)
    acc += _sigmoid(gb_ref[...]) * _cast_dot(ob_ref, wb_ref, False)
    acc += _sigmoid(gc_ref[...]) * _cast_dot(oc_ref, wc_ref, False)
    o_ref[...] = acc.astype(o_ref.dtype)


def _branch_merge(o_a, o_b, o_c, w_branch, layer, zb, *, tm=1024, tn=256):
    m, kb = o_a.shape
    o_spec = pl.BlockSpec((tm, kb), lambda i, j: (i, 0))

    def w_spec(r):
        return pl.BlockSpec((1, kb, tn), lambda i, j: (layer, r, j))

    def g_spec(off):
        return pl.BlockSpec((tm, tn), lambda i, j: (i, off // tn + j))

    return pl.pallas_call(
        _merge_kernel,
        grid=(m // tm, D_MODEL // tn),
        in_specs=[o_spec, o_spec, o_spec, w_spec(0), w_spec(1), w_spec(2),
                  g_spec(ZB_MA), g_spec(ZB_MB), g_spec(ZB_MC)],
        out_specs=pl.BlockSpec((tm, tn), lambda i, j: (i, j)),
        out_shape=jax.ShapeDtypeStruct((m, D_MODEL), BF16),
        compiler_params=_cparams("parallel", "arbitrary"),
        name="branch_merge",
    )(o_a, o_b, o_c, w_branch, w_branch, w_branch, zb, zb, zb)


def _hgrn_chunk(q_in, f_in, v, g, lb, nw, st, tri):
    c = HG_CHUNK
    qc = _silu(q_in) * (HG_DIM ** -0.5)
    f = lb + (1.0 - lb) * _sigmoid(f_in)
    kk = 1.0 - f
    b = _dot(tri, jnp.log(f), precision=HIGHEST)
    o = _dot_nt((qc * jnp.exp(b)).astype(BF16), st.astype(BF16))
    b_last = b[c - 1:c, :]
    kdec = kk * jnp.exp(b_last - b)
    v16 = v.astype(BF16)
    st_new = st * jnp.exp(b_last) + _dot_tn(v16, kdec.astype(BF16))
    row = lax.broadcasted_iota(jnp.int32, (HG_SUB, HG_DIM), 0)
    col = lax.broadcasted_iota(jnp.int32, (HG_SUB, c), 1)
    parts = []
    for a in range(c // HG_SUB):
        lo = a * HG_SUB
        ba = b[lo:lo + HG_SUB, :]
        qa = qc[lo:lo + HG_SUB, :]
        acc = jnp.zeros((HG_SUB, HG_DIM), F32)
        if a > 0:
            bref = b[lo:lo + 1, :]
            qn = qa * jnp.exp(ba - bref)
            kn = kk * jnp.exp(jnp.minimum(bref - b, 0.0))
            att = _dot_nt(qn.astype(BF16), kn.astype(BF16))
            acc = _dot(jnp.where(col < lo, att, 0.0).astype(BF16), v16)
        for j in range(HG_SUB):
            jj = lo + j
            d = jnp.where(row >= j, ba - b[jj:jj + 1, :], NEG)
            w = jnp.sum(qa * kk[jj:jj + 1, :] * jnp.exp(d), axis=-1, keepdims=True)
            acc = acc + w * v[jj:jj + 1, :]
        parts.append(acc)
    o = o + jnp.concatenate(parts, axis=0)
    o = o * lax.rsqrt(jnp.mean(o * o, axis=-1, keepdims=True) + NORM_EPS) * nw * _silu(g)
    return o, st_new


def _hgrn_kernel(tab_ref, nw_ref, q_ref, f_ref, v_ref, g_ref, o_ref, st_ref, *, layer, rows):
    @pl.when(pl.program_id(2) == 0)
    def _():
        st_ref[...] = jnp.zeros_like(st_ref)

    tab = tab_ref[...]
    e = jnp.exp(tab - jnp.max(tab, axis=0, keepdims=True))
    p = e / jnp.sum(e, axis=0, keepdims=True)
    lb = jnp.sum(p[:layer + 1, :], axis=0, keepdims=True) - p[0:1, :]
    nw = nw_ref[...]
    r_i = lax.broadcasted_iota(jnp.int32, (HG_CHUNK, HG_CHUNK), 0)
    c_i = lax.broadcasted_iota(jnp.int32, (HG_CHUNK, HG_CHUNK), 1)
    tri = (r_i >= c_i).astype(F32)
    sts = [st_ref[hh] for hh in range(HG_HPB)]
    for ch in range(rows // HG_CHUNK):
        sl = slice(ch * HG_CHUNK, (ch + 1) * HG_CHUNK)
        outs = []
        for hh in range(HG_HPB):
            hs = slice(hh * HG_DIM, (hh + 1) * HG_DIM)
            o, sts[hh] = _hgrn_chunk(q_ref[sl, hs], f_ref[sl, hs], v_ref[sl, hs], g_ref[sl, hs], lb[:, hs], nw,
                                     sts[hh], tri)
            outs.append(o)
        o_ref[sl, :] = jnp.concatenate(outs, axis=1).astype(o_ref.dtype)
    for hh in range(HG_HPB):
        st_ref[hh] = sts[hh]


def _hgrn2(za, lb_table, norm_w, layer, batch, seq, rows=256):
    nrow = seq // rows
    width = HG_HPB * HG_DIM

    def col(off):
        return pl.BlockSpec((rows, width), lambda b, h, c: (b * nrow + c, off // width + h))

    return pl.pallas_call(
        functools.partial(_hgrn_kernel, layer=layer, rows=rows),
        grid=(batch, HG_HEADS // HG_HPB, nrow),
        in_specs=[pl.BlockSpec((DEPTH, width), lambda b, h, c: (0, h)),
                  pl.BlockSpec((1, HG_DIM), lambda b, h, c: (0, 0)),
                  col(OFF_AQ), col(OFF_AF), col(OFF_AI), col(OFF_AG)],
        out_specs=pl.BlockSpec((rows, width), lambda b, h, c: (b * nrow + c, h)),
        out_shape=jax.ShapeDtypeStruct((batch * seq, HG_WIDTH), BF16),
        scratch_shapes=[pltpu.VMEM((HG_HPB, HG_DIM, HG_DIM), F32)],
        compiler_params=_cparams("parallel", "parallel", "arbitrary"),
        name="hgrn2",
    )(lb_table, norm_w.reshape(1, HG_DIM), za, za, za, za)


def _ret_kernel(lg_ref, q_ref, k_ref, v_ref, g_ref, o_ref, s_ref, *, rows):
    @pl.when(pl.program_id(2) == 0)
    def _():
        s_ref[...] = jnp.zeros_like(s_ref)

    c = RET_CHUNK
    lg_v = lg_ref[0]
    lg_k = lg_v[:, :c]
    pos_r = lax.broadcasted_iota(jnp.int32, (c, RET_V), 0).astype(F32)
    query_decay = jnp.exp(lg_v * (pos_r + 1.0))
    rel = (lax.broadcasted_iota(jnp.int32, (c, c), 0) - lax.broadcasted_iota(jnp.int32, (c, c), 1)).astype(F32)
    intra_decay = jnp.where(rel >= 0, jnp.exp(lg_k * jnp.maximum(rel, 0.0)), 0.0)
    key_decay = jnp.exp(lg_k * (c - 1.0 - pos_r[:, :RET_QK]))
    chunk_decay = jnp.exp(lg_v * float(c))
    s = s_ref[...]
    for ch in range(rows // c):
        sl = slice(ch * c, (ch + 1) * c)
        q = q_ref[sl, :].astype(BF16)
        k = k_ref[sl, :] * (RET_QK ** -0.5)
        v = v_ref[sl, :].astype(BF16)
        inter = _dot(q, s.astype(BF16)) * query_decay
        scores = _dot_nt(q, k.astype(BF16)) * intra_decay
        o = inter + _dot(scores.astype(BF16), v)
        s = chunk_decay * s + _dot_tn((k * key_decay).astype(BF16), v)
        mu = jnp.mean(o, axis=-1, keepdims=True)
        oc = o - mu
        o = oc * lax.rsqrt(jnp.mean(oc * oc, axis=-1, keepdims=True) + NORM_EPS)
        o_ref[sl, :] = (o * _silu(g_ref[sl, :])).astype(o_ref.dtype)
    s_ref[...] = s


def _retention(zb, batch, seq, rows=1024):
    rows = min(rows, seq)
    nrow = seq // rows
    log_gamma = jnp.log(1.0 - jnp.exp2(-5.0 - jnp.arange(RET_HEADS, dtype=F32)))
    lg = jnp.broadcast_to(log_gamma[:, None, None], (RET_HEADS, 1, RET_V))
    return pl.pallas_call(
        functools.partial(_ret_kernel, rows=rows),
        grid=(batch, RET_HEADS, nrow),
        in_specs=[pl.BlockSpec((1, 1, RET_V), lambda b, h, c: (h, 0, 0)),
                  pl.BlockSpec((rows, RET_QK), lambda b, h, c: (b * nrow + c, ZB_CQ // RET_QK + h)),
                  pl.BlockSpec((rows, RET_QK), lambda b, h, c: (b * nrow + c, ZB_CK // RET_QK + h)),
                  pl.BlockSpec((rows, RET_V), lambda b, h, c: (b * nrow + c, ZB_CV // RET_V + h)),
                  pl.BlockSpec((rows, RET_V), lambda b, h, c: (b * nrow + c, ZB_CG // RET_V + h))],
        out_specs=pl.BlockSpec((rows, RET_V), lambda b, h, c: (b * nrow + c, h)),
        out_shape=jax.ShapeDtypeStruct((batch * seq, RET_V_WIDTH), BF16),
        scratch_shapes=[pltpu.VMEM((RET_QK, RET_V), F32)],
        compiler_params=_cparams("parallel", "parallel", "arbitrary"),
        name="retention",
    )(lg, zb, zb, zb, zb)


def _compress_kernel(kv_ref, pe_ref, w1_ref, w2_ref, o_ref, *, nc):
    d = NSA_DIM
    top = bot = None
    for l in range(CMP_STRIDE):
        x = kv_ref[pl.ds(l, nc, stride=CMP_STRIDE), :]
        pt = _dot((x + pe_ref[0, l:l + 1, :]).astype(BF16), w1_ref[0, l * d:(l + 1) * d, :])
        lb = CMP_STRIDE + l
        pb = _dot((x + pe_ref[0, lb:lb + 1, :]).astype(BF16), w1_ref[0, lb * d:(lb + 1) * d, :])
        top = pt if top is None else top + pt
        bot = pb if bot is None else bot + pb
    h = top + pltpu.roll(bot, nc - 1, 0)
    o_ref[0, 0] = _dot(_silu(h).astype(BF16), w2_ref[0])


def _compress(za, pe, w1, w2, batch, seq):
    nc = seq // CMP_STRIDE
    kv_blocks = NSA_KV_WIDTH // NSA_DIM
    return pl.pallas_call(
        functools.partial(_compress_kernel, nc=nc),
        grid=(2, batch, NSA_GROUPS),
        in_specs=[pl.BlockSpec((seq, NSA_DIM), lambda s, b, g: (b, OFF_BKC // NSA_DIM + s * kv_blocks + g)),
                  pl.BlockSpec((1, CMP_BLOCK, NSA_DIM), lambda s, b, g: (s, 0, 0)),
                  pl.BlockSpec((1, CMP_BLOCK * NSA_DIM, CMP_HIDDEN), lambda s, b, g: (s, 0, 0)),
                  pl.BlockSpec((1, CMP_HIDDEN, NSA_DIM), lambda s, b, g: (s, 0, 0))],
        out_specs=pl.BlockSpec((1, 1, nc, NSA_DIM), lambda s, b, g: (s, b * NSA_GROUPS + g, 0, 0)),
        out_shape=jax.ShapeDtypeStruct((2, batch * NSA_GROUPS, nc, NSA_DIM), F32),
        compiler_params=_cparams("parallel", "parallel", "parallel"),
        name="nsa_compress",
    )(za, pe, w1, w2)


def _slope_table():
    slopes = np.exp2(-8.0 * np.arange(1, NSA_HEADS + 1, dtype=np.float64) / NSA_HEADS).astype(np.float32)
    tab = np.zeros((NSA_GROUPS, 8, 512), np.float32)
    tab[:, :NSA_REP, :] = slopes.reshape(NSA_GROUPS, NSA_REP)[:, :, None]
    return jnp.asarray(tab)


def _cmp_body(slope_ref, wmap_ref, q_ref, kc_ref, vc_ref, gate_ref, o_ref, sel_ref, used_ref, score_ref,
              t0, ncols, nblk, *, tq, nc, nslc):
    kc = kc_ref[0, 0, :ncols, :]
    vc = vc_ref[0, 0, :ncols, :].astype(BF16)
    gates = _sigmoid(gate_ref[0, 0])
    t_i = t0 + lax.broadcasted_iota(jnp.int32, (tq, ncols), 0)
    n_i = lax.broadcasted_iota(jnp.int32, (tq, ncols), 1)
    dist = t_i - (n_i * CMP_STRIDE + CMP_BLOCK - 1)
    valid = (dist >= 0) & (n_i < nc - 1)
    distf = dist.astype(F32)
    imp = jnp.zeros((tq, ncols), F32)
    for r in range(NSA_REP):
        q = q_ref[:, r * NSA_DIM:(r + 1) * NSA_DIM]
        slope = slope_ref[0, r:r + 1, :ncols]
        s = _dot_nt(q, kc, precision=HIGHEST) * (NSA_DIM ** -0.5) - slope * distf
        s = jnp.where(valid, s, NEG)
        m = jnp.max(s, axis=-1, keepdims=True)
        e = jnp.where(valid, jnp.exp(s - m), 0.0)
        den = jnp.sum(e, axis=-1, keepdims=True)
        p = e / jnp.where(den > 0, den, 1.0)
        o_ref[:, r * NSA_DIM:(r + 1) * NSA_DIM] = gates[:, 3 * r:3 * r + 1] * _dot(p.astype(BF16), vc)
        imp = imp + p
    imp_t = _dot_nt(wmap_ref[:nblk, :ncols], imp, precision=HIGHEST)
    blk = lax.broadcasted_iota(jnp.int32, (nblk, tq), 0)
    cur = jnp.right_shift(t0 + lax.broadcasted_iota(jnp.int32, (nblk, tq), 1), SLC_SHIFT)
    forced = (blk == 0) | (blk == cur) | (blk == cur - 1)
    score = jnp.where(blk > cur, -jnp.inf, jnp.where(forced, jnp.inf, imp_t))
    score_ref[:nblk, :] = score
    sub = 8
    groups = [score_ref[g0:g0 + sub, :] for g0 in range(0, nblk, sub)]
    ranks = [jnp.zeros((sub, tq), F32) for _ in groups]
    row8 = lax.broadcasted_iota(jnp.int32, (sub, tq), 0)
    for s_i in range(nblk):
        other = jnp.broadcast_to(score_ref[s_i:s_i + 1, :], (sub, tq))
        for gi, sc in enumerate(groups):
            g0 = gi * sub
            if g0 > s_i:
                beats = other >= sc
            elif g0 + sub - 1 <= s_i:
                beats = other > sc
            else:
                beats = (other > sc) | ((other == sc) & (row8 > s_i - g0))
            ranks[gi] = ranks[gi] + jnp.where(beats, 1.0, 0.0)
    rank = jnp.concatenate(ranks, axis=0)
    sel = ((rank < min(SLC_TOPK, nslc)) & (blk <= cur)).astype(F32)
    if nblk < nslc:
        sel = jnp.concatenate([sel, jnp.zeros((nslc - nblk, tq), F32)], axis=0)
    sel_ref[0, 0] = sel
    for a in range(tq // ATT_T):
        used_ref[0, 0, 0, :, a:a + 1] = jnp.max(sel[:, a * ATT_T:(a + 1) * ATT_T], axis=1, keepdims=True)


def _cmp_kernel(*refs, tq, nc, nslc):
    t0 = pl.program_id(2) * tq
    early = t0 + tq <= (nslc // 2) * SLC_BLOCK

    @pl.when(early)
    def _():
        _cmp_body(*refs, t0, nc // 2, nslc // 2, tq=tq, nc=nc, nslc=nslc)

    @pl.when(jnp.logical_not(early))
    def _():
        _cmp_body(*refs, t0, nc, nslc, tq=tq, nc=nc, nslc=nslc)


def _cmp_attention(za, kvc, gates, batch, seq, tq=256):
    nc = seq // CMP_STRIDE
    nslc = seq // SLC_BLOCK
    nq = seq // tq
    c_start = np.arange(nc) * CMP_STRIDE
    s_start = np.arange(nslc) * SLC_BLOCK
    overlap = np.clip(np.minimum(c_start[:, None] + CMP_BLOCK, s_start[None, :] + SLC_BLOCK)
                      - np.maximum(c_start[:, None], s_start[None, :]), 0, None)
    wmap_t = (overlap.astype(np.float32) / CMP_STRIDE).T.copy()
    wmap_t[:, nc - 1] = 0.0
    sub = tq // ATT_T
    return pl.pallas_call(
        functools.partial(_cmp_kernel, tq=tq, nc=nc, nslc=nslc),
        grid=(batch, NSA_GROUPS, nq),
        in_specs=[pl.BlockSpec((1, 8, 512), lambda b, g, i: (g, 0, 0)),
                  pl.BlockSpec((nslc, nc), lambda b, g, i: (0, 0)),
                  pl.BlockSpec((tq, NSA_GW), lambda b, g, i: (b * nq + i, OFF_BQ // NSA_GW + g)),
                  pl.BlockSpec((1, 1, nc, NSA_DIM), lambda b, g, i: (0, b * NSA_GROUPS + g, 0, 0)),
                  pl.BlockSpec((1, 1, nc, NSA_DIM), lambda b, g, i: (1, b * NSA_GROUPS + g, 0, 0)),
                  pl.BlockSpec((1, 1, tq, 3 * NSA_REP), lambda b, g, i: (b, g, i, 0))],
        out_specs=[pl.BlockSpec((tq, NSA_GW), lambda b, g, i: (b * nq + i, g)),
                   pl.BlockSpec((1, 1, nslc, tq), lambda b, g, i: (b, g, 0, i)),
                   pl.BlockSpec((1, 1, 1, nslc, sub), lambda b, g, i: (b, g, i, 0, 0))],
        out_shape=[jax.ShapeDtypeStruct((batch * seq, NSA_WIDTH), F32),
                   jax.ShapeDtypeStruct((batch, NSA_GROUPS, nslc, seq), F32),
                   jax.ShapeDtypeStruct((batch, NSA_GROUPS, nq, nslc, sub), F32)],
        scratch_shapes=[pltpu.VMEM((nslc, tq), F32)],
        compiler_params=_cparams("parallel", "parallel", "arbitrary"),
        name="nsa_cmp_select",
    )(_slope_table(), jnp.asarray(wmap_t), za, kvc, kvc, gates)


ATT_SUPER_SHIFT = 3
ATT_SUPER = 1 << ATT_SUPER_SHIFT
ALIBI_FEATS = 6


def _alibi_query_features():
    slopes = np.exp2(-8.0 * np.arange(1, NSA_HEADS + 1, dtype=np.float64) / NSA_HEADS).astype(np.float32)

    def top_bits(x):
        return (x.view(np.uint32) & np.uint32(0xFFFF0000)).view(np.float32)

    s1 = top_bits(slopes)
    r1 = slopes - s1
    s2 = top_bits(r1)
    s3 = r1 - s2
    rows = np.stack([-64.0 * s1, -64.0 * s2, -64.0 * s3, -s1, -s2, -s3]).astype(np.float32)
    feat = np.zeros((NSA_GROUPS, NSA_DIM, NSA_REP, ATT_T), np.float32)
    feat[:, :ALIBI_FEATS] = rows.reshape(ALIBI_FEATS, NSA_GROUPS, NSA_REP).transpose(1, 0, 2)[:, :, :, None]
    return jnp.asarray(feat.reshape(NSA_GROUPS, NSA_DIM, NSA_GW), dtype=BF16)


def _attend(k_ref, v_ref, chunks, valids, qi, qt_aug, kf_static, kf_ind, m_ref, l_ref, acc_ref):
    t = ATT_T
    ks, vs, mbs = [], [], []
    for c, valid in zip(chunks, valids):
        k0 = pl.multiple_of(c * t, t)
        kfeat = (kf_static + (2 * (qi - c)).astype(F32) * kf_ind).astype(BF16)
        ks.append(jnp.concatenate([k_ref[pl.ds(k0, t), :].astype(BF16), kfeat], axis=1))
        vs.append(v_ref[pl.ds(k0, t), :].astype(BF16))
        mbs.append(jnp.where(valid, 0.0, NEG))
    st = _dot(jnp.concatenate(ks, axis=0), qt_aug)
    mbias = jnp.concatenate(mbs, axis=0)
    ps, alphas = [], []
    for r in range(NSA_REP):
        cs = slice(r * t, (r + 1) * t)
        s = st[:, cs] + mbias
        m_old = m_ref[:, cs]
        m_new = jnp.maximum(m_old, jnp.max(s, axis=0, keepdims=True))
        alphas.append(jnp.exp(m_old - m_new))
        p = jnp.exp(s - m_new)
        l_ref[:, cs] = alphas[-1] * l_ref[:, cs] + jnp.sum(p, axis=0, keepdims=True)
        m_ref[:, cs] = m_new
        ps.append(p.astype(BF16))
    acc_ref[...] = (jnp.concatenate(alphas, axis=1) * acc_ref[...]
                    + _dot_tn(jnp.concatenate(vs, axis=0), jnp.concatenate(ps, axis=1)))


def _slc_win_kernel(ids_ref, cnt_ref, qfeat_ref, q_ref, ks_ref, vs_ref, kw_ref, vw_ref, sel_ref, gate_ref,
                    ocmp_ref, o_ref, m_ref, l_ref, acc_ref, *, nq):
    t = ATT_T
    b, g, qi = pl.program_id(0), pl.program_id(1), pl.program_id(2)
    t0 = qi * t
    qt = jnp.concatenate([q_ref[:, r * NSA_DIM:(r + 1) * NSA_DIM].T for r in range(NSA_REP)], axis=1)
    qt_aug = jnp.concatenate([(qt * (NSA_DIM ** -0.5)).astype(BF16), qfeat_ref[0]], axis=0)
    key_i = lax.broadcasted_iota(jnp.int32, (t, t), 0)
    qry_i = lax.broadcasted_iota(jnp.int32, (t, t), 1)
    back = (t - 1) - key_i
    kf_ind = (qry_i < 3).astype(F32)
    kf_static = jnp.where(qry_i < 3, jnp.right_shift(back, SLC_SHIFT),
                          jnp.where(qry_i < ALIBI_FEATS, jnp.bitwise_and(back, SLC_BLOCK - 1), 0)).astype(F32)

    def reset():
        m_ref[...] = jnp.full_like(m_ref, NEG)
        l_ref[...] = jnp.zeros_like(l_ref)
        acc_ref[...] = jnp.zeros_like(acc_ref)

    reset()
    tile = (b * NSA_GROUPS + g) * nq + qi
    cnt = cnt_ref[tile]
    half = SLC_BLOCK
    never = jnp.int32(2 ** 30)

    def slc_step(base, width):
        chunks, valids = [], []
        for u in range(width):
            e = base + u
            c = ids_ref[tile * nq + jnp.maximum(jnp.minimum(e, cnt - 1), 0)]
            lo = jnp.broadcast_to(sel_ref[0, 0, pl.ds(2 * c, 1), :], (half, t))
            hi = jnp.broadcast_to(sel_ref[0, 0, pl.ds(2 * c + 1, 1), :], (half, t))
            picked = jnp.concatenate([lo, hi], axis=0) > 0.5
            dist = (t0 + qry_i) - (c * t + key_i)
            chunks.append(c)
            valids.append(picked & (dist >= jnp.where(e < cnt, 0, never)))
        _attend(ks_ref, vs_ref, chunks, valids, qi, qt_aug, kf_static, kf_ind, m_ref, l_ref, acc_ref)

    def slc_body(s, carry):
        slc_step(s * ATT_SUPER, ATT_SUPER)
        return carry

    n_full = jnp.right_shift(cnt, ATT_SUPER_SHIFT)
    rem = cnt - n_full * ATT_SUPER
    lax.fori_loop(0, n_full, slc_body, 0)

    @pl.when((rem > 0) & (rem <= ATT_SUPER // 2))
    def _():
        slc_step(n_full * ATT_SUPER, ATT_SUPER // 2)

    @pl.when(rem > ATT_SUPER // 2)
    def _():
        slc_step(n_full * ATT_SUPER, ATT_SUPER)

    o_slc = acc_ref[...] * (1.0 / l_ref[...])

    reset()
    chunks, valids = [], []
    for i in range(WIN_SIZE // t + 1):
        c = jnp.maximum(qi - i, 0)
        dist = (t0 + qry_i) - (c * t + key_i)
        chunks.append(c)
        valids.append((dist >= jnp.where(qi - i >= 0, 0, never)) & (dist < WIN_SIZE))
    _attend(kw_ref, vw_ref, chunks, valids, qi, qt_aug, kf_static, kf_ind, m_ref, l_ref, acc_ref)
    o_win = acc_ref[...] * (1.0 / l_ref[...])

    gates = _sigmoid(gate_ref[0, 0])
    for r in range(NSA_REP):
        cs = slice(r * t, (r + 1) * t)
        o_t = gates[3 * r + 1:3 * r + 2, :] * o_slc[:, cs] + gates[3 * r + 2:3 * r + 3, :] * o_win[:, cs]
        o_ref[:, cs] = (o_t.T + ocmp_ref[:, cs]).astype(o_ref.dtype)


def _slc_win_attention(ids, cnt, za, sel_t, gates_t, o_cmp, batch, seq):
    t = ATT_T
    nslc = seq // SLC_BLOCK
    nq = seq // t

    def kv(off):
        return pl.BlockSpec((seq, NSA_DIM), lambda b, g, i, *_: (b, off // NSA_DIM + g))

    grid_spec = pltpu.PrefetchScalarGridSpec(
        num_scalar_prefetch=2,
        grid=(batch, NSA_GROUPS, nq),
        in_specs=[pl.BlockSpec((1, NSA_DIM, NSA_GW), lambda b, g, i, *_: (g, 0, 0)),
                  pl.BlockSpec((t, NSA_GW), lambda b, g, i, *_: (b * nq + i, OFF_BQ // NSA_GW + g)),
                  kv(OFF_BKS), kv(OFF_BVS), kv(OFF_BKW), kv(OFF_BVW),
                  pl.BlockSpec((1, 1, nslc, t), lambda b, g, i, *_: (b, g, 0, i)),
                  pl.BlockSpec((1, 1, 3 * NSA_REP, t), lambda b, g, i, *_: (b, g, 0, i)),
                  pl.BlockSpec((t, NSA_GW), lambda b, g, i, *_: (b * nq + i, g))],
        out_specs=pl.BlockSpec((t, NSA_GW), lambda b, g, i, *_: (b * nq + i, g)),
        scratch_shapes=[pltpu.VMEM((1, NSA_GW), F32), pltpu.VMEM((1, NSA_GW), F32),
                        pltpu.VMEM((NSA_DIM, NSA_GW), F32)])
    return pl.pallas_call(
        functools.partial(_slc_win_kernel, nq=nq),
        grid_spec=grid_spec,
        out_shape=jax.ShapeDtypeStruct((batch * seq, NSA_WIDTH), BF16),
        compiler_params=_cparams("parallel", "parallel", "arbitrary"),
        name="nsa_slc_win",
    )(ids, cnt, _alibi_query_features(), za, za, za, za, za, sel_t, gates_t, o_cmp)


def _nsa(za, pe, w1, w2, batch, seq):
    nq = seq // ATT_T
    kvc = _compress(za, pe, w1, w2, batch, seq)
    gates = (za[:, OFF_BGATE:OFF_BGATE + GATE_COLS].reshape(batch, seq, NSA_GROUPS, 3 * NSA_REP)
             .transpose(0, 2, 1, 3))
    o_cmp, sel_t, used = _cmp_attention(za, kvc, gates, batch, seq)
    nslc = seq // SLC_BLOCK
    used = used.transpose(0, 1, 2, 4, 3).reshape(batch * NSA_GROUPS * nq, nslc // 2, 2)
    unused = (jnp.max(used, axis=-1) < 0.5).astype(jnp.int32)
    ids = jnp.argsort(unused, axis=-1, stable=True).astype(jnp.int32).reshape(-1)
    cnt = (nslc // 2 - jnp.sum(unused, axis=-1)).astype(jnp.int32)
    return _slc_win_attention(ids, cnt, za, sel_t, gates.transpose(0, 1, 3, 2), o_cmp, batch, seq)


def _layer(x, layer, norm1_w, w_in_t, lb_table, hgrn_norm_w, pe, w1, w2, w_branch, w_out, norm2_w, w_ff1, w_ff2_bf,
           batch, seq):
    h = _rmsnorm(x, norm1_w, BF16)
    wide = dict(tn=PROJ_TN, a_single_buffer=True)
    za = _ar_matmul(h, w_in_t, layer, F32, n_cols=ZA_WIDTH, transposed=True, name="proj_in_a", **wide)
    zb = _ar_matmul(h, w_in_t, layer, F32, n_cols=ZB_WIDTH, col0=OFF_BGATE + GATE_COLS, transposed=True,
                    name="proj_in_b", **wide)
    o_a = _hgrn2(za, lb_table, hgrn_norm_w, layer, batch, seq)
    o_b = _nsa(za, pe, w1, w2, batch, seq)
    o_c = _retention(zb, batch, seq)
    merged = _branch_merge(o_a, o_b, o_c, w_branch, layer, zb)
    x = _ar_matmul(merged, w_out, layer, F32, n_cols=D_MODEL, res=x, name="proj_out", **wide)
    h = _rmsnorm(x, norm2_w, BF16)
    u = _ar_matmul(h, w_ff1, layer, BF16, n_cols=D_FF, relu2=True, name="ffn_up", **wide)
    return _ar_matmul(u, w_ff2_bf, layer, F32, n_cols=D_MODEL, res=x, tm=512, name="ffn_down", **wide)


def kernel(x, norm1_w, w_in, hgrn_lb_table, hgrn_norm_w, cmp_pe_k, cmp_pe_v, cmp_w1_k, cmp_w1_v, cmp_w2_k,
           cmp_w2_v, w_branch, w_out, norm2_w, w_ff1, w_ff2, final_norm_w):
    batch, seq, d = x.shape
    xf = x.reshape(batch * seq, d)
    w_ff2_bf = w_ff2.astype(BF16)
    w_in_t = jnp.swapaxes(w_in, 1, 2)
    for l in range(DEPTH):
        pe = jnp.stack([cmp_pe_k[l], cmp_pe_v[l]])
        w1 = jnp.stack([cmp_w1_k[l], cmp_w1_v[l]]).astype(BF16)
        w2 = jnp.stack([cmp_w2_k[l], cmp_w2_v[l]]).astype(BF16)
        xf = _layer(xf, l, norm1_w[l], w_in_t, hgrn_lb_table, hgrn_norm_w[l], pe, w1, w2, w_branch, w_out,
                    norm2_w[l], w_ff1, w_ff2_bf, batch, seq)
    return _rmsnorm(xf, final_norm_w, F32).reshape(batch, seq, d)
```

```python
import functools

import numpy as np
import jax
import jax.numpy as jnp
from jax import lax
from jax.experimental import pallas as pl
from jax.experimental.pallas import tpu as pltpu

F32 = jnp.float32
BF16 = jnp.bfloat16
HIGHEST = lax.Precision.HIGHEST

D_MODEL = 4096
DEPTH = 2
NORM_EPS = 1e-6
LANE = 128

HG_HEADS = 16
HG_DIM = 128
HG_WIDTH = HG_HEADS * HG_DIM
HG_CHUNK = 64
HG_SUB = 16
HG_HPB = 2
NSA_HEADS = 16
NSA_GROUPS = 4
NSA_REP = NSA_HEADS // NSA_GROUPS
NSA_DIM = 128
NSA_WIDTH = NSA_HEADS * NSA_DIM
NSA_KV_WIDTH = NSA_GROUPS * NSA_DIM
NSA_GW = NSA_REP * NSA_DIM
CMP_BLOCK = 32
CMP_STRIDE = 16
CMP_HIDDEN = 256
SLC_BLOCK = 64
SLC_SHIFT = 6
SLC_TOPK = 16
WIN_SIZE = 512
ATT_T = 128
RET_HEADS = 8
RET_QK = 128
RET_V = 256
RET_QK_WIDTH = RET_HEADS * RET_QK
RET_V_WIDTH = RET_HEADS * RET_V
RET_CHUNK = 128
MIX_WIDTH = HG_WIDTH + NSA_WIDTH + RET_V_WIDTH
D_FF = 4 * D_MODEL

PROJ_TN = 512
GATE_COLS = NSA_HEADS * 3
OFF_AQ = 0
OFF_AF = OFF_AQ + HG_WIDTH
OFF_AI = OFF_AF + HG_WIDTH
OFF_AG = OFF_AI + HG_WIDTH
OFF_BQ = OFF_AG + HG_WIDTH
OFF_BKC = OFF_BQ + NSA_WIDTH
OFF_BVC = OFF_BKC + NSA_KV_WIDTH
OFF_BKS = OFF_BVC + NSA_KV_WIDTH
OFF_BVS = OFF_BKS + NSA_KV_WIDTH
OFF_BKW = OFF_BVS + NSA_KV_WIDTH
OFF_BVW = OFF_BKW + NSA_KV_WIDTH
OFF_BGATE = OFF_BVW + NSA_KV_WIDTH
ZA_WIDTH = OFF_BGATE + PROJ_TN
ZB_CQ = 0
ZB_CK = ZB_CQ + RET_QK_WIDTH
ZB_CV = ZB_CK + RET_QK_WIDTH
ZB_CG = ZB_CV + RET_V_WIDTH
ZB_MA = ZB_CG + RET_V_WIDTH
ZB_MB = ZB_MA + D_MODEL
ZB_MC = ZB_MB + D_MODEL
ZB_WIDTH = ZB_MC + D_MODEL

NEG = -1e30
VMEM_LIMIT = 56 * 1024 * 1024


def _cparams(*sem):
    return pltpu.CompilerParams(dimension_semantics=sem, vmem_limit_bytes=VMEM_LIMIT)


def _sigmoid(x):
    return 1.0 / (1.0 + jnp.exp(-x))


def _silu(x):
    return x * _sigmoid(x)


def _dot(a, b, precision=None):
    return jnp.dot(a, b, preferred_element_type=F32, precision=precision)


def _dot_nt(a, b, precision=None):
    return lax.dot_general(a, b, (((1,), (1,)), ((), ())), preferred_element_type=F32, precision=precision)


def _dot_tn(a, b):
    return lax.dot_general(a, b, (((0,), (0,)), ((), ())), preferred_element_type=F32)


def _rmsnorm_kernel(x_ref, w_ref, o_ref):
    x = x_ref[...]
    y = x * lax.rsqrt(jnp.mean(x * x, axis=-1, keepdims=True) + NORM_EPS)
    o_ref[...] = (y * w_ref[...]).astype(o_ref.dtype)


def _rmsnorm(x, w, out_dtype, tm=256):
    m, d = x.shape
    return pl.pallas_call(
        _rmsnorm_kernel,
        grid=(m // tm,),
        in_specs=[pl.BlockSpec((tm, d), lambda i: (i, 0)), pl.BlockSpec((1, d), lambda i: (0, 0))],
        out_specs=pl.BlockSpec((tm, d), lambda i: (i, 0)),
        out_shape=jax.ShapeDtypeStruct((m, d), out_dtype),
        compiler_params=_cparams("parallel"),
        name="rmsnorm",
    )(x, w.reshape(1, d))


CAST_K = 512


def _cast_dot(a_ref, w_ref, transposed):
    if w_ref.dtype == BF16:
        return _dot_nt(a_ref[...], w_ref[0]) if transposed else _dot(a_ref[...], w_ref[0])
    acc = None
    for k0 in range(0, a_ref.shape[1], CAST_K):
        sl = slice(k0, k0 + CAST_K)
        if transposed:
            part = _dot_nt(a_ref[:, sl], w_ref[0, :, sl].astype(BF16))
        else:
            part = _dot(a_ref[:, sl], w_ref[0, sl, :].astype(BF16))
        acc = part if acc is None else acc + part
    return acc


def _ar_kernel(*refs, transposed, relu2, residual, side):
    a_ref, w_ref = refs[0], refs[1]
    n_in = 2 + residual + side
    o_ref = refs[n_in]
    if side:
        refs[n_in + 1][...] = refs[n_in - 1][...].astype(BF16)
    acc = _cast_dot(a_ref, w_ref, transposed)
    if relu2:
        acc = jnp.maximum(acc, 0.0)
        acc = acc * acc
    if residual:
        acc = acc + refs[2][...]
    o_ref[...] = acc.astype(o_ref.dtype)


def _ar_matmul(a, w, layer, out_dtype, *, n_cols, col0=0, transposed=False, relu2=False, res=None, tm=2048,
               tn=256, a_single_buffer=False, side_cast=None, name="ar_matmul"):
    m, k = a.shape
    a_mode = dict(pipeline_mode=pl.Buffered(1)) if a_single_buffer else {}
    if transposed:
        w_spec = pl.BlockSpec((pl.Element(1), pl.Element(tn), pl.Element(k)),
                              lambda i, j: (layer, pl.multiple_of(col0 + j * tn, 8), 0))
    else:
        w_spec = pl.BlockSpec((1, k, tn), lambda i, j: (layer, 0, col0 // tn + j))
    in_specs = [pl.BlockSpec((tm, k), lambda i, j: (i, 0), **a_mode), w_spec]
    args = [a, w]
    if res is not None:
        in_specs.append(pl.BlockSpec((tm, tn), lambda i, j: (i, j)))
        args.append(res)
    nj = n_cols // tn
    out_specs = pl.BlockSpec((tm, tn), lambda i, j: (i, j))
    out_shape = jax.ShapeDtypeStruct((m, n_cols), out_dtype)
    if side_cast is not None:
        _, rows, cols = side_cast.shape
        per = rows // ((m // tm) * nj)
        in_specs.append(pl.BlockSpec((1, per, cols), lambda i, j: (layer, i * nj + j, 0)))
        args.append(side_cast)
        out_specs = [out_specs, pl.BlockSpec((1, per, cols), lambda i, j: (0, i * nj + j, 0))]
        out_shape = [out_shape, jax.ShapeDtypeStruct((1, rows, cols), BF16)]
    return pl.pallas_call(
        functools.partial(_ar_kernel, transposed=transposed, relu2=relu2, residual=res is not None,
                          side=side_cast is not None),
        grid=(m // tm, nj),
        in_specs=in_specs,
        out_specs=out_specs,
        out_shape=out_shape,
        compiler_params=_cparams("parallel", "arbitrary"),
        name=name,
    )(*args)


def _merge_kernel(oa_ref, ob_ref, oc_ref, wa_ref, wb_ref, wc_ref, ga_ref, gb_ref, gc_ref, o_ref):
    acc = _sigmoid(ga_ref[...]) * _cast_dot(oa_ref, wa_ref, False)
    acc += _sigmoid(gb_ref[...]) * _cast_dot(ob_ref, wb_ref, False)
    acc += _sigmoid(gc_ref[...]) * _cast_dot(oc_ref, wc_ref, False)
    o_ref[...] = acc.astype(o_ref.dtype)


def _branch_merge(o_a, o_b, o_c, w_branch, layer, zb, *, tm=1024, tn=256):
    m, kb = o_a.shape
    o_spec = pl.BlockSpec((tm, kb), lambda i, j: (i, 0))

    def w_spec(r):
        return pl.BlockSpec((1, kb, tn), lambda i, j: (layer, r, j))

    def g_spec(off):
        return pl.BlockSpec((tm, tn), lambda i, j: (i, off // tn + j))

    return pl.pallas_call(
        _merge_kernel,
        grid=(m // tm, D_MODEL // tn),
        in_specs=[o_spec, o_spec, o_spec, w_spec(0), w_spec(1), w_spec(2),
                  g_spec(ZB_MA), g_spec(ZB_MB), g_spec(ZB_MC)],
        out_specs=pl.BlockSpec((tm, tn), lambda i, j: (i, j)),
        out_shape=jax.ShapeDtypeStruct((m, D_MODEL), BF16),
        compiler_params=_cparams("parallel", "arbitrary"),
        name="branch_merge",
    )(o_a, o_b, o_c, w_branch, w_branch, w_branch, zb, zb, zb)


def _hgrn_chunk(q_in, f_in, v, g, lb, nw, st, tri):
    c = HG_CHUNK
    qc = _silu(q_in) * (HG_DIM ** -0.5)
    f = lb + (1.0 - lb) * _sigmoid(f_in)
    kk = 1.0 - f
    b = _dot(tri, jnp.log(f), precision=HIGHEST)
    o = _dot_nt((qc * jnp.exp(b)).astype(BF16), st.astype(BF16))
    b_last = b[c - 1:c, :]
    kdec = kk * jnp.exp(b_last - b)
    v16 = v.astype(BF16)
    st_new = st * jnp.exp(b_last) + _dot_tn(v16, kdec.astype(BF16))
    row = lax.broadcasted_iota(jnp.int32, (HG_SUB, HG_DIM), 0)
    col = lax.broadcasted_iota(jnp.int32, (HG_SUB, c), 1)
    parts = []
    for a in range(c // HG_SUB):
        lo = a * HG_SUB
        ba = b[lo:lo + HG_SUB, :]
        qa = qc[lo:lo + HG_SUB, :]
        acc = jnp.zeros((HG_SUB, HG_DIM), F32)
        if a > 0:
            bref = b[lo:lo + 1, :]
            qn = qa * jnp.exp(ba - bref)
            kn = kk * jnp.exp(jnp.minimum(bref - b, 0.0))
            att = _dot_nt(qn.astype(BF16), kn.astype(BF16))
            acc = _dot(jnp.where(col < lo, att, 0.0).astype(BF16), v16)
        for j in range(HG_SUB):
            jj = lo + j
            d = jnp.where(row >= j, ba - b[jj:jj + 1, :], NEG)
            w = jnp.sum(qa * kk[jj:jj + 1, :] * jnp.exp(d), axis=-1, keepdims=True)
            acc = acc + w * v[jj:jj + 1, :]
        parts.append(acc)
    o = o + jnp.concatenate(parts, axis=0)
    o = o * lax.rsqrt(jnp.mean(o * o, axis=-1, keepdims=True) + NORM_EPS) * nw * _silu(g)
    return o, st_new


def _hgrn_kernel(tab_ref, nw_ref, q_ref, f_ref, v_ref, g_ref, o_ref, st_ref, *, layer, rows):
    @pl.when(pl.program_id(2) == 0)
    def _():
        st_ref[...] = jnp.zeros_like(st_ref)

    tab = tab_ref[...]
    e = jnp.exp(tab - jnp.max(tab, axis=0, keepdims=True))
    p = e / jnp.sum(e, axis=0, keepdims=True)
    lb = jnp.sum(p[:layer + 1, :], axis=0, keepdims=True) - p[0:1, :]
    nw = nw_ref[...]
    r_i = lax.broadcasted_iota(jnp.int32, (HG_CHUNK, HG_CHUNK), 0)
    c_i = lax.broadcasted_iota(jnp.int32, (HG_CHUNK, HG_CHUNK), 1)
    tri = (r_i >= c_i).astype(F32)
    sts = [st_ref[hh] for hh in range(HG_HPB)]
    for ch in range(rows // HG_CHUNK):
        sl = slice(ch * HG_CHUNK, (ch + 1) * HG_CHUNK)
        outs = []
        for hh in range(HG_HPB):
            hs = slice(hh * HG_DIM, (hh + 1) * HG_DIM)
            o, sts[hh] = _hgrn_chunk(q_ref[sl, hs], f_ref[sl, hs], v_ref[sl, hs], g_ref[sl, hs], lb[:, hs], nw,
                                     sts[hh], tri)
            outs.append(o)
        o_ref[sl, :] = jnp.concatenate(outs, axis=1).astype(o_ref.dtype)
    for hh in range(HG_HPB):
        st_ref[hh] = sts[hh]


def _hgrn2(za, lb_table, norm_w, layer, batch, seq, rows=256):
    nrow = seq // rows
    width = HG_HPB * HG_DIM

    def col(off):
        return pl.BlockSpec((rows, width), lambda b, h, c: (b * nrow + c, off // width + h))

    return pl.pallas_call(
        functools.partial(_hgrn_kernel, layer=layer, rows=rows),
        grid=(batch, HG_HEADS // HG_HPB, nrow),
        in_specs=[pl.BlockSpec((DEPTH, width), lambda b, h, c: (0, h)),
                  pl.BlockSpec((1, HG_DIM), lambda b, h, c: (0, 0)),
                  col(OFF_AQ), col(OFF_AF), col(OFF_AI), col(OFF_AG)],
        out_specs=pl.BlockSpec((rows, width), lambda b, h, c: (b * nrow + c, h)),
        out_shape=jax.ShapeDtypeStruct((batch * seq, HG_WIDTH), BF16),
        scratch_shapes=[pltpu.VMEM((HG_HPB, HG_DIM, HG_DIM), F32)],
        compiler_params=_cparams("parallel", "parallel", "arbitrary"),
        name="hgrn2",
    )(lb_table, norm_w.reshape(1, HG_DIM), za, za, za, za)


def _ret_kernel(lg_ref, q_ref, k_ref, v_ref, g_ref, o_ref, s_ref, *, rows):
    @pl.when(pl.program_id(2) == 0)
    def _():
        s_ref[...] = jnp.zeros_like(s_ref)

    c = RET_CHUNK
    lg_v = lg_ref[0]
    lg_k = lg_v[:, :c]
    pos_r = lax.broadcasted_iota(jnp.int32, (c, RET_V), 0).astype(F32)
    query_decay = jnp.exp(lg_v * (pos_r + 1.0))
    rel = (lax.broadcasted_iota(jnp.int32, (c, c), 0) - lax.broadcasted_iota(jnp.int32, (c, c), 1)).astype(F32)
    intra_decay = jnp.where(rel >= 0, jnp.exp(lg_k * jnp.maximum(rel, 0.0)), 0.0)
    key_decay = jnp.exp(lg_k * (c - 1.0 - pos_r[:, :RET_QK]))
    chunk_decay = jnp.exp(lg_v * float(c))
    s = s_ref[...]
    for ch in range(rows // c):
        sl = slice(ch * c, (ch + 1) * c)
        q = q_ref[sl, :].astype(BF16)
        k = k_ref[sl, :] * (RET_QK ** -0.5)
        v = v_ref[sl, :].astype(BF16)
        inter = _dot(q, s.astype(BF16)) * query_decay
        scores = _dot_nt(q, k.astype(BF16)) * intra_decay
        o = inter + _dot(scores.astype(BF16), v)
        s = chunk_decay * s + _dot_tn((k * key_decay).astype(BF16), v)
        mu = jnp.mean(o, axis=-1, keepdims=True)
        oc = o - mu
        o = oc * lax.rsqrt(jnp.mean(oc * oc, axis=-1, keepdims=True) + NORM_EPS)
        o_ref[sl, :] = (o * _silu(g_ref[sl, :])).astype(o_ref.dtype)
    s_ref[...] = s


def _retention(zb, batch, seq, rows=1024):
    rows = min(rows, seq)
    nrow = seq // rows
    log_gamma = jnp.log(1.0 - jnp.exp2(-5.0 - jnp.arange(RET_HEADS, dtype=F32)))
    lg = jnp.broadcast_to(log_gamma[:, None, None], (RET_HEADS, 1, RET_V))
    return pl.pallas_call(
        functools.partial(_ret_kernel, rows=rows),
        grid=(batch, RET_HEADS, nrow),
        in_specs=[pl.BlockSpec((1, 1, RET_V), lambda b, h, c: (h, 0, 0)),
                  pl.BlockSpec((rows, RET_QK), lambda b, h, c: (b * nrow + c, ZB_CQ // RET_QK + h)),
                  pl.BlockSpec((rows, RET_QK), lambda b, h, c: (b * nrow + c, ZB_CK // RET_QK + h)),
                  pl.BlockSpec((rows, RET_V), lambda b, h, c: (b * nrow + c, ZB_CV // RET_V + h)),
                  pl.BlockSpec((rows, RET_V), lambda b, h, c: (b * nrow + c, ZB_CG // RET_V + h))],
        out_specs=pl.BlockSpec((rows, RET_V), lambda b, h, c: (b * nrow + c, h)),
        out_shape=jax.ShapeDtypeStruct((batch * seq, RET_V_WIDTH), BF16),
        scratch_shapes=[pltpu.VMEM((RET_QK, RET_V), F32)],
        compiler_params=_cparams("parallel", "parallel", "arbitrary"),
        name="retention",
    )(lg, zb, zb, zb, zb)


def _compress_kernel(kv_ref, pe_ref, w1_ref, w2_ref, o_ref, *, nc):
    d = NSA_DIM
    top = bot = None
    for l in range(CMP_STRIDE):
        x = kv_ref[pl.ds(l, nc, stride=CMP_STRIDE), :]
        pt = _dot((x + pe_ref[0, l:l + 1, :]).astype(BF16), w1_ref[0, l * d:(l + 1) * d, :])
        lb = CMP_STRIDE + l
        pb = _dot((x + pe_ref[0, lb:lb + 1, :]).astype(BF16), w1_ref[0, lb * d:(lb + 1) * d, :])
        top = pt if top is None else top + pt
        bot = pb if bot is None else bot + pb
    h = top + pltpu.roll(bot, nc - 1, 0)
    o_ref[0, 0] = _dot(_silu(h).astype(BF16), w2_ref[0])


def _compress(za, pe, w1, w2, batch, seq):
    nc = seq // CMP_STRIDE
    kv_blocks = NSA_KV_WIDTH // NSA_DIM
    return pl.pallas_call(
        functools.partial(_compress_kernel, nc=nc),
        grid=(2, batch, NSA_GROUPS),
        in_specs=[pl.BlockSpec((seq, NSA_DIM), lambda s, b, g: (b, OFF_BKC // NSA_DIM + s * kv_blocks + g)),
                  pl.BlockSpec((1, CMP_BLOCK, NSA_DIM), lambda s, b, g: (s, 0, 0)),
                  pl.BlockSpec((1, CMP_BLOCK * NSA_DIM, CMP_HIDDEN), lambda s, b, g: (s, 0, 0)),
                  pl.BlockSpec((1, CMP_HIDDEN, NSA_DIM), lambda s, b, g: (s, 0, 0))],
        out_specs=pl.BlockSpec((1, 1, nc, NSA_DIM), lambda s, b, g: (s, b * NSA_GROUPS + g, 0, 0)),
        out_shape=jax.ShapeDtypeStruct((2, batch * NSA_GROUPS, nc, NSA_DIM), F32),
        compiler_params=_cparams("parallel", "parallel", "parallel"),
        name="nsa_compress",
    )(za, pe, w1, w2)


def _slope_table():
    slopes = np.exp2(-8.0 * np.arange(1, NSA_HEADS + 1, dtype=np.float64) / NSA_HEADS).astype(np.float32)
    tab = np.zeros((NSA_GROUPS, 8, 512), np.float32)
    tab[:, :NSA_REP, :] = slopes.reshape(NSA_GROUPS, NSA_REP)[:, :, None]
    return jnp.asarray(tab)


def _cmp_body(slope_ref, wmap_ref, q_ref, kc_ref, vc_ref, gate_ref, o_ref, sel_ref, used_ref, score_ref,
              t0, ncols, nblk, *, tq, nc, nslc):
    kc = kc_ref[0, 0, :ncols, :]
    vc = vc_ref[0, 0, :ncols, :].astype(BF16)
    gates = _sigmoid(gate_ref[0, 0])
    t_i = t0 + lax.broadcasted_iota(jnp.int32, (tq, ncols), 0)
    n_i = lax.broadcasted_iota(jnp.int32, (tq, ncols), 1)
    dist = t_i - (n_i * CMP_STRIDE + CMP_BLOCK - 1)
    valid = (dist >= 0) & (n_i < nc - 1)
    distf = dist.astype(F32)
    imp = jnp.zeros((tq, ncols), F32)
    for r in range(NSA_REP):
        q = q_ref[:, r * NSA_DIM:(r + 1) * NSA_DIM]
        slope = slope_ref[0, r:r + 1, :ncols]
        s = _dot_nt(q, kc, precision=HIGHEST) * (NSA_DIM ** -0.5) - slope * distf
        s = jnp.where(valid, s, NEG)
        m = jnp.max(s, axis=-1, keepdims=True)
        e = jnp.where(valid, jnp.exp(s - m), 0.0)
        den = jnp.sum(e, axis=-1, keepdims=True)
        p = e / jnp.where(den > 0, den, 1.0)
        o_ref[:, r * NSA_DIM:(r + 1) * NSA_DIM] = gates[:, 3 * r:3 * r + 1] * _dot(p.astype(BF16), vc)
        imp = imp + p
    imp_t = _dot_nt(wmap_ref[:nblk, :ncols], imp, precision=HIGHEST)
    blk = lax.broadcasted_iota(jnp.int32, (nblk, tq), 0)
    cur = jnp.right_shift(t0 + lax.broadcasted_iota(jnp.int32, (nblk, tq), 1), SLC_SHIFT)
    forced = (blk == 0) | (blk == cur) | (blk == cur - 1)
    score = jnp.where(blk > cur, -jnp.inf, jnp.where(forced, jnp.inf, imp_t))
    score_ref[:nblk, :] = score
    sub = 8
    groups = [score_ref[g0:g0 + sub, :] for g0 in range(0, nblk, sub)]
    ranks = [jnp.zeros((sub, tq), F32) for _ in groups]
    row8 = lax.broadcasted_iota(jnp.int32, (sub, tq), 0)
    for s_i in range(nblk):
        other = jnp.broadcast_to(score_ref[s_i:s_i + 1, :], (sub, tq))
        for gi, sc in enumerate(groups):
            g0 = gi * sub
            if g0 > s_i:
                beats = other >= sc
            elif g0 + sub - 1 <= s_i:
                beats = other > sc
            else:
                beats = (other > sc) | ((other == sc) & (row8 > s_i - g0))
            ranks[gi] = ranks[gi] + jnp.where(beats, 1.0, 0.0)
    rank = jnp.concatenate(ranks, axis=0)
    sel = ((rank < min(SLC_TOPK, nslc)) & (blk <= cur)).astype(F32)
    if nblk < nslc:
        sel = jnp.concatenate([sel, jnp.zeros((nslc - nblk, tq), F32)], axis=0)
    sel_ref[0, 0] = sel
    for a in range(tq // ATT_T):
        used_ref[0, 0, 0, :, a:a + 1] = jnp.max(sel[:, a * ATT_T:(a + 1) * ATT_T], axis=1, keepdims=True)


def _cmp_kernel(*refs, tq, nc, nslc):
    t0 = pl.program_id(2) * tq
    early = t0 + tq <= (nslc // 2) * SLC_BLOCK

    @pl.when(early)
    def _():
        _cmp_body(*refs, t0, nc // 2, nslc // 2, tq=tq, nc=nc, nslc=nslc)

    @pl.when(jnp.logical_not(early))
    def _():
        _cmp_body(*refs, t0, nc, nslc, tq=tq, nc=nc, nslc=nslc)


def _cmp_attention(za, kvc, gates, batch, seq, tq=256):
    nc = seq // CMP_STRIDE
    nslc = seq // SLC_BLOCK
    nq = seq // tq
    c_start = np.arange(nc) * CMP_STRIDE
    s_start = np.arange(nslc) * SLC_BLOCK
    overlap = np.clip(np.minimum(c_start[:, None] + CMP_BLOCK, s_start[None, :] + SLC_BLOCK)
                      - np.maximum(c_start[:, None], s_start[None, :]), 0, None)
    wmap_t = (overlap.astype(np.float32) / CMP_STRIDE).T.copy()
    wmap_t[:, nc - 1] = 0.0
    sub = tq // ATT_T
    return pl.pallas_call(
        functools.partial(_cmp_kernel, tq=tq, nc=nc, nslc=nslc),
        grid=(batch, NSA_GROUPS, nq),
        in_specs=[pl.BlockSpec((1, 8, 512), lambda b, g, i: (g, 0, 0)),
                  pl.BlockSpec((nslc, nc), lambda b, g, i: (0, 0)),
                  pl.BlockSpec((tq, NSA_GW), lambda b, g, i: (b * nq + i, OFF_BQ // NSA_GW + g)),
                  pl.BlockSpec((1, 1, nc, NSA_DIM), lambda b, g, i: (0, b * NSA_GROUPS + g, 0, 0)),
                  pl.BlockSpec((1, 1, nc, NSA_DIM), lambda b, g, i: (1, b * NSA_GROUPS + g, 0, 0)),
                  pl.BlockSpec((1, 1, tq, 3 * NSA_REP), lambda b, g, i: (b, g, i, 0))],
        out_specs=[pl.BlockSpec((tq, NSA_GW), lambda b, g, i: (b * nq + i, g)),
                   pl.BlockSpec((1, 1, nslc, tq), lambda b, g, i: (b, g, 0, i)),
                   pl.BlockSpec((1, 1, 1, nslc, sub), lambda b, g, i: (b, g, i, 0, 0))],
        out_shape=[jax.ShapeDtypeStruct((batch * seq, NSA_WIDTH), F32),
                   jax.ShapeDtypeStruct((batch, NSA_GROUPS, nslc, seq), F32),
                   jax.ShapeDtypeStruct((batch, NSA_GROUPS, nq, nslc, sub), F32)],
        scratch_shapes=[pltpu.VMEM((nslc, tq), F32)],
        compiler_params=_cparams("parallel", "parallel", "arbitrary"),
        name="nsa_cmp_select",
    )(_slope_table(), jnp.asarray(wmap_t), za, kvc, kvc, gates)


ATT_SUPER_SHIFT = 3
ATT_SUPER = 1 << ATT_SUPER_SHIFT
ALIBI_FEATS = 6


def _alibi_query_features():
    slopes = np.exp2(-8.0 * np.arange(1, NSA_HEADS + 1, dtype=np.float64) / NSA_HEADS).astype(np.float32)

    def top_bits(x):
        return (x.view(np.uint32) & np.uint32(0xFFFF0000)).view(np.float32)

    s1 = top_bits(slopes)
    r1 = slopes - s1
    s2 = top_bits(r1)
    s3 = r1 - s2
    rows = np.stack([-64.0 * s1, -64.0 * s2, -64.0 * s3, -s1, -s2, -s3]).astype(np.float32)
    feat = np.zeros((NSA_GROUPS, NSA_DIM, NSA_REP, ATT_T), np.float32)
    feat[:, :ALIBI_FEATS] = rows.reshape(ALIBI_FEATS, NSA_GROUPS, NSA_REP).transpose(1, 0, 2)[:, :, :, None]
    return jnp.asarray(feat.reshape(NSA_GROUPS, NSA_DIM, NSA_GW), dtype=BF16)


def _attend(k_ref, v_ref, chunks, valids, qi, qt_aug, kf_static, kf_ind, m_ref, l_ref, acc_ref):
    t = ATT_T
    ks, vs, mbs = [], [], []
    for c, valid in zip(chunks, valids):
        k0 = pl.multiple_of(c * t, t)
        kfeat = (kf_static + (2 * (qi - c)).astype(F32) * kf_ind).astype(BF16)
        ks.append(jnp.concatenate([k_ref[pl.ds(k0, t), :].astype(BF16), kfeat], axis=1))
        vs.append(v_ref[pl.ds(k0, t), :].astype(BF16))
        mbs.append(jnp.where(valid, 0.0, NEG))
    st = _dot(jnp.concatenate(ks, axis=0), qt_aug)
    mbias = jnp.concatenate(mbs, axis=0)
    ps, alphas = [], []
    for r in range(NSA_REP):
        cs = slice(r * t, (r + 1) * t)
        s = st[:, cs] + mbias
        m_old = m_ref[:, cs]
        m_new = jnp.maximum(m_old, jnp.max(s, axis=0, keepdims=True))
        alphas.append(jnp.exp(m_old - m_new))
        p = jnp.exp(s - m_new)
        l_ref[:, cs] = alphas[-1] * l_ref[:, cs] + jnp.sum(p, axis=0, keepdims=True)
        m_ref[:, cs] = m_new
        ps.append(p.astype(BF16))
    acc_ref[...] = (jnp.concatenate(alphas, axis=1) * acc_ref[...]
                    + _dot_tn(jnp.concatenate(vs, axis=0), jnp.concatenate(ps, axis=1)))


def _slc_win_kernel(ids_ref, cnt_ref, qfeat_ref, q_ref, ks_ref, vs_ref, kw_ref, vw_ref, sel_ref, gate_ref,
                    ocmp_ref, o_ref, m_ref, l_ref, acc_ref, *, nq):
    t = ATT_T
    b, g, qi = pl.program_id(0), pl.program_id(1), pl.program_id(2)
    t0 = qi * t
    qt = jnp.concatenate([q_ref[:, r * NSA_DIM:(r + 1) * NSA_DIM].T for r in range(NSA_REP)], axis=1)
    qt_aug = jnp.concatenate([(qt * (NSA_DIM ** -0.5)).astype(BF16), qfeat_ref[0]], axis=0)
    key_i = lax.broadcasted_iota(jnp.int32, (t, t), 0)
    qry_i = lax.broadcasted_iota(jnp.int32, (t, t), 1)
    back = (t - 1) - key_i
    kf_ind = (qry_i < 3).astype(F32)
    kf_static = jnp.where(qry_i < 3, jnp.right_shift(back, SLC_SHIFT),
                          jnp.where(qry_i < ALIBI_FEATS, jnp.bitwise_and(back, SLC_BLOCK - 1), 0)).astype(F32)

    def reset():
        m_ref[...] = jnp.full_like(m_ref, NEG)
        l_ref[...] = jnp.zeros_like(l_ref)
        acc_ref[...] = jnp.zeros_like(acc_ref)

    reset()
    tile = (b * NSA_GROUPS + g) * nq + qi
    cnt = cnt_ref[tile]
    half = SLC_BLOCK
    never = jnp.int32(2 ** 30)

    def slc_step(base, width):
        chunks, valids = [], []
        for u in range(width):
            e = base + u
            c = ids_ref[tile * nq + jnp.maximum(jnp.minimum(e, cnt - 1), 0)]
            lo = jnp.broadcast_to(sel_ref[0, 0, pl.ds(2 * c, 1), :], (half, t))
            hi = jnp.broadcast_to(sel_ref[0, 0, pl.ds(2 * c + 1, 1), :], (half, t))
            picked = jnp.concatenate([lo, hi], axis=0) > 0.5
            dist = (t0 + qry_i) - (c * t + key_i)
            chunks.append(c)
            valids.append(picked & (dist >= jnp.where(e < cnt, 0, never)))
        _attend(ks_ref, vs_ref, chunks, valids, qi, qt_aug, kf_static, kf_ind, m_ref, l_ref, acc_ref)

    def slc_body(s, carry):
        slc_step(s * ATT_SUPER, ATT_SUPER)
        return carry

    n_full = jnp.right_shift(cnt, ATT_SUPER_SHIFT)
    rem = cnt - n_full * ATT_SUPER
    lax.fori_loop(0, n_full, slc_body, 0)

    @pl.when((rem > 0) & (rem <= ATT_SUPER // 2))
    def _():
        slc_step(n_full * ATT_SUPER, ATT_SUPER // 2)

    @pl.when(rem > ATT_SUPER // 2)
    def _():
        slc_step(n_full * ATT_SUPER, ATT_SUPER)

    o_slc = acc_ref[...] * (1.0 / l_ref[...])

    reset()
    chunks, valids = [], []
    for i in range(WIN_SIZE // t + 1):
        c = jnp.maximum(qi - i, 0)
        dist = (t0 + qry_i) - (c * t + key_i)
        chunks.append(c)
        valids.append((dist >= jnp.where(qi - i >= 0, 0, never)) & (dist < WIN_SIZE))
    _attend(kw_ref, vw_ref, chunks, valids, qi, qt_aug, kf_static, kf_ind, m_ref, l_ref, acc_ref)
    o_win = acc_ref[...] * (1.0 / l_ref[...])

    gates = _sigmoid(gate_ref[0, 0])
    for r in range(NSA_REP):
        cs = slice(r * t, (r + 1) * t)
        o_t = gates[3 * r + 1:3 * r + 2, :] * o_slc[:, cs] + gates[3 * r + 2:3 * r + 3, :] * o_win[:, cs]
        o_ref[:, cs] = (o_t.T + ocmp_ref[:, cs]).astype(o_ref.dtype)


def _slc_win_attention(ids, cnt, za, sel_t, gates_t, o_cmp, batch, seq):
    t = ATT_T
    nslc = seq // SLC_BLOCK
    nq = seq // t

    def kv(off):
        return pl.BlockSpec((seq, NSA_DIM), lambda b, g, i, *_: (b, off // NSA_DIM + g))

    grid_spec = pltpu.PrefetchScalarGridSpec(
        num_scalar_prefetch=2,
        grid=(batch, NSA_GROUPS, nq),
        in_specs=[pl.BlockSpec((1, NSA_DIM, NSA_GW), lambda b, g, i, *_: (g, 0, 0)),
                  pl.BlockSpec((t, NSA_GW), lambda b, g, i, *_: (b * nq + i, OFF_BQ // NSA_GW + g)),
                  kv(OFF_BKS), kv(OFF_BVS), kv(OFF_BKW), kv(OFF_BVW),
                  pl.BlockSpec((1, 1, nslc, t), lambda b, g, i, *_: (b, g, 0, i)),
                  pl.BlockSpec((1, 1, 3 * NSA_REP, t), lambda b, g, i, *_: (b, g, 0, i)),
                  pl.BlockSpec((t, NSA_GW), lambda b, g, i, *_: (b * nq + i, g))],
        out_specs=pl.BlockSpec((t, NSA_GW), lambda b, g, i, *_: (b * nq + i, g)),
        scratch_shapes=[pltpu.VMEM((1, NSA_GW), F32), pltpu.VMEM((1, NSA_GW), F32),
                        pltpu.VMEM((NSA_DIM, NSA_GW), F32)])
    return pl.pallas_call(
        functools.partial(_slc_win_kernel, nq=nq),
        grid_spec=grid_spec,
        out_shape=jax.ShapeDtypeStruct((batch * seq, NSA_WIDTH), BF16),
        compiler_params=_cparams("parallel", "parallel", "arbitrary"),
        name="nsa_slc_win",
    )(ids, cnt, _alibi_query_features(), za, za, za, za, za, sel_t, gates_t, o_cmp)


def _nsa(za, pe, w1, w2, batch, seq):
    nq = seq // ATT_T
    kvc = _compress(za, pe, w1, w2, batch, seq)
    gates = (za[:, OFF_BGATE:OFF_BGATE + GATE_COLS].reshape(batch, seq, NSA_GROUPS, 3 * NSA_REP)
             .transpose(0, 2, 1, 3))
    o_cmp, sel_t, used = _cmp_attention(za, kvc, gates, batch, seq)
    nslc = seq // SLC_BLOCK
    used = used.transpose(0, 1, 2, 4, 3).reshape(batch * NSA_GROUPS * nq, nslc // 2, 2)
    unused = (jnp.max(used, axis=-1) < 0.5).astype(jnp.int32)
    ids = jnp.argsort(unused, axis=-1, stable=True).astype(jnp.int32).reshape(-1)
    cnt = (nslc // 2 - jnp.sum(unused, axis=-1)).astype(jnp.int32)
    return _slc_win_attention(ids, cnt, za, sel_t, gates.transpose(0, 1, 3, 2), o_cmp, batch, seq)


def _layer(x, layer, norm1_w, w_in_t, lb_table, hgrn_norm_w, pe, w1, w2, w_branch, w_out, norm2_w, w_ff1, w_ff2,
           batch, seq):
    h = _rmsnorm(x, norm1_w, BF16)
    wide = dict(tn=PROJ_TN, a_single_buffer=True)
    za = _ar_matmul(h, w_in_t, layer, F32, n_cols=ZA_WIDTH, transposed=True, name="proj_in_a", **wide)
    zb = _ar_matmul(h, w_in_t, layer, F32, n_cols=ZB_WIDTH, col0=OFF_BGATE + GATE_COLS, transposed=True,
                    name="proj_in_b", **wide)
    o_a = _hgrn2(za, lb_table, hgrn_norm_w, layer, batch, seq)
    o_b = _nsa(za, pe, w1, w2, batch, seq)
    o_c = _retention(zb, batch, seq)
    merged = _branch_merge(o_a, o_b, o_c, w_branch, layer, zb)
    x = _ar_matmul(merged, w_out, layer, F32, n_cols=D_MODEL, res=x, name="proj_out", **wide)
    h = _rmsnorm(x, norm2_w, BF16)
    u, w_ff2_bf = _ar_matmul(h, w_ff1, layer, BF16, n_cols=D_FF, relu2=True, side_cast=w_ff2, name="ffn_up",
                             **wide)
    return _ar_matmul(u, w_ff2_bf, 0, F32, n_cols=D_MODEL, res=x, tm=512, name="ffn_down")


def kernel(x, norm1_w, w_in, hgrn_lb_table, hgrn_norm_w, cmp_pe_k, cmp_pe_v, cmp_w1_k, cmp_w1_v, cmp_w2_k,
           cmp_w2_v, w_branch, w_out, norm2_w, w_ff1, w_ff2, final_norm_w):
    batch, seq, d = x.shape
    xf = x.reshape(batch * seq, d)
    w_in_t = jnp.swapaxes(w_in, 1, 2)
    for l in range(DEPTH):
        pe = jnp.stack([cmp_pe_k[l], cmp_pe_v[l]])
        w1 = jnp.stack([cmp_w1_k[l], cmp_w1_v[l]]).astype(BF16)
        w2 = jnp.stack([cmp_w2_k[l], cmp_w2_v[l]]).astype(BF16)
        xf = _layer(xf, l, norm1_w[l], w_in_t, hgrn_lb_table, hgrn_norm_w[l], pe, w1, w2, w_branch, w_out,
                    norm2_w[l], w_ff1, w_ff2, batch, seq)
    return _rmsnorm(xf, final_norm_w, F32).reshape(batch, seq, d)
```

```python
import functools

import numpy as np
import jax
import jax.numpy as jnp
from jax import lax
from jax.experimental import pallas as pl
from jax.experimental.pallas import tpu as pltpu

F32 = jnp.float32
BF16 = jnp.bfloat16
HIGHEST = lax.Precision.HIGHEST

D_MODEL = 4096
DEPTH = 2
NORM_EPS = 1e-6
LANE = 128

HG_HEADS = 16
HG_DIM = 128
HG_WIDTH = HG_HEADS * HG_DIM
HG_CHUNK = 64
HG_SUB = 16
HG_HPB = 4
NSA_HEADS = 16
NSA_GROUPS = 4
NSA_REP = NSA_HEADS // NSA_GROUPS
NSA_DIM = 128
NSA_WIDTH = NSA_HEADS * NSA_DIM
NSA_KV_WIDTH = NSA_GROUPS * NSA_DIM
NSA_GW = NSA_REP * NSA_DIM
CMP_BLOCK = 32
CMP_STRIDE = 16
CMP_HIDDEN = 256
SLC_BLOCK = 64
SLC_SHIFT = 6
SLC_TOPK = 16
WIN_SIZE = 512
ATT_T = 128
RET_HEADS = 8
RET_QK = 128
RET_V = 256
RET_QK_WIDTH = RET_HEADS * RET_QK
RET_V_WIDTH = RET_HEADS * RET_V
RET_CHUNK = 128
MIX_WIDTH = HG_WIDTH + NSA_WIDTH + RET_V_WIDTH
D_FF = 4 * D_MODEL

PROJ_TN = 512
GATE_COLS = NSA_HEADS * 3
OFF_AQ = 0
OFF_AF = OFF_AQ + HG_WIDTH
OFF_AI = OFF_AF + HG_WIDTH
OFF_AG = OFF_AI + HG_WIDTH
OFF_BQ = OFF_AG + HG_WIDTH
OFF_BKC = OFF_BQ + NSA_WIDTH
OFF_BVC = OFF_BKC + NSA_KV_WIDTH
OFF_BKS = OFF_BVC + NSA_KV_WIDTH
OFF_BVS = OFF_BKS + NSA_KV_WIDTH
OFF_BKW = OFF_BVS + NSA_KV_WIDTH
OFF_BVW = OFF_BKW + NSA_KV_WIDTH
OFF_BGATE = OFF_BVW + NSA_KV_WIDTH
ZA_WIDTH = OFF_BGATE + PROJ_TN
ZB_CQ = 0
ZB_CK = ZB_CQ + RET_QK_WIDTH
ZB_CV = ZB_CK + RET_QK_WIDTH
ZB_CG = ZB_CV + RET_V_WIDTH
ZB_MA = ZB_CG + RET_V_WIDTH
ZB_MB = ZB_MA + D_MODEL
ZB_MC = ZB_MB + D_MODEL
ZB_WIDTH = ZB_MC + D_MODEL

NEG = -1e30
VMEM_LIMIT = 56 * 1024 * 1024


def _cparams(*sem):
    return pltpu.CompilerParams(dimension_semantics=sem, vmem_limit_bytes=VMEM_LIMIT)


def _sigmoid(x):
    return 1.0 / (1.0 + jnp.exp(-x))


def _silu(x):
    return x * _sigmoid(x)


def _dot(a, b, precision=None):
    return jnp.dot(a, b, preferred_element_type=F32, precision=precision)


def _dot_nt(a, b, precision=None):
    return lax.dot_general(a, b, (((1,), (1,)), ((), ())), preferred_element_type=F32, precision=precision)


def _dot_tn(a, b):
    return lax.dot_general(a, b, (((0,), (0,)), ((), ())), preferred_element_type=F32)


def _rmsnorm_kernel(x_ref, w_ref, o_ref):
    x = x_ref[...]
    y = x * lax.rsqrt(jnp.mean(x * x, axis=-1, keepdims=True) + NORM_EPS)
    o_ref[...] = (y * w_ref[...]).astype(o_ref.dtype)


def _rmsnorm(x, w, out_dtype, tm=256):
    m, d = x.shape
    return pl.pallas_call(
        _rmsnorm_kernel,
        grid=(m // tm,),
        in_specs=[pl.BlockSpec((tm, d), lambda i: (i, 0)), pl.BlockSpec((1, d), lambda i: (0, 0))],
        out_specs=pl.BlockSpec((tm, d), lambda i: (i, 0)),
        out_shape=jax.ShapeDtypeStruct((m, d), out_dtype),
        compiler_params=_cparams("parallel"),
        name="rmsnorm",
    )(x, w.reshape(1, d))


CAST_K = 512


def _cast_dot(a_ref, w_ref, transposed):
    if w_ref.dtype == BF16:
        return _dot_nt(a_ref[...], w_ref[0]) if transposed else _dot(a_ref[...], w_ref[0])
    acc = None
    for k0 in range(0, a_ref.shape[1], CAST_K):
        sl = slice(k0, k0 + CAST_K)
        if transposed:
            part = _dot_nt(a_ref[:, sl], w_ref[0, :, sl].astype(BF16))
        else:
            part = _dot(a_ref[:, sl], w_ref[0, sl, :].astype(BF16))
        acc = part if acc is None else acc + part
    return acc


def _ar_kernel(*refs, transposed, relu2, residual, side):
    a_ref, w_ref = refs[0], refs[1]
    n_in = 2 + residual + side
    o_ref = refs[n_in]
    if side:
        refs[n_in + 1][...] = refs[n_in - 1][...].astype(BF16)
    acc = _cast_dot(a_ref, w_ref, transposed)
    if relu2:
        acc = jnp.maximum(acc, 0.0)
        acc = acc * acc
    if residual:
        acc = acc + refs[2][...]
    o_ref[...] = acc.astype(o_ref.dtype)


def _ar_matmul(a, w, layer, out_dtype, *, n_cols, col0=0, transposed=False, relu2=False, res=None, tm=2048,
               tn=256, a_single_buffer=False, side_cast=None, name="ar_matmul"):
    m, k = a.shape
    a_mode = dict(pipeline_mode=pl.Buffered(1)) if a_single_buffer else {}
    if transposed:
        w_spec = pl.BlockSpec((pl.Element(1), pl.Element(tn), pl.Element(k)),
                              lambda i, j: (layer, pl.multiple_of(col0 + j * tn, 8), 0))
    else:
        w_spec = pl.BlockSpec((1, k, tn), lambda i, j: (layer, 0, col0 // tn + j))
    in_specs = [pl.BlockSpec((tm, k), lambda i, j: (i, 0), **a_mode), w_spec]
    args = [a, w]
    if res is not None:
        in_specs.append(pl.BlockSpec((tm, tn), lambda i, j: (i, j)))
        args.append(res)
    nj = n_cols // tn
    out_specs = pl.BlockSpec((tm, tn), lambda i, j: (i, j))
    out_shape = jax.ShapeDtypeStruct((m, n_cols), out_dtype)
    if side_cast is not None:
        _, rows, cols = side_cast.shape
        per = rows // ((m // tm) * nj)
        in_specs.append(pl.BlockSpec((1, per, cols), lambda i, j: (layer, i * nj + j, 0)))
        args.append(side_cast)
        out_specs = [out_specs, pl.BlockSpec((1, per, cols), lambda i, j: (0, i * nj + j, 0))]
        out_shape = [out_shape, jax.ShapeDtypeStruct((1, rows, cols), BF16)]
    return pl.pallas_call(
        functools.partial(_ar_kernel, transposed=transposed, relu2=relu2, residual=res is not None,
                          side=side_cast is not None),
        grid=(m // tm, nj),
        in_specs=in_specs,
        out_specs=out_specs,
        out_shape=out_shape,
        compiler_params=_cparams("parallel", "arbitrary"),
        name=name,
    )(*args)


def _merge_kernel(oa_ref, ob_ref, oc_ref, wa_ref, wb_ref, wc_ref, ga_ref, gb_ref, gc_ref, o_ref):
    acc = _sigmoid(ga_ref[...]) * _cast_dot(oa_ref, wa_ref, False)
    acc += _sigmoid(gb_ref[...]) * _cast_dot(ob_ref, wb_ref, False)
    acc += _sigmoid(gc_ref[...]) * _cast_dot(oc_ref, wc_ref, False)
    o_ref[...] = acc.astype(o_ref.dtype)


def _branch_merge(o_a, o_b, o_c, w_branch, layer, zb, *, tm=1024, tn=256):
    m, kb = o_a.shape
    o_spec = pl.BlockSpec((tm, kb), lambda i, j: (i, 0))

    def w_spec(r):
        return pl.BlockSpec((1, kb, tn), lambda i, j: (layer, r, j))

    def g_spec(off):
        return pl.BlockSpec((tm, tn), lambda i, j: (i, off // tn + j))

    return pl.pallas_call(
        _merge_kernel,
        grid=(m // tm, D_MODEL // tn),
        in_specs=[o_spec, o_spec, o_spec, w_spec(0), w_spec(1), w_spec(2),
                  g_spec(ZB_MA), g_spec(ZB_MB), g_spec(ZB_MC)],
        out_specs=pl.BlockSpec((tm, tn), lambda i, j: (i, j)),
        out_shape=jax.ShapeDtypeStruct((m, D_MODEL), BF16),
        compiler_params=_cparams("parallel", "arbitrary"),
        name="branch_merge",
    )(o_a, o_b, o_c, w_branch, w_branch, w_branch, zb, zb, zb)


def _hgrn_chunk(qc, kk, b, v, g, nw, st):
    c = HG_CHUNK
    o = _dot_nt((qc * jnp.exp(b)).astype(BF16), st.astype(BF16))
    b_last = b[c - 1:c, :]
    kdec = kk * jnp.exp(b_last - b)
    v16 = v.astype(BF16)
    st_new = st * jnp.exp(b_last) + _dot_tn(v16, kdec.astype(BF16))
    row = lax.broadcasted_iota(jnp.int32, (HG_SUB, HG_DIM), 0)
    col = lax.broadcasted_iota(jnp.int32, (HG_SUB, c), 1)
    atts = [jnp.zeros((HG_SUB, c), F32)]
    for a in range(1, c // HG_SUB):
        lo = a * HG_SUB
        bref = b[lo:lo + 1, :]
        qn = qc[lo:lo + HG_SUB, :] * jnp.exp(b[lo:lo + HG_SUB, :] - bref)
        kn = kk * jnp.exp(jnp.minimum(bref - b, 0.0))
        atts.append(jnp.where(col < lo, _dot_nt(qn.astype(BF16), kn.astype(BF16)), 0.0))
    o = o + _dot(jnp.concatenate(atts, axis=0).astype(BF16), v16)
    parts = []
    for a in range(c // HG_SUB):
        lo = a * HG_SUB
        ba = b[lo:lo + HG_SUB, :]
        qa = qc[lo:lo + HG_SUB, :]
        acc = jnp.zeros((HG_SUB, HG_DIM), F32)
        for j in range(HG_SUB):
            jj = lo + j
            d = jnp.where(row >= j, ba - b[jj:jj + 1, :], NEG)
            w = jnp.sum(qa * kk[jj:jj + 1, :] * jnp.exp(d), axis=-1, keepdims=True)
            acc = acc + w * v[jj:jj + 1, :]
        parts.append(acc)
    o = o + jnp.concatenate(parts, axis=0)
    o = o * lax.rsqrt(jnp.mean(o * o, axis=-1, keepdims=True) + NORM_EPS) * nw * _silu(g)
    return o, st_new


def _hgrn_kernel(tab_ref, nw_ref, q_ref, f_ref, v_ref, g_ref, o_ref, st_ref, *, layer, rows):
    @pl.when(pl.program_id(2) == 0)
    def _():
        st_ref[...] = jnp.zeros_like(st_ref)

    tab = tab_ref[...]
    e = jnp.exp(tab - jnp.max(tab, axis=0, keepdims=True))
    p = e / jnp.sum(e, axis=0, keepdims=True)
    lb = jnp.sum(p[:layer + 1, :], axis=0, keepdims=True) - p[0:1, :]
    nw = nw_ref[...]
    r_i = lax.broadcasted_iota(jnp.int32, (HG_CHUNK, HG_CHUNK), 0)
    c_i = lax.broadcasted_iota(jnp.int32, (HG_CHUNK, HG_CHUNK), 1)
    tri = (r_i >= c_i).astype(F32)
    sts = [st_ref[hh] for hh in range(HG_HPB)]
    for ch in range(rows // HG_CHUNK):
        sl = slice(ch * HG_CHUNK, (ch + 1) * HG_CHUNK)
        qc = _silu(q_ref[sl, :]) * (HG_DIM ** -0.5)
        f = lb + (1.0 - lb) * _sigmoid(f_ref[sl, :])
        kk = 1.0 - f
        b = _dot(tri, jnp.log(f), precision=HIGHEST)
        outs = []
        for hh in range(HG_HPB):
            hs = slice(hh * HG_DIM, (hh + 1) * HG_DIM)
            o, sts[hh] = _hgrn_chunk(qc[:, hs], kk[:, hs], b[:, hs], v_ref[sl, hs], g_ref[sl, hs], nw, sts[hh])
            outs.append(o)
        o_ref[sl, :] = jnp.concatenate(outs, axis=1).astype(o_ref.dtype)
    for hh in range(HG_HPB):
        st_ref[hh] = sts[hh]


def _hgrn2(za, lb_table, norm_w, layer, batch, seq, rows=256):
    nrow = seq // rows
    width = HG_HPB * HG_DIM

    def col(off):
        return pl.BlockSpec((rows, width), lambda b, h, c: (b * nrow + c, off // width + h))

    return pl.pallas_call(
        functools.partial(_hgrn_kernel, layer=layer, rows=rows),
        grid=(batch, HG_HEADS // HG_HPB, nrow),
        in_specs=[pl.BlockSpec((DEPTH, width), lambda b, h, c: (0, h)),
                  pl.BlockSpec((1, HG_DIM), lambda b, h, c: (0, 0)),
                  col(OFF_AQ), col(OFF_AF), col(OFF_AI), col(OFF_AG)],
        out_specs=pl.BlockSpec((rows, width), lambda b, h, c: (b * nrow + c, h)),
        out_shape=jax.ShapeDtypeStruct((batch * seq, HG_WIDTH), BF16),
        scratch_shapes=[pltpu.VMEM((HG_HPB, HG_DIM, HG_DIM), F32)],
        compiler_params=_cparams("parallel", "parallel", "arbitrary"),
        name="hgrn2",
    )(lb_table, norm_w.reshape(1, HG_DIM), za, za, za, za)


def _ret_kernel(lg_ref, q_ref, k_ref, v_ref, g_ref, o_ref, s_ref, *, rows):
    @pl.when(pl.program_id(2) == 0)
    def _():
        s_ref[...] = jnp.zeros_like(s_ref)

    c = RET_CHUNK
    lg_v = lg_ref[0]
    lg_k = lg_v[:, :c]
    pos_r = lax.broadcasted_iota(jnp.int32, (c, RET_V), 0).astype(F32)
    query_decay = jnp.exp(lg_v * (pos_r + 1.0))
    rel = (lax.broadcasted_iota(jnp.int32, (c, c), 0) - lax.broadcasted_iota(jnp.int32, (c, c), 1)).astype(F32)
    intra_decay = jnp.where(rel >= 0, jnp.exp(lg_k * jnp.maximum(rel, 0.0)), 0.0)
    key_decay = jnp.exp(lg_k * (c - 1.0 - pos_r[:, :RET_QK]))
    chunk_decay = jnp.exp(lg_v * float(c))
    s = s_ref[...]
    for ch in range(rows // c):
        sl = slice(ch * c, (ch + 1) * c)
        q = q_ref[sl, :].astype(BF16)
        k = k_ref[sl, :] * (RET_QK ** -0.5)
        v = v_ref[sl, :].astype(BF16)
        inter = _dot(q, s.astype(BF16)) * query_decay
        scores = _dot_nt(q, k.astype(BF16)) * intra_decay
        o = inter + _dot(scores.astype(BF16), v)
        s = chunk_decay * s + _dot_tn((k * key_decay).astype(BF16), v)
        mu = jnp.mean(o, axis=-1, keepdims=True)
        oc = o - mu
        o = oc * lax.rsqrt(jnp.mean(oc * oc, axis=-1, keepdims=True) + NORM_EPS)
        o_ref[sl, :] = (o * _silu(g_ref[sl, :])).astype(o_ref.dtype)
    s_ref[...] = s


def _retention(zb, batch, seq, rows=1024):
    rows = min(rows, seq)
    nrow = seq // rows
    log_gamma = jnp.log(1.0 - jnp.exp2(-5.0 - jnp.arange(RET_HEADS, dtype=F32)))
    lg = jnp.broadcast_to(log_gamma[:, None, None], (RET_HEADS, 1, RET_V))
    return pl.pallas_call(
        functools.partial(_ret_kernel, rows=rows),
        grid=(batch, RET_HEADS, nrow),
        in_specs=[pl.BlockSpec((1, 1, RET_V), lambda b, h, c: (h, 0, 0)),
                  pl.BlockSpec((rows, RET_QK), lambda b, h, c: (b * nrow + c, ZB_CQ // RET_QK + h)),
                  pl.BlockSpec((rows, RET_QK), lambda b, h, c: (b * nrow + c, ZB_CK // RET_QK + h)),
                  pl.BlockSpec((rows, RET_V), lambda b, h, c: (b * nrow + c, ZB_CV // RET_V + h)),
                  pl.BlockSpec((rows, RET_V), lambda b, h, c: (b * nrow + c, ZB_CG // RET_V + h))],
        out_specs=pl.BlockSpec((rows, RET_V), lambda b, h, c: (b * nrow + c, h)),
        out_shape=jax.ShapeDtypeStruct((batch * seq, RET_V_WIDTH), BF16),
        scratch_shapes=[pltpu.VMEM((RET_QK, RET_V), F32)],
        compiler_params=_cparams("parallel", "parallel", "arbitrary"),
        name="retention",
    )(lg, zb, zb, zb, zb)


def _compress_kernel(kv_ref, pe_ref, w1_ref, w2_ref, o_ref, *, nc):
    d = NSA_DIM
    top = bot = None
    for l in range(CMP_STRIDE):
        x = kv_ref[pl.ds(l, nc, stride=CMP_STRIDE), :]
        pt = _dot((x + pe_ref[0, l:l + 1, :]).astype(BF16), w1_ref[0, l * d:(l + 1) * d, :])
        lb = CMP_STRIDE + l
        pb = _dot((x + pe_ref[0, lb:lb + 1, :]).astype(BF16), w1_ref[0, lb * d:(lb + 1) * d, :])
        top = pt if top is None else top + pt
        bot = pb if bot is None else bot + pb
    h = top + pltpu.roll(bot, nc - 1, 0)
    o_ref[0, 0] = _dot(_silu(h).astype(BF16), w2_ref[0])


def _compress(za, pe, w1, w2, batch, seq):
    nc = seq // CMP_STRIDE
    kv_blocks = NSA_KV_WIDTH // NSA_DIM
    return pl.pallas_call(
        functools.partial(_compress_kernel, nc=nc),
        grid=(2, batch, NSA_GROUPS),
        in_specs=[pl.BlockSpec((seq, NSA_DIM), lambda s, b, g: (b, OFF_BKC // NSA_DIM + s * kv_blocks + g)),
                  pl.BlockSpec((1, CMP_BLOCK, NSA_DIM), lambda s, b, g: (s, 0, 0)),
                  pl.BlockSpec((1, CMP_BLOCK * NSA_DIM, CMP_HIDDEN), lambda s, b, g: (s, 0, 0)),
                  pl.BlockSpec((1, CMP_HIDDEN, NSA_DIM), lambda s, b, g: (s, 0, 0))],
        out_specs=pl.BlockSpec((1, 1, nc, NSA_DIM), lambda s, b, g: (s, b * NSA_GROUPS + g, 0, 0)),
        out_shape=jax.ShapeDtypeStruct((2, batch * NSA_GROUPS, nc, NSA_DIM), F32),
        compiler_params=_cparams("parallel", "parallel", "parallel"),
        name="nsa_compress",
    )(za, pe, w1, w2)


def _slope_table():
    slopes = np.exp2(-8.0 * np.arange(1, NSA_HEADS + 1, dtype=np.float64) / NSA_HEADS).astype(np.float32)
    tab = np.zeros((NSA_GROUPS, 8, 512), np.float32)
    tab[:, :NSA_REP, :] = slopes.reshape(NSA_GROUPS, NSA_REP)[:, :, None]
    return jnp.asarray(tab)


def _cmp_body(slope_ref, wmap_ref, q_ref, kc_ref, vc_ref, gate_ref, o_ref, sel_ref, used_ref, score_ref,
              t0, ncols, nblk, *, tq, nc, nslc):
    kc = kc_ref[0, 0, :ncols, :]
    vc = vc_ref[0, 0, :ncols, :].astype(BF16)
    gates = _sigmoid(gate_ref[0, 0])
    t_i = t0 + lax.broadcasted_iota(jnp.int32, (tq, ncols), 0)
    n_i = lax.broadcasted_iota(jnp.int32, (tq, ncols), 1)
    dist = t_i - (n_i * CMP_STRIDE + CMP_BLOCK - 1)
    valid = (dist >= 0) & (n_i < nc - 1)
    distf = dist.astype(F32)
    imp = jnp.zeros((tq, ncols), F32)
    for r in range(NSA_REP):
        q = q_ref[:, r * NSA_DIM:(r + 1) * NSA_DIM]
        slope = slope_ref[0, r:r + 1, :ncols]
        s = _dot_nt(q, kc, precision=HIGHEST) * (NSA_DIM ** -0.5) - slope * distf
        s = jnp.where(valid, s, NEG)
        m = jnp.max(s, axis=-1, keepdims=True)
        e = jnp.where(valid, jnp.exp(s - m), 0.0)
        den = jnp.sum(e, axis=-1, keepdims=True)
        p = e / jnp.where(den > 0, den, 1.0)
        o_ref[:, r * NSA_DIM:(r + 1) * NSA_DIM] = gates[:, 3 * r:3 * r + 1] * _dot(p.astype(BF16), vc)
        imp = imp + p
    imp_t = _dot_nt(wmap_ref[:nblk, :ncols], imp, precision=HIGHEST)
    blk = lax.broadcasted_iota(jnp.int32, (nblk, tq), 0)
    cur = jnp.right_shift(t0 + lax.broadcasted_iota(jnp.int32, (nblk, tq), 1), SLC_SHIFT)
    forced = (blk == 0) | (blk == cur) | (blk == cur - 1)
    score = jnp.where(blk > cur, -jnp.inf, jnp.where(forced, jnp.inf, imp_t))
    score_ref[:nblk, :] = score
    sub = 8
    groups = [score_ref[g0:g0 + sub, :] for g0 in range(0, nblk, sub)]
    ranks = [jnp.zeros((sub, tq), F32) for _ in groups]
    row8 = lax.broadcasted_iota(jnp.int32, (sub, tq), 0)
    for s_i in range(nblk):
        other = jnp.broadcast_to(score_ref[s_i:s_i + 1, :], (sub, tq))
        for gi, sc in enumerate(groups):
            g0 = gi * sub
            if g0 > s_i:
                beats = other >= sc
            elif g0 + sub - 1 <= s_i:
                beats = other > sc
            else:
                beats = (other > sc) | ((other == sc) & (row8 > s_i - g0))
            ranks[gi] = ranks[gi] + jnp.where(beats, 1.0, 0.0)
    rank = jnp.concatenate(ranks, axis=0)
    sel = ((rank < min(SLC_TOPK, nslc)) & (blk <= cur)).astype(F32)
    if nblk < nslc:
        sel = jnp.concatenate([sel, jnp.zeros((nslc - nblk, tq), F32)], axis=0)
    sel_ref[0, 0] = sel
    for a in range(tq // ATT_T):
        used_ref[0, 0, 0, :, a:a + 1] = jnp.max(sel[:, a * ATT_T:(a + 1) * ATT_T], axis=1, keepdims=True)


def _cmp_kernel(*refs, tq, nc, nslc):
    t0 = pl.program_id(2) * tq
    early = t0 + tq <= (nslc // 2) * SLC_BLOCK

    @pl.when(early)
    def _():
        _cmp_body(*refs, t0, nc // 2, nslc // 2, tq=tq, nc=nc, nslc=nslc)

    @pl.when(jnp.logical_not(early))
    def _():
        _cmp_body(*refs, t0, nc, nslc, tq=tq, nc=nc, nslc=nslc)


def _cmp_attention(za, kvc, gates, batch, seq, tq=256):
    nc = seq // CMP_STRIDE
    nslc = seq // SLC_BLOCK
    nq = seq // tq
    c_start = np.arange(nc) * CMP_STRIDE
    s_start = np.arange(nslc) * SLC_BLOCK
    overlap = np.clip(np.minimum(c_start[:, None] + CMP_BLOCK, s_start[None, :] + SLC_BLOCK)
                      - np.maximum(c_start[:, None], s_start[None, :]), 0, None)
    wmap_t = (overlap.astype(np.float32) / CMP_STRIDE).T.copy()
    wmap_t[:, nc - 1] = 0.0
    sub = tq // ATT_T
    return pl.pallas_call(
        functools.partial(_cmp_kernel, tq=tq, nc=nc, nslc=nslc),
        grid=(batch, NSA_GROUPS, nq),
        in_specs=[pl.BlockSpec((1, 8, 512), lambda b, g, i: (g, 0, 0)),
                  pl.BlockSpec((nslc, nc), lambda b, g, i: (0, 0)),
                  pl.BlockSpec((tq, NSA_GW), lambda b, g, i: (b * nq + i, OFF_BQ // NSA_GW + g)),
                  pl.BlockSpec((1, 1, nc, NSA_DIM), lambda b, g, i: (0, b * NSA_GROUPS + g, 0, 0)),
                  pl.BlockSpec((1, 1, nc, NSA_DIM), lambda b, g, i: (1, b * NSA_GROUPS + g, 0, 0)),
                  pl.BlockSpec((1, 1, tq, 3 * NSA_REP), lambda b, g, i: (b, g, i, 0))],
        out_specs=[pl.BlockSpec((tq, NSA_GW), lambda b, g, i: (b * nq + i, g)),
                   pl.BlockSpec((1, 1, nslc, tq), lambda b, g, i: (b, g, 0, i)),
                   pl.BlockSpec((1, 1, 1, nslc, sub), lambda b, g, i: (b, g, i, 0, 0))],
        out_shape=[jax.ShapeDtypeStruct((batch * seq, NSA_WIDTH), F32),
                   jax.ShapeDtypeStruct((batch, NSA_GROUPS, nslc, seq), F32),
                   jax.ShapeDtypeStruct((batch, NSA_GROUPS, nq, nslc, sub), F32)],
        scratch_shapes=[pltpu.VMEM((nslc, tq), F32)],
        compiler_params=_cparams("parallel", "parallel", "arbitrary"),
        name="nsa_cmp_select",
    )(_slope_table(), jnp.asarray(wmap_t), za, kvc, kvc, gates)


ATT_SUPER_SHIFT = 3
ATT_SUPER = 1 << ATT_SUPER_SHIFT
ALIBI_FEATS = 6


def _alibi_query_features():
    slopes = np.exp2(-8.0 * np.arange(1, NSA_HEADS + 1, dtype=np.float64) / NSA_HEADS).astype(np.float32)

    def top_bits(x):
        return (x.view(np.uint32) & np.uint32(0xFFFF0000)).view(np.float32)

    s1 = top_bits(slopes)
    r1 = slopes - s1
    s2 = top_bits(r1)
    s3 = r1 - s2
    rows = np.stack([-64.0 * s1, -64.0 * s2, -64.0 * s3, -s1, -s2, -s3]).astype(np.float32)
    feat = np.zeros((NSA_GROUPS, NSA_DIM, NSA_REP, ATT_T), np.float32)
    feat[:, :ALIBI_FEATS] = rows.reshape(ALIBI_FEATS, NSA_GROUPS, NSA_REP).transpose(1, 0, 2)[:, :, :, None]
    return jnp.asarray(feat.reshape(NSA_GROUPS, NSA_DIM, NSA_GW), dtype=BF16)


def _attend(k_ref, v_ref, chunks, valids, qi, qt_aug, kf_static, kf_ind, m_ref, l_ref, acc_ref):
    t = ATT_T
    ks, vs, mbs = [], [], []
    for c, valid in zip(chunks, valids):
        k0 = pl.multiple_of(c * t, t)
        kfeat = (kf_static + (2 * (qi - c)).astype(F32) * kf_ind).astype(BF16)
        ks.append(jnp.concatenate([k_ref[pl.ds(k0, t), :].astype(BF16), kfeat], axis=1))
        vs.append(v_ref[pl.ds(k0, t), :].astype(BF16))
        mbs.append(jnp.where(valid, 0.0, NEG))
    st = _dot(jnp.concatenate(ks, axis=0), qt_aug)
    mbias = jnp.concatenate(mbs, axis=0)
    ps, alphas = [], []
    for r in range(NSA_REP):
        cs = slice(r * t, (r + 1) * t)
        s = st[:, cs] + mbias
        m_old = m_ref[:, cs]
        m_new = jnp.maximum(m_old, jnp.max(s, axis=0, keepdims=True))
        alphas.append(jnp.exp(m_old - m_new))
        p = jnp.exp(s - m_new)
        l_ref[:, cs] = alphas[-1] * l_ref[:, cs] + jnp.sum(p, axis=0, keepdims=True)
        m_ref[:, cs] = m_new
        ps.append(p.astype(BF16))
    acc_ref[...] = (jnp.concatenate(alphas, axis=1) * acc_ref[...]
                    + _dot_tn(jnp.concatenate(vs, axis=0), jnp.concatenate(ps, axis=1)))


def _slc_win_kernel(ids_ref, cnt_ref, qfeat_ref, q_ref, ks_ref, vs_ref, kw_ref, vw_ref, sel_ref, gate_ref,
                    ocmp_ref, o_ref, m_ref, l_ref, acc_ref, *, nq):
    t = ATT_T
    b, g, qi = pl.program_id(0), pl.program_id(1), pl.program_id(2)
    t0 = qi * t
    qt = jnp.concatenate([q_ref[:, r * NSA_DIM:(r + 1) * NSA_DIM].T for r in range(NSA_REP)], axis=1)
    qt_aug = jnp.concatenate([(qt * (NSA_DIM ** -0.5)).astype(BF16), qfeat_ref[0]], axis=0)
    key_i = lax.broadcasted_iota(jnp.int32, (t, t), 0)
    qry_i = lax.broadcasted_iota(jnp.int32, (t, t), 1)
    back = (t - 1) - key_i
    kf_ind = (qry_i < 3).astype(F32)
    kf_static = jnp.where(qry_i < 3, jnp.right_shift(back, SLC_SHIFT),
                          jnp.where(qry_i < ALIBI_FEATS, jnp.bitwise_and(back, SLC_BLOCK - 1), 0)).astype(F32)

    def reset():
        m_ref[...] = jnp.full_like(m_ref, NEG)
        l_ref[...] = jnp.zeros_like(l_ref)
        acc_ref[...] = jnp.zeros_like(acc_ref)

    reset()
    tile = (b * NSA_GROUPS + g) * nq + qi
    cnt = cnt_ref[tile]
    half = SLC_BLOCK
    never = jnp.int32(2 ** 30)

    def slc_step(base, width):
        chunks, valids = [], []
        for u in range(width):
            e = base + u
            c = ids_ref[tile * nq + jnp.maximum(jnp.minimum(e, cnt - 1), 0)]
            lo = jnp.broadcast_to(sel_ref[0, 0, pl.ds(2 * c, 1), :], (half, t))
            hi = jnp.broadcast_to(sel_ref[0, 0, pl.ds(2 * c + 1, 1), :], (half, t))
            picked = jnp.concatenate([lo, hi], axis=0) > 0.5
            dist = (t0 + qry_i) - (c * t + key_i)
            chunks.append(c)
            valids.append(picked & (dist >= jnp.where(e < cnt, 0, never)))
        _attend(ks_ref, vs_ref, chunks, valids, qi, qt_aug, kf_static, kf_ind, m_ref, l_ref, acc_ref)

    def slc_body(s, carry):
        slc_step(s * ATT_SUPER, ATT_SUPER)
        return carry

    n_full = jnp.right_shift(cnt, ATT_SUPER_SHIFT)
    rem = cnt - n_full * ATT_SUPER
    lax.fori_loop(0, n_full, slc_body, 0)

    @pl.when((rem > 0) & (rem <= ATT_SUPER // 2))
    def _():
        slc_step(n_full * ATT_SUPER, ATT_SUPER // 2)

    @pl.when(rem > ATT_SUPER // 2)
    def _():
        slc_step(n_full * ATT_SUPER, ATT_SUPER)

    o_slc = acc_ref[...] * (1.0 / l_ref[...])

    reset()
    chunks, valids = [], []
    for i in range(WIN_SIZE // t + 1):
        c = jnp.maximum(qi - i, 0)
        dist = (t0 + qry_i) - (c * t + key_i)
        chunks.append(c)
        valids.append((dist >= jnp.where(qi - i >= 0, 0, never)) & (dist < WIN_SIZE))
    _attend(kw_ref, vw_ref, chunks, valids, qi, qt_aug, kf_static, kf_ind, m_ref, l_ref, acc_ref)
    o_win = acc_ref[...] * (1.0 / l_ref[...])

    gates = _sigmoid(gate_ref[0, 0])
    for r in range(NSA_REP):
        cs = slice(r * t, (r + 1) * t)
        o_t = gates[3 * r + 1:3 * r + 2, :] * o_slc[:, cs] + gates[3 * r + 2:3 * r + 3, :] * o_win[:, cs]
        o_ref[:, cs] = (o_t.T + ocmp_ref[:, cs]).astype(o_ref.dtype)


def _slc_win_attention(ids, cnt, za, sel_t, gates_t, o_cmp, batch, seq):
    t = ATT_T
    nslc = seq // SLC_BLOCK
    nq = seq // t

    def kv(off):
        return pl.BlockSpec((seq, NSA_DIM), lambda b, g, i, *_: (b, off // NSA_DIM + g))

    grid_spec = pltpu.PrefetchScalarGridSpec(
        num_scalar_prefetch=2,
        grid=(batch, NSA_GROUPS, nq),
        in_specs=[pl.BlockSpec((1, NSA_DIM, NSA_GW), lambda b, g, i, *_: (g, 0, 0)),
                  pl.BlockSpec((t, NSA_GW), lambda b, g, i, *_: (b * nq + i, OFF_BQ // NSA_GW + g)),
                  kv(OFF_BKS), kv(OFF_BVS), kv(OFF_BKW), kv(OFF_BVW),
                  pl.BlockSpec((1, 1, nslc, t), lambda b, g, i, *_: (b, g, 0, i)),
                  pl.BlockSpec((1, 1, 3 * NSA_REP, t), lambda b, g, i, *_: (b, g, 0, i)),
                  pl.BlockSpec((t, NSA_GW), lambda b, g, i, *_: (b * nq + i, g))],
        out_specs=pl.BlockSpec((t, NSA_GW), lambda b, g, i, *_: (b * nq + i, g)),
        scratch_shapes=[pltpu.VMEM((1, NSA_GW), F32), pltpu.VMEM((1, NSA_GW), F32),
                        pltpu.VMEM((NSA_DIM, NSA_GW), F32)])
    return pl.pallas_call(
        functools.partial(_slc_win_kernel, nq=nq),
        grid_spec=grid_spec,
        out_shape=jax.ShapeDtypeStruct((batch * seq, NSA_WIDTH), BF16),
        compiler_params=_cparams("parallel", "parallel", "arbitrary"),
        name="nsa_slc_win",
    )(ids, cnt, _alibi_query_features(), za, za, za, za, za, sel_t, gates_t, o_cmp)


def _nsa(za, pe, w1, w2, batch, seq):
    nq = seq // ATT_T
    kvc = _compress(za, pe, w1, w2, batch, seq)
    gates = (za[:, OFF_BGATE:OFF_BGATE + GATE_COLS].reshape(batch, seq, NSA_GROUPS, 3 * NSA_REP)
             .transpose(0, 2, 1, 3))
    o_cmp, sel_t, used = _cmp_attention(za, kvc, gates, batch, seq)
    nslc = seq // SLC_BLOCK
    used = used.transpose(0, 1, 2, 4, 3).reshape(batch * NSA_GROUPS * nq, nslc // 2, 2)
    unused = (jnp.max(used, axis=-1) < 0.5).astype(jnp.int32)
    ids = jnp.argsort(unused, axis=-1, stable=True).astype(jnp.int32).reshape(-1)
    cnt = (nslc // 2 - jnp.sum(unused, axis=-1)).astype(jnp.int32)
    return _slc_win_attention(ids, cnt, za, sel_t, gates.transpose(0, 1, 3, 2), o_cmp, batch, seq)


def _layer(x, layer, norm1_w, w_in_t, lb_table, hgrn_norm_w, pe, w1, w2, w_branch, w_out, norm2_w, w_ff1, w_ff2,
           batch, seq):
    h = _rmsnorm(x, norm1_w, BF16)
    wide = dict(tn=PROJ_TN, a_single_buffer=True)
    za = _ar_matmul(h, w_in_t, layer, F32, n_cols=ZA_WIDTH, transposed=True, name="proj_in_a", **wide)
    zb = _ar_matmul(h, w_in_t, layer, F32, n_cols=ZB_WIDTH, col0=OFF_BGATE + GATE_COLS, transposed=True,
                    name="proj_in_b", **wide)
    o_a = _hgrn2(za, lb_table, hgrn_norm_w, layer, batch, seq)
    o_b = _nsa(za, pe, w1, w2, batch, seq)
    o_c = _retention(zb, batch, seq)
    merged = _branch_merge(o_a, o_b, o_c, w_branch, layer, zb)
    x = _ar_matmul(merged, w_out, layer, F32, n_cols=D_MODEL, res=x, name="proj_out", **wide)
    h = _rmsnorm(x, norm2_w, BF16)
    u, w_ff2_bf = _ar_matmul(h, w_ff1, layer, BF16, n_cols=D_FF, relu2=True, side_cast=w_ff2, name="ffn_up",
                             **wide)
    return _ar_matmul(u, w_ff2_bf, 0, F32, n_cols=D_MODEL, res=x, tm=512, name="ffn_down")


def kernel(x, norm1_w, w_in, hgrn_lb_table, hgrn_norm_w, cmp_pe_k, cmp_pe_v, cmp_w1_k, cmp_w1_v, cmp_w2_k,
           cmp_w2_v, w_branch, w_out, norm2_w, w_ff1, w_ff2, final_norm_w):
    batch, seq, d = x.shape
    xf = x.reshape(batch * seq, d)
    w_in_t = jnp.swapaxes(w_in, 1, 2)
    for l in range(DEPTH):
        pe = jnp.stack([cmp_pe_k[l], cmp_pe_v[l]])
        w1 = jnp.stack([cmp_w1_k[l], cmp_w1_v[l]]).astype(BF16)
        w2 = jnp.stack([cmp_w2_k[l], cmp_w2_v[l]]).astype(BF16)
        xf = _layer(xf, l, norm1_w[l], w_in_t, hgrn_lb_table, hgrn_norm_w[l], pe, w1, w2, w_branch, w_out,
                    norm2_w[l], w_ff1, w_ff2, batch, seq)
    return _rmsnorm(xf, final_norm_w, F32).reshape(batch, seq, d)
```

```python
import functools

import numpy as np
import jax
import jax.numpy as jnp
from jax import lax
from jax.experimental import pallas as pl
from jax.experimental.pallas import tpu as pltpu

F32 = jnp.float32
BF16 = jnp.bfloat16
HIGHEST = lax.Precision.HIGHEST

D_MODEL = 4096
DEPTH = 2
NORM_EPS = 1e-6
LANE = 128

HG_HEADS = 16
HG_DIM = 128
HG_WIDTH = HG_HEADS * HG_DIM
HG_CHUNK = 64
HG_SUB = 16
HG_HPB = 4
NSA_HEADS = 16
NSA_GROUPS = 4
NSA_REP = NSA_HEADS // NSA_GROUPS
NSA_DIM = 128
NSA_WIDTH = NSA_HEADS * NSA_DIM
NSA_KV_WIDTH = NSA_GROUPS * NSA_DIM
NSA_GW = NSA_REP * NSA_DIM
CMP_BLOCK = 32
CMP_STRIDE = 16
CMP_HIDDEN = 256
SLC_BLOCK = 64
SLC_SHIFT = 6
SLC_TOPK = 16
WIN_SIZE = 512
ATT_T = 128
RET_HEADS = 8
RET_QK = 128
RET_V = 256
RET_QK_WIDTH = RET_HEADS * RET_QK
RET_V_WIDTH = RET_HEADS * RET_V
RET_CHUNK = 128
MIX_WIDTH = HG_WIDTH + NSA_WIDTH + RET_V_WIDTH
D_FF = 4 * D_MODEL

PROJ_TN = 512
GATE_COLS = NSA_HEADS * 3
OFF_AQ = 0
OFF_AF = OFF_AQ + HG_WIDTH
OFF_AI = OFF_AF + HG_WIDTH
OFF_AG = OFF_AI + HG_WIDTH
OFF_BQ = OFF_AG + HG_WIDTH
OFF_BKC = OFF_BQ + NSA_WIDTH
OFF_BVC = OFF_BKC + NSA_KV_WIDTH
OFF_BKS = OFF_BVC + NSA_KV_WIDTH
OFF_BVS = OFF_BKS + NSA_KV_WIDTH
OFF_BKW = OFF_BVS + NSA_KV_WIDTH
OFF_BVW = OFF_BKW + NSA_KV_WIDTH
OFF_BGATE = OFF_BVW + NSA_KV_WIDTH
ZA_WIDTH = OFF_BGATE + PROJ_TN
ZB_CQ = 0
ZB_CK = ZB_CQ + RET_QK_WIDTH
ZB_CV = ZB_CK + RET_QK_WIDTH
ZB_CG = ZB_CV + RET_V_WIDTH
ZB_MA = ZB_CG + RET_V_WIDTH
ZB_MB = ZB_MA + D_MODEL
ZB_MC = ZB_MB + D_MODEL
ZB_WIDTH = ZB_MC + D_MODEL

NEG = -1e30
VMEM_LIMIT = 56 * 1024 * 1024


def _cparams(*sem):
    return pltpu.CompilerParams(dimension_semantics=sem, vmem_limit_bytes=VMEM_LIMIT)


def _sigmoid(x):
    return 1.0 / (1.0 + jnp.exp(-x))


def _silu(x):
    return x * _sigmoid(x)


def _dot(a, b, precision=None):
    return jnp.dot(a, b, preferred_element_type=F32, precision=precision)


def _dot_nt(a, b, precision=None):
    return lax.dot_general(a, b, (((1,), (1,)), ((), ())), preferred_element_type=F32, precision=precision)


def _dot_tn(a, b):
    return lax.dot_general(a, b, (((0,), (0,)), ((), ())), preferred_element_type=F32)


def _rmsnorm_kernel(x_ref, w_ref, o_ref):
    x = x_ref[...]
    y = x * lax.rsqrt(jnp.mean(x * x, axis=-1, keepdims=True) + NORM_EPS)
    o_ref[...] = (y * w_ref[...]).astype(o_ref.dtype)


def _rmsnorm(x, w, out_dtype, tm=256):
    m, d = x.shape
    return pl.pallas_call(
        _rmsnorm_kernel,
        grid=(m // tm,),
        in_specs=[pl.BlockSpec((tm, d), lambda i: (i, 0)), pl.BlockSpec((1, d), lambda i: (0, 0))],
        out_specs=pl.BlockSpec((tm, d), lambda i: (i, 0)),
        out_shape=jax.ShapeDtypeStruct((m, d), out_dtype),
        compiler_params=_cparams("parallel"),
        name="rmsnorm",
    )(x, w.reshape(1, d))


CAST_K = 512


def _cast_dot(a_ref, w_ref, transposed):
    if w_ref.dtype == BF16:
        return _dot_nt(a_ref[...], w_ref[0]) if transposed else _dot(a_ref[...], w_ref[0])
    acc = None
    for k0 in range(0, a_ref.shape[1], CAST_K):
        sl = slice(k0, k0 + CAST_K)
        if transposed:
            part = _dot_nt(a_ref[:, sl], w_ref[0, :, sl].astype(BF16))
        else:
            part = _dot(a_ref[:, sl], w_ref[0, sl, :].astype(BF16))
        acc = part if acc is None else acc + part
    return acc


def _ar_kernel(*refs, transposed, relu2, residual, side):
    a_ref, w_ref = refs[0], refs[1]
    n_in = 2 + residual + side
    o_ref = refs[n_in]
    if side:
        refs[n_in + 1][...] = refs[n_in - 1][...].astype(BF16)
    acc = _cast_dot(a_ref, w_ref, transposed)
    if relu2:
        acc = jnp.maximum(acc, 0.0)
        acc = acc * acc
    if residual:
        acc = acc + refs[2][...]
    o_ref[...] = acc.astype(o_ref.dtype)


def _ar_matmul(a, w, layer, out_dtype, *, n_cols, col0=0, transposed=False, relu2=False, res=None, tm=2048,
               tn=256, a_single_buffer=False, side_cast=None, name="ar_matmul"):
    m, k = a.shape
    a_mode = dict(pipeline_mode=pl.Buffered(1)) if a_single_buffer else {}
    if transposed:
        w_spec = pl.BlockSpec((pl.Element(1), pl.Element(tn), pl.Element(k)),
                              lambda i, j: (layer, pl.multiple_of(col0 + j * tn, 8), 0))
    else:
        w_spec = pl.BlockSpec((1, k, tn), lambda i, j: (layer, 0, col0 // tn + j))
    in_specs = [pl.BlockSpec((tm, k), lambda i, j: (i, 0), **a_mode), w_spec]
    args = [a, w]
    if res is not None:
        in_specs.append(pl.BlockSpec((tm, tn), lambda i, j: (i, j)))
        args.append(res)
    nj = n_cols // tn
    out_specs = pl.BlockSpec((tm, tn), lambda i, j: (i, j))
    out_shape = jax.ShapeDtypeStruct((m, n_cols), out_dtype)
    if side_cast is not None:
        _, rows, cols = side_cast.shape
        per = rows // ((m // tm) * nj)
        in_specs.append(pl.BlockSpec((1, per, cols), lambda i, j: (layer, i * nj + j, 0)))
        args.append(side_cast)
        out_specs = [out_specs, pl.BlockSpec((1, per, cols), lambda i, j: (0, i * nj + j, 0))]
        out_shape = [out_shape, jax.ShapeDtypeStruct((1, rows, cols), BF16)]
    return pl.pallas_call(
        functools.partial(_ar_kernel, transposed=transposed, relu2=relu2, residual=res is not None,
                          side=side_cast is not None),
        grid=(m // tm, nj),
        in_specs=in_specs,
        out_specs=out_specs,
        out_shape=out_shape,
        compiler_params=_cparams("parallel", "arbitrary"),
        name=name,
    )(*args)


def _merge_kernel(oa_ref, ob_ref, oc_ref, wa_ref, wb_ref, wc_ref, ga_ref, gb_ref, gc_ref, o_ref):
    acc = _sigmoid(ga_ref[...]) * _cast_dot(oa_ref, wa_ref, False)
    acc += _sigmoid(gb_ref[...]) * _cast_dot(ob_ref, wb_ref, False)
    acc += _sigmoid(gc_ref[...]) * _cast_dot(oc_ref, wc_ref, False)
    o_ref[...] = acc.astype(o_ref.dtype)


def _branch_merge(o_a, o_b, o_c, w_branch, layer, zb, *, tm=1024, tn=256):
    m, kb = o_a.shape
    o_spec = pl.BlockSpec((tm, kb), lambda i, j: (i, 0))

    def w_spec(r):
        return pl.BlockSpec((1, kb, tn), lambda i, j: (layer, r, j))

    def g_spec(off):
        return pl.BlockSpec((tm, tn), lambda i, j: (i, off // tn + j))

    return pl.pallas_call(
        _merge_kernel,
        grid=(m // tm, D_MODEL // tn),
        in_specs=[o_spec, o_spec, o_spec, w_spec(0), w_spec(1), w_spec(2),
                  g_spec(ZB_MA), g_spec(ZB_MB), g_spec(ZB_MC)],
        out_specs=pl.BlockSpec((tm, tn), lambda i, j: (i, j)),
        out_shape=jax.ShapeDtypeStruct((m, D_MODEL), BF16),
        compiler_params=_cparams("parallel", "arbitrary"),
        name="branch_merge",
    )(o_a, o_b, o_c, w_branch, w_branch, w_branch, zb, zb, zb)


def _hgrn_chunk(qc, kk, b, v, g, nw, st):
    c = HG_CHUNK
    o = _dot_nt((qc * jnp.exp(b)).astype(BF16), st.astype(BF16))
    b_last = b[c - 1:c, :]
    kdec = kk * jnp.exp(b_last - b)
    v16 = v.astype(BF16)
    st_new = st * jnp.exp(b_last) + _dot_tn(v16, kdec.astype(BF16))
    row = lax.broadcasted_iota(jnp.int32, (HG_SUB, HG_DIM), 0)
    col = lax.broadcasted_iota(jnp.int32, (HG_SUB, c), 1)
    atts = [jnp.zeros((HG_SUB, c), F32)]
    for a in range(1, c // HG_SUB):
        lo = a * HG_SUB
        bref = b[lo:lo + 1, :]
        qn = qc[lo:lo + HG_SUB, :] * jnp.exp(b[lo:lo + HG_SUB, :] - bref)
        kn = kk * jnp.exp(jnp.minimum(bref - b, 0.0))
        atts.append(jnp.where(col < lo, _dot_nt(qn.astype(BF16), kn.astype(BF16)), 0.0))
    o = o + _dot(jnp.concatenate(atts, axis=0).astype(BF16), v16)
    parts = []
    for a in range(c // HG_SUB):
        lo = a * HG_SUB
        ba = b[lo:lo + HG_SUB, :]
        qa = qc[lo:lo + HG_SUB, :]
        acc = jnp.zeros((HG_SUB, HG_DIM), F32)
        for j in range(HG_SUB):
            jj = lo + j
            d = jnp.where(row >= j, ba - b[jj:jj + 1, :], NEG)
            w = jnp.sum(qa * kk[jj:jj + 1, :] * jnp.exp(d), axis=-1, keepdims=True)
            acc = acc + w * v[jj:jj + 1, :]
        parts.append(acc)
    o = o + jnp.concatenate(parts, axis=0)
    o = o * lax.rsqrt(jnp.mean(o * o, axis=-1, keepdims=True) + NORM_EPS) * nw * _silu(g)
    return o, st_new


def _hgrn_kernel(tab_ref, nw_ref, q_ref, f_ref, v_ref, g_ref, o_ref, st_ref, *, layer, rows):
    @pl.when(pl.program_id(2) == 0)
    def _():
        st_ref[...] = jnp.zeros_like(st_ref)

    tab = tab_ref[...]
    e = jnp.exp(tab - jnp.max(tab, axis=0, keepdims=True))
    p = e / jnp.sum(e, axis=0, keepdims=True)
    lb = jnp.sum(p[:layer + 1, :], axis=0, keepdims=True) - p[0:1, :]
    nw = nw_ref[...]
    r_i = lax.broadcasted_iota(jnp.int32, (HG_CHUNK, HG_CHUNK), 0)
    c_i = lax.broadcasted_iota(jnp.int32, (HG_CHUNK, HG_CHUNK), 1)
    tri = (r_i >= c_i).astype(F32)
    sts = [st_ref[hh] for hh in range(HG_HPB)]
    for ch in range(rows // HG_CHUNK):
        sl = slice(ch * HG_CHUNK, (ch + 1) * HG_CHUNK)
        qc = _silu(q_ref[sl, :]) * (HG_DIM ** -0.5)
        f = lb + (1.0 - lb) * _sigmoid(f_ref[sl, :])
        kk = 1.0 - f
        b = _dot(tri, jnp.log(f), precision=HIGHEST)
        outs = []
        for hh in range(HG_HPB):
            hs = slice(hh * HG_DIM, (hh + 1) * HG_DIM)
            o, sts[hh] = _hgrn_chunk(qc[:, hs], kk[:, hs], b[:, hs], v_ref[sl, hs], g_ref[sl, hs], nw, sts[hh])
            outs.append(o)
        o_ref[sl, :] = jnp.concatenate(outs, axis=1).astype(o_ref.dtype)
    for hh in range(HG_HPB):
        st_ref[hh] = sts[hh]


def _hgrn2(za, lb_table, norm_w, layer, batch, seq, rows=256):
    nrow = seq // rows
    width = HG_HPB * HG_DIM

    def col(off):
        return pl.BlockSpec((rows, width), lambda b, h, c: (b * nrow + c, off // width + h))

    return pl.pallas_call(
        functools.partial(_hgrn_kernel, layer=layer, rows=rows),
        grid=(batch, HG_HEADS // HG_HPB, nrow),
        in_specs=[pl.BlockSpec((DEPTH, width), lambda b, h, c: (0, h)),
                  pl.BlockSpec((1, HG_DIM), lambda b, h, c: (0, 0)),
                  col(OFF_AQ), col(OFF_AF), col(OFF_AI), col(OFF_AG)],
        out_specs=pl.BlockSpec((rows, width), lambda b, h, c: (b * nrow + c, h)),
        out_shape=jax.ShapeDtypeStruct((batch * seq, HG_WIDTH), BF16),
        scratch_shapes=[pltpu.VMEM((HG_HPB, HG_DIM, HG_DIM), F32)],
        compiler_params=_cparams("parallel", "parallel", "arbitrary"),
        name="hgrn2",
    )(lb_table, norm_w.reshape(1, HG_DIM), za, za, za, za)


def _ret_kernel(lg_ref, q_ref, k_ref, v_ref, g_ref, o_ref, s_ref, *, rows):
    @pl.when(pl.program_id(2) == 0)
    def _():
        s_ref[...] = jnp.zeros_like(s_ref)

    c = RET_CHUNK
    lg_v = lg_ref[0]
    lg_k = lg_v[:, :c]
    pos_r = lax.broadcasted_iota(jnp.int32, (c, RET_V), 0).astype(F32)
    query_decay = jnp.exp(lg_v * (pos_r + 1.0))
    rel = (lax.broadcasted_iota(jnp.int32, (c, c), 0) - lax.broadcasted_iota(jnp.int32, (c, c), 1)).astype(F32)
    intra_decay = jnp.where(rel >= 0, jnp.exp(lg_k * jnp.maximum(rel, 0.0)), 0.0)
    key_decay = jnp.exp(lg_k * (c - 1.0 - pos_r[:, :RET_QK]))
    chunk_decay = jnp.exp(lg_v * float(c))
    s = s_ref[...]
    for ch in range(rows // c):
        sl = slice(ch * c, (ch + 1) * c)
        q = q_ref[sl, :].astype(BF16)
        k = k_ref[sl, :] * (RET_QK ** -0.5)
        v = v_ref[sl, :].astype(BF16)
        inter = _dot(q, s.astype(BF16)) * query_decay
        scores = _dot_nt(q, k.astype(BF16)) * intra_decay
        o = inter + _dot(scores.astype(BF16), v)
        s = chunk_decay * s + _dot_tn((k * key_decay).astype(BF16), v)
        mu = jnp.mean(o, axis=-1, keepdims=True)
        oc = o - mu
        o = oc * lax.rsqrt(jnp.mean(oc * oc, axis=-1, keepdims=True) + NORM_EPS)
        o_ref[sl, :] = (o * _silu(g_ref[sl, :])).astype(o_ref.dtype)
    s_ref[...] = s


def _retention(zb, batch, seq, rows=1024):
    rows = min(rows, seq)
    nrow = seq // rows
    log_gamma = jnp.log(1.0 - jnp.exp2(-5.0 - jnp.arange(RET_HEADS, dtype=F32)))
    lg = jnp.broadcast_to(log_gamma[:, None, None], (RET_HEADS, 1, RET_V))
    return pl.pallas_call(
        functools.partial(_ret_kernel, rows=rows),
        grid=(batch, RET_HEADS, nrow),
        in_specs=[pl.BlockSpec((1, 1, RET_V), lambda b, h, c: (h, 0, 0)),
                  pl.BlockSpec((rows, RET_QK), lambda b, h, c: (b * nrow + c, ZB_CQ // RET_QK + h)),
                  pl.BlockSpec((rows, RET_QK), lambda b, h, c: (b * nrow + c, ZB_CK // RET_QK + h)),
                  pl.BlockSpec((rows, RET_V), lambda b, h, c: (b * nrow + c, ZB_CV // RET_V + h)),
                  pl.BlockSpec((rows, RET_V), lambda b, h, c: (b * nrow + c, ZB_CG // RET_V + h))],
        out_specs=pl.BlockSpec((rows, RET_V), lambda b, h, c: (b * nrow + c, h)),
        out_shape=jax.ShapeDtypeStruct((batch * seq, RET_V_WIDTH), BF16),
        scratch_shapes=[pltpu.VMEM((RET_QK, RET_V), F32)],
        compiler_params=_cparams("parallel", "parallel", "arbitrary"),
        name="retention",
    )(lg, zb, zb, zb, zb)


def _compress_kernel(kv_ref, pe_ref, w1_ref, w2_ref, o_ref, *, nc):
    d = NSA_DIM
    top = bot = None
    for l in range(CMP_STRIDE):
        x = kv_ref[pl.ds(l, nc, stride=CMP_STRIDE), :]
        pt = _dot((x + pe_ref[0, l:l + 1, :]).astype(BF16), w1_ref[0, l * d:(l + 1) * d, :])
        lb = CMP_STRIDE + l
        pb = _dot((x + pe_ref[0, lb:lb + 1, :]).astype(BF16), w1_ref[0, lb * d:(lb + 1) * d, :])
        top = pt if top is None else top + pt
        bot = pb if bot is None else bot + pb
    h = top + pltpu.roll(bot, nc - 1, 0)
    o_ref[0, 0] = _dot(_silu(h).astype(BF16), w2_ref[0])


def _compress(za, pe, w1, w2, batch, seq):
    nc = seq // CMP_STRIDE
    kv_blocks = NSA_KV_WIDTH // NSA_DIM
    return pl.pallas_call(
        functools.partial(_compress_kernel, nc=nc),
        grid=(2, batch, NSA_GROUPS),
        in_specs=[pl.BlockSpec((seq, NSA_DIM), lambda s, b, g: (b, OFF_BKC // NSA_DIM + s * kv_blocks + g)),
                  pl.BlockSpec((1, CMP_BLOCK, NSA_DIM), lambda s, b, g: (s, 0, 0)),
                  pl.BlockSpec((1, CMP_BLOCK * NSA_DIM, CMP_HIDDEN), lambda s, b, g: (s, 0, 0)),
                  pl.BlockSpec((1, CMP_HIDDEN, NSA_DIM), lambda s, b, g: (s, 0, 0))],
        out_specs=pl.BlockSpec((1, 1, nc, NSA_DIM), lambda s, b, g: (s, b * NSA_GROUPS + g, 0, 0)),
        out_shape=jax.ShapeDtypeStruct((2, batch * NSA_GROUPS, nc, NSA_DIM), F32),
        compiler_params=_cparams("parallel", "parallel", "parallel"),
        name="nsa_compress",
    )(za, pe, w1, w2)


def _slope_table():
    slopes = np.exp2(-8.0 * np.arange(1, NSA_HEADS + 1, dtype=np.float64) / NSA_HEADS).astype(np.float32)
    tab = np.zeros((NSA_GROUPS, 8, 512), np.float32)
    tab[:, :NSA_REP, :] = slopes.reshape(NSA_GROUPS, NSA_REP)[:, :, None]
    return jnp.asarray(tab)


def _cmp_body(slope_ref, wmap_ref, q_ref, kc_ref, vc_ref, gate_ref, o_ref, sel_ref, used_ref, score_ref,
              t0, ncols, nblk, *, tq, nc, nslc):
    kc = kc_ref[0, 0, :ncols, :]
    vc = vc_ref[0, 0, :ncols, :].astype(BF16)
    gates = _sigmoid(gate_ref[0, 0])
    t_i = t0 + lax.broadcasted_iota(jnp.int32, (tq, ncols), 0)
    n_i = lax.broadcasted_iota(jnp.int32, (tq, ncols), 1)
    dist = t_i - (n_i * CMP_STRIDE + CMP_BLOCK - 1)
    valid = (dist >= 0) & (n_i < nc - 1)
    distf = dist.astype(F32)
    imp = jnp.zeros((tq, ncols), F32)
    q_all = jnp.concatenate([q_ref[:, r * NSA_DIM:(r + 1) * NSA_DIM] for r in range(NSA_REP)], axis=0)
    s_all = _dot_nt(q_all, kc, precision=HIGHEST)
    ps = []
    for r in range(NSA_REP):
        slope = slope_ref[0, r:r + 1, :ncols]
        s = s_all[r * tq:(r + 1) * tq, :] * (NSA_DIM ** -0.5) - slope * distf
        s = jnp.where(valid, s, NEG)
        m = jnp.max(s, axis=-1, keepdims=True)
        e = jnp.where(valid, jnp.exp(s - m), 0.0)
        den = jnp.sum(e, axis=-1, keepdims=True)
        p = e / jnp.where(den > 0, den, 1.0)
        ps.append(p.astype(BF16))
        imp = imp + p
    o_all = _dot(jnp.concatenate(ps, axis=0), vc)
    for r in range(NSA_REP):
        o_ref[:, r * NSA_DIM:(r + 1) * NSA_DIM] = gates[:, 3 * r:3 * r + 1] * o_all[r * tq:(r + 1) * tq, :]
    imp_t = _dot_nt(wmap_ref[:nblk, :ncols], imp, precision=HIGHEST)
    blk = lax.broadcasted_iota(jnp.int32, (nblk, tq), 0)
    cur = jnp.right_shift(t0 + lax.broadcasted_iota(jnp.int32, (nblk, tq), 1), SLC_SHIFT)
    forced = (blk == 0) | (blk == cur) | (blk == cur - 1)
    score = jnp.where(blk > cur, -jnp.inf, jnp.where(forced, jnp.inf, imp_t))
    score_ref[:nblk, :] = score
    sub = 8
    groups = [score_ref[g0:g0 + sub, :] for g0 in range(0, nblk, sub)]
    ranks = [jnp.zeros((sub, tq), F32) for _ in groups]
    row8 = lax.broadcasted_iota(jnp.int32, (sub, tq), 0)
    for s_i in range(nblk):
        other = jnp.broadcast_to(score_ref[s_i:s_i + 1, :], (sub, tq))
        for gi, sc in enumerate(groups):
            g0 = gi * sub
            if g0 > s_i:
                beats = other >= sc
            elif g0 + sub - 1 <= s_i:
                beats = other > sc
            else:
                beats = (other > sc) | ((other == sc) & (row8 > s_i - g0))
            ranks[gi] = ranks[gi] + jnp.where(beats, 1.0, 0.0)
    rank = jnp.concatenate(ranks, axis=0)
    sel = ((rank < min(SLC_TOPK, nslc)) & (blk <= cur)).astype(F32)
    if nblk < nslc:
        sel = jnp.concatenate([sel, jnp.zeros((nslc - nblk, tq), F32)], axis=0)
    sel_ref[0, 0] = sel
    for a in range(tq // ATT_T):
        used_ref[0, 0, 0, :, a:a + 1] = jnp.max(sel[:, a * ATT_T:(a + 1) * ATT_T], axis=1, keepdims=True)


def _cmp_kernel(*refs, tq, nc, nslc):
    t0 = pl.program_id(2) * tq
    early = t0 + tq <= (nslc // 2) * SLC_BLOCK

    @pl.when(early)
    def _():
        _cmp_body(*refs, t0, nc // 2, nslc // 2, tq=tq, nc=nc, nslc=nslc)

    @pl.when(jnp.logical_not(early))
    def _():
        _cmp_body(*refs, t0, nc, nslc, tq=tq, nc=nc, nslc=nslc)


def _cmp_attention(za, kvc, gates, batch, seq, tq=256):
    nc = seq // CMP_STRIDE
    nslc = seq // SLC_BLOCK
    nq = seq // tq
    c_start = np.arange(nc) * CMP_STRIDE
    s_start = np.arange(nslc) * SLC_BLOCK
    overlap = np.clip(np.minimum(c_start[:, None] + CMP_BLOCK, s_start[None, :] + SLC_BLOCK)
                      - np.maximum(c_start[:, None], s_start[None, :]), 0, None)
    wmap_t = (overlap.astype(np.float32) / CMP_STRIDE).T.copy()
    wmap_t[:, nc - 1] = 0.0
    sub = tq // ATT_T
    return pl.pallas_call(
        functools.partial(_cmp_kernel, tq=tq, nc=nc, nslc=nslc),
        grid=(batch, NSA_GROUPS, nq),
        in_specs=[pl.BlockSpec((1, 8, 512), lambda b, g, i: (g, 0, 0)),
                  pl.BlockSpec((nslc, nc), lambda b, g, i: (0, 0)),
                  pl.BlockSpec((tq, NSA_GW), lambda b, g, i: (b * nq + i, OFF_BQ // NSA_GW + g)),
                  pl.BlockSpec((1, 1, nc, NSA_DIM), lambda b, g, i: (0, b * NSA_GROUPS + g, 0, 0)),
                  pl.BlockSpec((1, 1, nc, NSA_DIM), lambda b, g, i: (1, b * NSA_GROUPS + g, 0, 0)),
                  pl.BlockSpec((1, 1, tq, 3 * NSA_REP), lambda b, g, i: (b, g, i, 0))],
        out_specs=[pl.BlockSpec((tq, NSA_GW), lambda b, g, i: (b * nq + i, g)),
                   pl.BlockSpec((1, 1, nslc, tq), lambda b, g, i: (b, g, 0, i)),
                   pl.BlockSpec((1, 1, 1, nslc, sub), lambda b, g, i: (b, g, i, 0, 0))],
        out_shape=[jax.ShapeDtypeStruct((batch * seq, NSA_WIDTH), F32),
                   jax.ShapeDtypeStruct((batch, NSA_GROUPS, nslc, seq), F32),
                   jax.ShapeDtypeStruct((batch, NSA_GROUPS, nq, nslc, sub), F32)],
        scratch_shapes=[pltpu.VMEM((nslc, tq), F32)],
        compiler_params=_cparams("parallel", "parallel", "arbitrary"),
        name="nsa_cmp_select",
    )(_slope_table(), jnp.asarray(wmap_t), za, kvc, kvc, gates)


ATT_SUPER_SHIFT = 3
ATT_SUPER = 1 << ATT_SUPER_SHIFT
ALIBI_FEATS = 6


def _alibi_query_features():
    slopes = np.exp2(-8.0 * np.arange(1, NSA_HEADS + 1, dtype=np.float64) / NSA_HEADS).astype(np.float32)

    def top_bits(x):
        return (x.view(np.uint32) & np.uint32(0xFFFF0000)).view(np.float32)

    s1 = top_bits(slopes)
    r1 = slopes - s1
    s2 = top_bits(r1)
    s3 = r1 - s2
    rows = np.stack([-64.0 * s1, -64.0 * s2, -64.0 * s3, -s1, -s2, -s3]).astype(np.float32)
    feat = np.zeros((NSA_GROUPS, NSA_DIM, NSA_REP, ATT_T), np.float32)
    feat[:, :ALIBI_FEATS] = rows.reshape(ALIBI_FEATS, NSA_GROUPS, NSA_REP).transpose(1, 0, 2)[:, :, :, None]
    return jnp.asarray(feat.reshape(NSA_GROUPS, NSA_DIM, NSA_GW), dtype=BF16)


def _attend(k_ref, v_ref, chunks, valids, qi, qt_aug, kf_static, kf_ind, m_ref, l_ref, acc_ref):
    t = ATT_T
    ks, vs, mbs = [], [], []
    for c, valid in zip(chunks, valids):
        k0 = pl.multiple_of(c * t, t)
        kfeat = (kf_static + (2 * (qi - c)).astype(F32) * kf_ind).astype(BF16)
        ks.append(jnp.concatenate([k_ref[pl.ds(k0, t), :].astype(BF16), kfeat], axis=1))
        vs.append(v_ref[pl.ds(k0, t), :].astype(BF16))
        mbs.append(jnp.where(valid, 0.0, NEG))
    st = _dot(jnp.concatenate(ks, axis=0), qt_aug)
    mbias = jnp.concatenate(mbs, axis=0)
    ps, alphas = [], []
    for r in range(NSA_REP):
        cs = slice(r * t, (r + 1) * t)
        s = st[:, cs] + mbias
        m_old = m_ref[:, cs]
        m_new = jnp.maximum(m_old, jnp.max(s, axis=0, keepdims=True))
        alphas.append(jnp.exp(m_old - m_new))
        p = jnp.exp(s - m_new)
        l_ref[:, cs] = alphas[-1] * l_ref[:, cs] + jnp.sum(p, axis=0, keepdims=True)
        m_ref[:, cs] = m_new
        ps.append(p.astype(BF16))
    acc_ref[...] = (jnp.concatenate(alphas, axis=1) * acc_ref[...]
                    + _dot_tn(jnp.concatenate(vs, axis=0), jnp.concatenate(ps, axis=1)))


def _slc_win_kernel(ids_ref, cnt_ref, qfeat_ref, q_ref, ks_ref, vs_ref, kw_ref, vw_ref, sel_ref, gate_ref,
                    ocmp_ref, o_ref, m_ref, l_ref, acc_ref, *, nq):
    t = ATT_T
    b, g, qi = pl.program_id(0), pl.program_id(1), pl.program_id(2)
    t0 = qi * t
    qt = jnp.concatenate([q_ref[:, r * NSA_DIM:(r + 1) * NSA_DIM].T for r in range(NSA_REP)], axis=1)
    qt_aug = jnp.concatenate([(qt * (NSA_DIM ** -0.5)).astype(BF16), qfeat_ref[0]], axis=0)
    key_i = lax.broadcasted_iota(jnp.int32, (t, t), 0)
    qry_i = lax.broadcasted_iota(jnp.int32, (t, t), 1)
    back = (t - 1) - key_i
    kf_ind = (qry_i < 3).astype(F32)
    kf_static = jnp.where(qry_i < 3, jnp.right_shift(back, SLC_SHIFT),
                          jnp.where(qry_i < ALIBI_FEATS, jnp.bitwise_and(back, SLC_BLOCK - 1), 0)).astype(F32)

    def reset():
        m_ref[...] = jnp.full_like(m_ref, NEG)
        l_ref[...] = jnp.zeros_like(l_ref)
        acc_ref[...] = jnp.zeros_like(acc_ref)

    reset()
    tile = (b * NSA_GROUPS + g) * nq + qi
    cnt = cnt_ref[tile]
    half = SLC_BLOCK
    never = jnp.int32(2 ** 30)

    def slc_step(base, width):
        chunks, valids = [], []
        for u in range(width):
            e = base + u
            c = ids_ref[tile * nq + jnp.maximum(jnp.minimum(e, cnt - 1), 0)]
            lo = jnp.broadcast_to(sel_ref[0, 0, pl.ds(2 * c, 1), :], (half, t))
            hi = jnp.broadcast_to(sel_ref[0, 0, pl.ds(2 * c + 1, 1), :], (half, t))
            picked = jnp.concatenate([lo, hi], axis=0) > 0.5
            dist = (t0 + qry_i) - (c * t + key_i)
            chunks.append(c)
            valids.append(picked & (dist >= jnp.where(e < cnt, 0, never)))
        _attend(ks_ref, vs_ref, chunks, valids, qi, qt_aug, kf_static, kf_ind, m_ref, l_ref, acc_ref)

    def slc_body(s, carry):
        slc_step(s * ATT_SUPER, ATT_SUPER)
        return carry

    n_full = jnp.right_shift(cnt, ATT_SUPER_SHIFT)
    rem = cnt - n_full * ATT_SUPER
    lax.fori_loop(0, n_full, slc_body, 0)

    @pl.when((rem > 0) & (rem <= ATT_SUPER // 2))
    def _():
        slc_step(n_full * ATT_SUPER, ATT_SUPER // 2)

    @pl.when(rem > ATT_SUPER // 2)
    def _():
        slc_step(n_full * ATT_SUPER, ATT_SUPER)

    o_slc = acc_ref[...] * (1.0 / l_ref[...])

    reset()
    chunks, valids = [], []
    for i in range(WIN_SIZE // t + 1):
        c = jnp.maximum(qi - i, 0)
        dist = (t0 + qry_i) - (c * t + key_i)
        chunks.append(c)
        valids.append((dist >= jnp.where(qi - i >= 0, 0, never)) & (dist < WIN_SIZE))
    _attend(kw_ref, vw_ref, chunks, valids, qi, qt_aug, kf_static, kf_ind, m_ref, l_ref, acc_ref)
    o_win = acc_ref[...] * (1.0 / l_ref[...])

    gates = _sigmoid(gate_ref[0, 0])
    for r in range(NSA_REP):
        cs = slice(r * t, (r + 1) * t)
        o_t = gates[3 * r + 1:3 * r + 2, :] * o_slc[:, cs] + gates[3 * r + 2:3 * r + 3, :] * o_win[:, cs]
        o_ref[:, cs] = (o_t.T + ocmp_ref[:, cs]).astype(o_ref.dtype)


def _slc_win_attention(ids, cnt, za, sel_t, gates_t, o_cmp, batch, seq):
    t = ATT_T
    nslc = seq // SLC_BLOCK
    nq = seq // t

    def kv(off):
        return pl.BlockSpec((seq, NSA_DIM), lambda b, g, i, *_: (b, off // NSA_DIM + g))

    grid_spec = pltpu.PrefetchScalarGridSpec(
        num_scalar_prefetch=2,
        grid=(batch, NSA_GROUPS, nq),
        in_specs=[pl.BlockSpec((1, NSA_DIM, NSA_GW), lambda b, g, i, *_: (g, 0, 0)),
                  pl.BlockSpec((t, NSA_GW), lambda b, g, i, *_: (b * nq + i, OFF_BQ // NSA_GW + g)),
                  kv(OFF_BKS), kv(OFF_BVS), kv(OFF_BKW), kv(OFF_BVW),
                  pl.BlockSpec((1, 1, nslc, t), lambda b, g, i, *_: (b, g, 0, i)),
                  pl.BlockSpec((1, 1, 3 * NSA_REP, t), lambda b, g, i, *_: (b, g, 0, i)),
                  pl.BlockSpec((t, NSA_GW), lambda b, g, i, *_: (b * nq + i, g))],
        out_specs=pl.BlockSpec((t, NSA_GW), lambda b, g, i, *_: (b * nq + i, g)),
        scratch_shapes=[pltpu.VMEM((1, NSA_GW), F32), pltpu.VMEM((1, NSA_GW), F32),
                        pltpu.VMEM((NSA_DIM, NSA_GW), F32)])
    return pl.pallas_call(
        functools.partial(_slc_win_kernel, nq=nq),
        grid_spec=grid_spec,
        out_shape=jax.ShapeDtypeStruct((batch * seq, NSA_WIDTH), BF16),
        compiler_params=_cparams("parallel", "parallel", "arbitrary"),
        name="nsa_slc_win",
    )(ids, cnt, _alibi_query_features(), za, za, za, za, za, sel_t, gates_t, o_cmp)


def _nsa(za, pe, w1, w2, batch, seq):
    nq = seq // ATT_T
    kvc = _compress(za, pe, w1, w2, batch, seq)
    gates = (za[:, OFF_BGATE:OFF_BGATE + GATE_COLS].reshape(batch, seq, NSA_GROUPS, 3 * NSA_REP)
             .transpose(0, 2, 1, 3))
    o_cmp, sel_t, used = _cmp_attention(za, kvc, gates, batch, seq)
    nslc = seq // SLC_BLOCK
    used = used.transpose(0, 1, 2, 4, 3).reshape(batch * NSA_GROUPS * nq, nslc // 2, 2)
    unused = (jnp.max(used, axis=-1) < 0.5).astype(jnp.int32)
    ids = jnp.argsort(unused, axis=-1, stable=True).astype(jnp.int32).reshape(-1)
    cnt = (nslc // 2 - jnp.sum(unused, axis=-1)).astype(jnp.int32)
    return _slc_win_attention(ids, cnt, za, sel_t, gates.transpose(0, 1, 3, 2), o_cmp, batch, seq)


def _layer(x, layer, norm1_w, w_in_t, lb_table, hgrn_norm_w, pe, w1, w2, w_branch, w_out, norm2_w, w_ff1, w_ff2,
           batch, seq):
    h = _rmsnorm(x, norm1_w, BF16)
    wide = dict(tn=PROJ_TN, a_single_buffer=True)
    za = _ar_matmul(h, w_in_t, layer, F32, n_cols=ZA_WIDTH, transposed=True, name="proj_in_a", **wide)
    zb = _ar_matmul(h, w_in_t, layer, F32, n_cols=ZB_WIDTH, col0=OFF_BGATE + GATE_COLS, transposed=True,
                    name="proj_in_b", **wide)
    o_a = _hgrn2(za, lb_table, hgrn_norm_w, layer, batch, seq)
    o_b = _nsa(za, pe, w1, w2, batch, seq)
    o_c = _retention(zb, batch, seq)
    merged = _branch_merge(o_a, o_b, o_c, w_branch, layer, zb)
    x = _ar_matmul(merged, w_out, layer, F32, n_cols=D_MODEL, res=x, name="proj_out", **wide)
    h = _rmsnorm(x, norm2_w, BF16)
    u, w_ff2_bf = _ar_matmul(h, w_ff1, layer, BF16, n_cols=D_FF, relu2=True, side_cast=w_ff2, name="ffn_up",
                             **wide)
    return _ar_matmul(u, w_ff2_bf, 0, F32, n_cols=D_MODEL, res=x, tm=512, name="ffn_down")


def kernel(x, norm1_w, w_in, hgrn_lb_table, hgrn_norm_w, cmp_pe_k, cmp_pe_v, cmp_w1_k, cmp_w1_v, cmp_w2_k,
           cmp_w2_v, w_branch, w_out, norm2_w, w_ff1, w_ff2, final_norm_w):
    batch, seq, d = x.shape
    xf = x.reshape(batch * seq, d)
    w_in_t = jnp.swapaxes(w_in, 1, 2)
    for l in range(DEPTH):
        pe = jnp.stack([cmp_pe_k[l], cmp_pe_v[l]])
        w1 = jnp.stack([cmp_w1_k[l], cmp_w1_v[l]]).astype(BF16)
        w2 = jnp.stack([cmp_w2_k[l], cmp_w2_v[l]]).astype(BF16)
        xf = _layer(xf, l, norm1_w[l], w_in_t, hgrn_lb_table, hgrn_norm_w[l], pe, w1, w2, w_branch, w_out,
                    norm2_w[l], w_ff1, w_ff2, batch, seq)
    return _rmsnorm(xf, final_norm_w, F32).reshape(batch, seq, d)
```

```python
import functools

import numpy as np
import jax
import jax.numpy as jnp
from jax import lax
from jax.experimental import pallas as pl
from jax.experimental.pallas import tpu as pltpu

F32 = jnp.float32
BF16 = jnp.bfloat16
HIGHEST = lax.Precision.HIGHEST

D_MODEL = 4096
DEPTH = 2
NORM_EPS = 1e-6
LANE = 128

HG_HEADS = 16
HG_DIM = 128
HG_WIDTH = HG_HEADS * HG_DIM
HG_CHUNK = 64
HG_SUB = 16
HG_HPB = 4
NSA_HEADS = 16
NSA_GROUPS = 4
NSA_REP = NSA_HEADS // NSA_GROUPS
NSA_DIM = 128
NSA_WIDTH = NSA_HEADS * NSA_DIM
NSA_KV_WIDTH = NSA_GROUPS * NSA_DIM
NSA_GW = NSA_REP * NSA_DIM
CMP_BLOCK = 32
CMP_STRIDE = 16
CMP_HIDDEN = 256
SLC_BLOCK = 64
SLC_SHIFT = 6
SLC_TOPK = 16
WIN_SIZE = 512
ATT_T = 128
RET_HEADS = 8
RET_QK = 128
RET_V = 256
RET_QK_WIDTH = RET_HEADS * RET_QK
RET_V_WIDTH = RET_HEADS * RET_V
RET_CHUNK = 128
MIX_WIDTH = HG_WIDTH + NSA_WIDTH + RET_V_WIDTH
D_FF = 4 * D_MODEL

PROJ_TN = 512
GATE_COLS = NSA_HEADS * 3
OFF_AQ = 0
OFF_AF = OFF_AQ + HG_WIDTH
OFF_AI = OFF_AF + HG_WIDTH
OFF_AG = OFF_AI + HG_WIDTH
OFF_BQ = OFF_AG + HG_WIDTH
OFF_BKC = OFF_BQ + NSA_WIDTH
OFF_BVC = OFF_BKC + NSA_KV_WIDTH
OFF_BKS = OFF_BVC + NSA_KV_WIDTH
OFF_BVS = OFF_BKS + NSA_KV_WIDTH
OFF_BKW = OFF_BVS + NSA_KV_WIDTH
OFF_BVW = OFF_BKW + NSA_KV_WIDTH
OFF_BGATE = OFF_BVW + NSA_KV_WIDTH
ZA_WIDTH = OFF_BGATE + PROJ_TN
ZB_CQ = 0
ZB_CK = ZB_CQ + RET_QK_WIDTH
ZB_CV = ZB_CK + RET_QK_WIDTH
ZB_CG = ZB_CV + RET_V_WIDTH
ZB_MA = ZB_CG + RET_V_WIDTH
ZB_MB = ZB_MA + D_MODEL
ZB_MC = ZB_MB + D_MODEL
ZB_WIDTH = ZB_MC + D_MODEL

NEG = -1e30
VMEM_LIMIT = 56 * 1024 * 1024


def _cparams(*sem):
    return pltpu.CompilerParams(dimension_semantics=sem, vmem_limit_bytes=VMEM_LIMIT)


def _sigmoid(x):
    return 1.0 / (1.0 + jnp.exp(-x))


def _silu(x):
    return x * _sigmoid(x)


def _dot(a, b, precision=None):
    return jnp.dot(a, b, preferred_element_type=F32, precision=precision)


def _dot_nt(a, b, precision=None):
    return lax.dot_general(a, b, (((1,), (1,)), ((), ())), preferred_element_type=F32, precision=precision)


def _dot_tn(a, b):
    return lax.dot_general(a, b, (((0,), (0,)), ((), ())), preferred_element_type=F32)


def _rmsnorm_kernel(x_ref, w_ref, o_ref):
    x = x_ref[...]
    y = x * lax.rsqrt(jnp.mean(x * x, axis=-1, keepdims=True) + NORM_EPS)
    o_ref[...] = (y * w_ref[...]).astype(o_ref.dtype)


def _rmsnorm(x, w, out_dtype, tm=256):
    m, d = x.shape
    return pl.pallas_call(
        _rmsnorm_kernel,
        grid=(m // tm,),
        in_specs=[pl.BlockSpec((tm, d), lambda i: (i, 0)), pl.BlockSpec((1, d), lambda i: (0, 0))],
        out_specs=pl.BlockSpec((tm, d), lambda i: (i, 0)),
        out_shape=jax.ShapeDtypeStruct((m, d), out_dtype),
        compiler_params=_cparams("parallel"),
        name="rmsnorm",
    )(x, w.reshape(1, d))


CAST_K = 512


def _cast_dot(a_ref, w_ref, transposed):
    if w_ref.dtype == BF16:
        return _dot_nt(a_ref[...], w_ref[0]) if transposed else _dot(a_ref[...], w_ref[0])
    acc = None
    for k0 in range(0, a_ref.shape[1], CAST_K):
        sl = slice(k0, k0 + CAST_K)
        if transposed:
            part = _dot_nt(a_ref[:, sl], w_ref[0, :, sl].astype(BF16))
        else:
            part = _dot(a_ref[:, sl], w_ref[0, sl, :].astype(BF16))
        acc = part if acc is None else acc + part
    return acc


def _ar_kernel(*refs, transposed, relu2, residual, side):
    a_ref, w_ref = refs[0], refs[1]
    n_in = 2 + residual + side
    o_ref = refs[n_in]
    if side:
        refs[n_in + 1][...] = refs[n_in - 1][...].astype(BF16)
    acc = _cast_dot(a_ref, w_ref, transposed)
    if relu2:
        acc = jnp.maximum(acc, 0.0)
        acc = acc * acc
    if residual:
        acc = acc + refs[2][...]
    o_ref[...] = acc.astype(o_ref.dtype)


def _ar_matmul(a, w, layer, out_dtype, *, n_cols, col0=0, transposed=False, relu2=False, res=None, tm=2048,
               tn=256, a_single_buffer=False, side_cast=None, name="ar_matmul"):
    m, k = a.shape
    a_mode = dict(pipeline_mode=pl.Buffered(1)) if a_single_buffer else {}
    if transposed:
        w_spec = pl.BlockSpec((pl.Element(1), pl.Element(tn), pl.Element(k)),
                              lambda i, j: (layer, pl.multiple_of(col0 + j * tn, 8), 0))
    else:
        w_spec = pl.BlockSpec((1, k, tn), lambda i, j: (layer, 0, col0 // tn + j))
    in_specs = [pl.BlockSpec((tm, k), lambda i, j: (i, 0), **a_mode), w_spec]
    args = [a, w]
    if res is not None:
        in_specs.append(pl.BlockSpec((tm, tn), lambda i, j: (i, j)))
        args.append(res)
    nj = n_cols // tn
    out_specs = pl.BlockSpec((tm, tn), lambda i, j: (i, j))
    out_shape = jax.ShapeDtypeStruct((m, n_cols), out_dtype)
    if side_cast is not None:
        _, rows, cols = side_cast.shape
        per = rows // ((m // tm) * nj)
        in_specs.append(pl.BlockSpec((1, per, cols), lambda i, j: (layer, i * nj + j, 0)))
        args.append(side_cast)
        out_specs = [out_specs, pl.BlockSpec((1, per, cols), lambda i, j: (0, i * nj + j, 0))]
        out_shape = [out_shape, jax.ShapeDtypeStruct((1, rows, cols), BF16)]
    return pl.pallas_call(
        functools.partial(_ar_kernel, transposed=transposed, relu2=relu2, residual=res is not None,
                          side=side_cast is not None),
        grid=(m // tm, nj),
        in_specs=in_specs,
        out_specs=out_specs,
        out_shape=out_shape,
        compiler_params=_cparams("parallel", "arbitrary"),
        name=name,
    )(*args)


def _merge_kernel(oa_ref, ob_ref, oc_ref, wa_ref, wb_ref, wc_ref, ga_ref, gb_ref, gc_ref, o_ref):
    acc = _sigmoid(ga_ref[...]) * _cast_dot(oa_ref, wa_ref, False)
    acc += _sigmoid(gb_ref[...]) * _cast_dot(ob_ref, wb_ref, False)
    acc += _sigmoid(gc_ref[...]) * _cast_dot(oc_ref, wc_ref, False)
    o_ref[...] = acc.astype(o_ref.dtype)


def _branch_merge(o_a, o_b, o_c, w_branch, layer, zb, *, tm=1024, tn=256):
    m, kb = o_a.shape
    o_spec = pl.BlockSpec((tm, kb), lambda i, j: (i, 0))

    def w_spec(r):
        return pl.BlockSpec((1, kb, tn), lambda i, j: (layer, r, j))

    def g_spec(off):
        return pl.BlockSpec((tm, tn), lambda i, j: (i, off // tn + j))

    return pl.pallas_call(
        _merge_kernel,
        grid=(m // tm, D_MODEL // tn),
        in_specs=[o_spec, o_spec, o_spec, w_spec(0), w_spec(1), w_spec(2),
                  g_spec(ZB_MA), g_spec(ZB_MB), g_spec(ZB_MC)],
        out_specs=pl.BlockSpec((tm, tn), lambda i, j: (i, j)),
        out_shape=jax.ShapeDtypeStruct((m, D_MODEL), BF16),
        compiler_params=_cparams("parallel", "arbitrary"),
        name="branch_merge",
    )(o_a, o_b, o_c, w_branch, w_branch, w_branch, zb, zb, zb)


def _hgrn_chunk(qc, kk, b, v, g, nw, st):
    c = HG_CHUNK
    o = _dot_nt((qc * jnp.exp(b)).astype(BF16), st.astype(BF16))
    b_last = b[c - 1:c, :]
    kdec = kk * jnp.exp(b_last - b)
    v16 = v.astype(BF16)
    st_new = st * jnp.exp(b_last) + _dot_tn(v16, kdec.astype(BF16))
    row = lax.broadcasted_iota(jnp.int32, (HG_SUB, HG_DIM), 0)
    col = lax.broadcasted_iota(jnp.int32, (HG_SUB, c), 1)
    atts = [jnp.zeros((HG_SUB, c), F32)]
    for a in range(1, c // HG_SUB):
        lo = a * HG_SUB
        bref = b[lo:lo + 1, :]
        qn = qc[lo:lo + HG_SUB, :] * jnp.exp(b[lo:lo + HG_SUB, :] - bref)
        kn = kk * jnp.exp(jnp.minimum(bref - b, 0.0))
        atts.append(jnp.where(col < lo, _dot_nt(qn.astype(BF16), kn.astype(BF16)), 0.0))
    o = o + _dot(jnp.concatenate(atts, axis=0).astype(BF16), v16)
    parts = []
    for a in range(c // HG_SUB):
        lo = a * HG_SUB
        ba = b[lo:lo + HG_SUB, :]
        qa = qc[lo:lo + HG_SUB, :]
        acc = jnp.zeros((HG_SUB, HG_DIM), F32)
        for j in range(HG_SUB):
            jj = lo + j
            d = jnp.where(row >= j, ba - b[jj:jj + 1, :], NEG)
            w = jnp.sum(qa * kk[jj:jj + 1, :] * jnp.exp(d), axis=-1, keepdims=True)
            acc = acc + w * v[jj:jj + 1, :]
        parts.append(acc)
    o = o + jnp.concatenate(parts, axis=0)
    o = o * lax.rsqrt(jnp.mean(o * o, axis=-1, keepdims=True) + NORM_EPS) * nw * _silu(g)
    return o, st_new


def _hgrn_kernel(tab_ref, nw_ref, q_ref, f_ref, v_ref, g_ref, o_ref, st_ref, *, layer, rows):
    @pl.when(pl.program_id(2) == 0)
    def _():
        st_ref[...] = jnp.zeros_like(st_ref)

    tab = tab_ref[...]
    e = jnp.exp(tab - jnp.max(tab, axis=0, keepdims=True))
    p = e / jnp.sum(e, axis=0, keepdims=True)
    lb = jnp.sum(p[:layer + 1, :], axis=0, keepdims=True) - p[0:1, :]
    nw = nw_ref[...]
    r_i = lax.broadcasted_iota(jnp.int32, (HG_CHUNK, HG_CHUNK), 0)
    c_i = lax.broadcasted_iota(jnp.int32, (HG_CHUNK, HG_CHUNK), 1)
    tri = (r_i >= c_i).astype(F32)
    sts = [st_ref[hh] for hh in range(HG_HPB)]
    for ch in range(rows // HG_CHUNK):
        sl = slice(ch * HG_CHUNK, (ch + 1) * HG_CHUNK)
        qc = _silu(q_ref[sl, :]) * (HG_DIM ** -0.5)
        f = lb + (1.0 - lb) * _sigmoid(f_ref[sl, :])
        kk = 1.0 - f
        b = _dot(tri, jnp.log(f), precision=HIGHEST)
        outs = []
        for hh in range(HG_HPB):
            hs = slice(hh * HG_DIM, (hh + 1) * HG_DIM)
            o, sts[hh] = _hgrn_chunk(qc[:, hs], kk[:, hs], b[:, hs], v_ref[sl, hs], g_ref[sl, hs], nw, sts[hh])
            outs.append(o)
        o_ref[sl, :] = jnp.concatenate(outs, axis=1).astype(o_ref.dtype)
    for hh in range(HG_HPB):
        st_ref[hh] = sts[hh]


def _hgrn2(za, lb_table, norm_w, layer, batch, seq, rows=256):
    nrow = seq // rows
    width = HG_HPB * HG_DIM

    def col(off):
        return pl.BlockSpec((rows, width), lambda b, h, c: (b * nrow + c, off // width + h))

    return pl.pallas_call(
        functools.partial(_hgrn_kernel, layer=layer, rows=rows),
        grid=(batch, HG_HEADS // HG_HPB, nrow),
        in_specs=[pl.BlockSpec((DEPTH, width), lambda b, h, c: (0, h)),
                  pl.BlockSpec((1, HG_DIM), lambda b, h, c: (0, 0)),
                  col(OFF_AQ), col(OFF_AF), col(OFF_AI), col(OFF_AG)],
        out_specs=pl.BlockSpec((rows, width), lambda b, h, c: (b * nrow + c, h)),
        out_shape=jax.ShapeDtypeStruct((batch * seq, HG_WIDTH), BF16),
        scratch_shapes=[pltpu.VMEM((HG_HPB, HG_DIM, HG_DIM), F32)],
        compiler_params=_cparams("parallel", "parallel", "arbitrary"),
        name="hgrn2",
    )(lb_table, norm_w.reshape(1, HG_DIM), za, za, za, za)


def _ret_kernel(lg_ref, q_ref, k_ref, v_ref, g_ref, o_ref, s_ref, *, rows):
    @pl.when(pl.program_id(2) == 0)
    def _():
        s_ref[...] = jnp.zeros_like(s_ref)

    c = RET_CHUNK
    lg_v = lg_ref[0]
    lg_k = lg_v[:, :c]
    pos_r = lax.broadcasted_iota(jnp.int32, (c, RET_V), 0).astype(F32)
    query_decay = jnp.exp(lg_v * (pos_r + 1.0))
    rel = (lax.broadcasted_iota(jnp.int32, (c, c), 0) - lax.broadcasted_iota(jnp.int32, (c, c), 1)).astype(F32)
    intra_decay = jnp.where(rel >= 0, jnp.exp(lg_k * jnp.maximum(rel, 0.0)), 0.0)
    key_decay = jnp.exp(lg_k * (c - 1.0 - pos_r[:, :RET_QK]))
    chunk_decay = jnp.exp(lg_v * float(c))
    s = s_ref[...]
    for ch in range(rows // c):
        sl = slice(ch * c, (ch + 1) * c)
        q = q_ref[sl, :].astype(BF16)
        k = k_ref[sl, :] * (RET_QK ** -0.5)
        v = v_ref[sl, :].astype(BF16)
        inter = _dot(q, s.astype(BF16)) * query_decay
        scores = _dot_nt(q, k.astype(BF16)) * intra_decay
        o = inter + _dot(scores.astype(BF16), v)
        s = chunk_decay * s + _dot_tn((k * key_decay).astype(BF16), v)
        mu = jnp.mean(o, axis=-1, keepdims=True)
        oc = o - mu
        o = oc * lax.rsqrt(jnp.mean(oc * oc, axis=-1, keepdims=True) + NORM_EPS)
        o_ref[sl, :] = (o * _silu(g_ref[sl, :])).astype(o_ref.dtype)
    s_ref[...] = s


def _retention(zb, batch, seq, rows=2048):
    rows = min(rows, seq)
    nrow = seq // rows
    log_gamma = jnp.log(1.0 - jnp.exp2(-5.0 - jnp.arange(RET_HEADS, dtype=F32)))
    lg = jnp.broadcast_to(log_gamma[:, None, None], (RET_HEADS, 1, RET_V))
    return pl.pallas_call(
        functools.partial(_ret_kernel, rows=rows),
        grid=(batch, RET_HEADS, nrow),
        in_specs=[pl.BlockSpec((1, 1, RET_V), lambda b, h, c: (h, 0, 0)),
                  pl.BlockSpec((rows, RET_QK), lambda b, h, c: (b * nrow + c, ZB_CQ // RET_QK + h)),
                  pl.BlockSpec((rows, RET_QK), lambda b, h, c: (b * nrow + c, ZB_CK // RET_QK + h)),
                  pl.BlockSpec((rows, RET_V), lambda b, h, c: (b * nrow + c, ZB_CV // RET_V + h)),
                  pl.BlockSpec((rows, RET_V), lambda b, h, c: (b * nrow + c, ZB_CG // RET_V + h))],
        out_specs=pl.BlockSpec((rows, RET_V), lambda b, h, c: (b * nrow + c, h)),
        out_shape=jax.ShapeDtypeStruct((batch * seq, RET_V_WIDTH), BF16),
        scratch_shapes=[pltpu.VMEM((RET_QK, RET_V), F32)],
        compiler_params=_cparams("parallel", "parallel", "arbitrary"),
        name="retention",
    )(lg, zb, zb, zb, zb)


def _compress_kernel(kv_ref, pe_ref, w1_ref, w2_ref, o_ref, *, nc):
    d = NSA_DIM
    top = bot = None
    for l in range(CMP_STRIDE):
        x = kv_ref[pl.ds(l, nc, stride=CMP_STRIDE), :]
        pt = _dot((x + pe_ref[0, l:l + 1, :]).astype(BF16), w1_ref[0, l * d:(l + 1) * d, :])
        lb = CMP_STRIDE + l
        pb = _dot((x + pe_ref[0, lb:lb + 1, :]).astype(BF16), w1_ref[0, lb * d:(lb + 1) * d, :])
        top = pt if top is None else top + pt
        bot = pb if bot is None else bot + pb
    h = top + pltpu.roll(bot, nc - 1, 0)
    o_ref[0, 0] = _dot(_silu(h).astype(BF16), w2_ref[0])


def _compress(za, pe, w1, w2, batch, seq):
    nc = seq // CMP_STRIDE
    kv_blocks = NSA_KV_WIDTH // NSA_DIM
    return pl.pallas_call(
        functools.partial(_compress_kernel, nc=nc),
        grid=(2, batch, NSA_GROUPS),
        in_specs=[pl.BlockSpec((seq, NSA_DIM), lambda s, b, g: (b, OFF_BKC // NSA_DIM + s * kv_blocks + g)),
                  pl.BlockSpec((1, CMP_BLOCK, NSA_DIM), lambda s, b, g: (s, 0, 0)),
                  pl.BlockSpec((1, CMP_BLOCK * NSA_DIM, CMP_HIDDEN), lambda s, b, g: (s, 0, 0)),
                  pl.BlockSpec((1, CMP_HIDDEN, NSA_DIM), lambda s, b, g: (s, 0, 0))],
        out_specs=pl.BlockSpec((1, 1, nc, NSA_DIM), lambda s, b, g: (s, b * NSA_GROUPS + g, 0, 0)),
        out_shape=jax.ShapeDtypeStruct((2, batch * NSA_GROUPS, nc, NSA_DIM), F32),
        compiler_params=_cparams("parallel", "parallel", "parallel"),
        name="nsa_compress",
    )(za, pe, w1, w2)


def _slope_table():
    slopes = np.exp2(-8.0 * np.arange(1, NSA_HEADS + 1, dtype=np.float64) / NSA_HEADS).astype(np.float32)
    tab = np.zeros((NSA_GROUPS, 8, 512), np.float32)
    tab[:, :NSA_REP, :] = slopes.reshape(NSA_GROUPS, NSA_REP)[:, :, None]
    return jnp.asarray(tab)


def _cmp_body(slope_ref, wmap_ref, q_ref, kc_ref, vc_ref, gate_ref, o_ref, sel_ref, used_ref, score_ref,
              t0, ncols, nblk, *, tq, nc, nslc):
    kc = kc_ref[0, 0, :ncols, :]
    vc = vc_ref[0, 0, :ncols, :].astype(BF16)
    gates = _sigmoid(gate_ref[0, 0])
    t_i = t0 + lax.broadcasted_iota(jnp.int32, (tq, ncols), 0)
    n_i = lax.broadcasted_iota(jnp.int32, (tq, ncols), 1)
    dist = t_i - (n_i * CMP_STRIDE + CMP_BLOCK - 1)
    valid = (dist >= 0) & (n_i < nc - 1)
    distf = dist.astype(F32)
    imp = jnp.zeros((tq, ncols), F32)
    q_all = jnp.concatenate([q_ref[:, r * NSA_DIM:(r + 1) * NSA_DIM] for r in range(NSA_REP)], axis=0)
    s_all = _dot_nt(q_all, kc, precision=HIGHEST)
    ps = []
    for r in range(NSA_REP):
        slope = slope_ref[0, r:r + 1, :ncols]
        s = s_all[r * tq:(r + 1) * tq, :] * (NSA_DIM ** -0.5) - slope * distf
        s = jnp.where(valid, s, NEG)
        m = jnp.max(s, axis=-1, keepdims=True)
        e = jnp.where(valid, jnp.exp(s - m), 0.0)
        den = jnp.sum(e, axis=-1, keepdims=True)
        p = e / jnp.where(den > 0, den, 1.0)
        ps.append(p.astype(BF16))
        imp = imp + p
    o_all = _dot(jnp.concatenate(ps, axis=0), vc)
    for r in range(NSA_REP):
        o_ref[:, r * NSA_DIM:(r + 1) * NSA_DIM] = gates[:, 3 * r:3 * r + 1] * o_all[r * tq:(r + 1) * tq, :]
    imp_t = _dot_nt(wmap_ref[:nblk, :ncols], imp, precision=HIGHEST)
    blk = lax.broadcasted_iota(jnp.int32, (nblk, tq), 0)
    cur = jnp.right_shift(t0 + lax.broadcasted_iota(jnp.int32, (nblk, tq), 1), SLC_SHIFT)
    forced = (blk == 0) | (blk == cur) | (blk == cur - 1)
    score = jnp.where(blk > cur, -jnp.inf, jnp.where(forced, jnp.inf, imp_t))
    score_ref[:nblk, :] = score
    sub = 8
    groups = [score_ref[g0:g0 + sub, :] for g0 in range(0, nblk, sub)]
    ranks = [jnp.zeros((sub, tq), F32) for _ in groups]
    row8 = lax.broadcasted_iota(jnp.int32, (sub, tq), 0)
    for s_i in range(nblk):
        other = jnp.broadcast_to(score_ref[s_i:s_i + 1, :], (sub, tq))
        for gi, sc in enumerate(groups):
            g0 = gi * sub
            if g0 > s_i:
                beats = other >= sc
            elif g0 + sub - 1 <= s_i:
                beats = other > sc
            else:
                beats = (other > sc) | ((other == sc) & (row8 > s_i - g0))
            ranks[gi] = ranks[gi] + jnp.where(beats, 1.0, 0.0)
    rank = jnp.concatenate(ranks, axis=0)
    sel = ((rank < min(SLC_TOPK, nslc)) & (blk <= cur)).astype(F32)
    if nblk < nslc:
        sel = jnp.concatenate([sel, jnp.zeros((nslc - nblk, tq), F32)], axis=0)
    sel_ref[0, 0] = sel
    for a in range(tq // ATT_T):
        used_ref[0, 0, 0, :, a:a + 1] = jnp.max(sel[:, a * ATT_T:(a + 1) * ATT_T], axis=1, keepdims=True)


def _cmp_kernel(*refs, tq, nc, nslc):
    t0 = pl.program_id(2) * tq
    early = t0 + tq <= (nslc // 2) * SLC_BLOCK

    @pl.when(early)
    def _():
        _cmp_body(*refs, t0, nc // 2, nslc // 2, tq=tq, nc=nc, nslc=nslc)

    @pl.when(jnp.logical_not(early))
    def _():
        _cmp_body(*refs, t0, nc, nslc, tq=tq, nc=nc, nslc=nslc)


def _cmp_attention(za, kvc, gates, batch, seq, tq=512):
    nc = seq // CMP_STRIDE
    nslc = seq // SLC_BLOCK
    nq = seq // tq
    c_start = np.arange(nc) * CMP_STRIDE
    s_start = np.arange(nslc) * SLC_BLOCK
    overlap = np.clip(np.minimum(c_start[:, None] + CMP_BLOCK, s_start[None, :] + SLC_BLOCK)
                      - np.maximum(c_start[:, None], s_start[None, :]), 0, None)
    wmap_t = (overlap.astype(np.float32) / CMP_STRIDE).T.copy()
    wmap_t[:, nc - 1] = 0.0
    sub = tq // ATT_T
    return pl.pallas_call(
        functools.partial(_cmp_kernel, tq=tq, nc=nc, nslc=nslc),
        grid=(batch, NSA_GROUPS, nq),
        in_specs=[pl.BlockSpec((1, 8, 512), lambda b, g, i: (g, 0, 0)),
                  pl.BlockSpec((nslc, nc), lambda b, g, i: (0, 0)),
                  pl.BlockSpec((tq, NSA_GW), lambda b, g, i: (b * nq + i, OFF_BQ // NSA_GW + g)),
                  pl.BlockSpec((1, 1, nc, NSA_DIM), lambda b, g, i: (0, b * NSA_GROUPS + g, 0, 0)),
                  pl.BlockSpec((1, 1, nc, NSA_DIM), lambda b, g, i: (1, b * NSA_GROUPS + g, 0, 0)),
                  pl.BlockSpec((1, 1, tq, 3 * NSA_REP), lambda b, g, i: (b, g, i, 0))],
        out_specs=[pl.BlockSpec((tq, NSA_GW), lambda b, g, i: (b * nq + i, g)),
                   pl.BlockSpec((1, 1, nslc, tq), lambda b, g, i: (b, g, 0, i)),
                   pl.BlockSpec((1, 1, 1, nslc, sub), lambda b, g, i: (b, g, i, 0, 0))],
        out_shape=[jax.ShapeDtypeStruct((batch * seq, NSA_WIDTH), F32),
                   jax.ShapeDtypeStruct((batch, NSA_GROUPS, nslc, seq), F32),
                   jax.ShapeDtypeStruct((batch, NSA_GROUPS, nq, nslc, sub), F32)],
        scratch_shapes=[pltpu.VMEM((nslc, tq), F32)],
        compiler_params=_cparams("parallel", "parallel", "arbitrary"),
        name="nsa_cmp_select",
    )(_slope_table(), jnp.asarray(wmap_t), za, kvc, kvc, gates)


ATT_SUPER_SHIFT = 3
ATT_SUPER = 1 << ATT_SUPER_SHIFT
ALIBI_FEATS = 6


def _alibi_query_features():
    slopes = np.exp2(-8.0 * np.arange(1, NSA_HEADS + 1, dtype=np.float64) / NSA_HEADS).astype(np.float32)

    def top_bits(x):
        return (x.view(np.uint32) & np.uint32(0xFFFF0000)).view(np.float32)

    s1 = top_bits(slopes)
    r1 = slopes - s1
    s2 = top_bits(r1)
    s3 = r1 - s2
    rows = np.stack([-64.0 * s1, -64.0 * s2, -64.0 * s3, -s1, -s2, -s3]).astype(np.float32)
    feat = np.zeros((NSA_GROUPS, NSA_DIM, NSA_REP, ATT_T), np.float32)
    feat[:, :ALIBI_FEATS] = rows.reshape(ALIBI_FEATS, NSA_GROUPS, NSA_REP).transpose(1, 0, 2)[:, :, :, None]
    return jnp.asarray(feat.reshape(NSA_GROUPS, NSA_DIM, NSA_GW), dtype=BF16)


def _attend(k_ref, v_ref, chunks, valids, qi, qt_aug, kf_static, kf_ind, m_ref, l_ref, acc_ref):
    t = ATT_T
    ks, vs, mbs = [], [], []
    for c, valid in zip(chunks, valids):
        k0 = pl.multiple_of(c * t, t)
        kfeat = (kf_static + (2 * (qi - c)).astype(F32) * kf_ind).astype(BF16)
        ks.append(jnp.concatenate([k_ref[pl.ds(k0, t), :].astype(BF16), kfeat], axis=1))
        vs.append(v_ref[pl.ds(k0, t), :].astype(BF16))
        mbs.append(jnp.where(valid, 0.0, NEG))
    st = _dot(jnp.concatenate(ks, axis=0), qt_aug)
    mbias = jnp.concatenate(mbs, axis=0)
    ps, alphas = [], []
    for r in range(NSA_REP):
        cs = slice(r * t, (r + 1) * t)
        s = st[:, cs] + mbias
        m_old = m_ref[:, cs]
        m_new = jnp.maximum(m_old, jnp.max(s, axis=0, keepdims=True))
        alphas.append(jnp.exp(m_old - m_new))
        p = jnp.exp(s - m_new)
        l_ref[:, cs] = alphas[-1] * l_ref[:, cs] + jnp.sum(p, axis=0, keepdims=True)
        m_ref[:, cs] = m_new
        ps.append(p.astype(BF16))
    acc_ref[...] = (jnp.concatenate(alphas, axis=1) * acc_ref[...]
                    + _dot_tn(jnp.concatenate(vs, axis=0), jnp.concatenate(ps, axis=1)))


def _slc_win_kernel(ids_ref, cnt_ref, qfeat_ref, q_ref, ks_ref, vs_ref, kw_ref, vw_ref, sel_ref, gate_ref,
                    ocmp_ref, o_ref, m_ref, l_ref, acc_ref, *, nq):
    t = ATT_T
    b, g, qi = pl.program_id(0), pl.program_id(1), pl.program_id(2)
    t0 = qi * t
    qt = jnp.concatenate([q_ref[:, r * NSA_DIM:(r + 1) * NSA_DIM].T for r in range(NSA_REP)], axis=1)
    qt_aug = jnp.concatenate([(qt * (NSA_DIM ** -0.5)).astype(BF16), qfeat_ref[0]], axis=0)
    key_i = lax.broadcasted_iota(jnp.int32, (t, t), 0)
    qry_i = lax.broadcasted_iota(jnp.int32, (t, t), 1)
    back = (t - 1) - key_i
    kf_ind = (qry_i < 3).astype(F32)
    kf_static = jnp.where(qry_i < 3, jnp.right_shift(back, SLC_SHIFT),
                          jnp.where(qry_i < ALIBI_FEATS, jnp.bitwise_and(back, SLC_BLOCK - 1), 0)).astype(F32)

    def reset():
        m_ref[...] = jnp.full_like(m_ref, NEG)
        l_ref[...] = jnp.zeros_like(l_ref)
        acc_ref[...] = jnp.zeros_like(acc_ref)

    reset()
    tile = (b * NSA_GROUPS + g) * nq + qi
    cnt = cnt_ref[tile]
    half = SLC_BLOCK
    never = jnp.int32(2 ** 30)

    def slc_step(base, width):
        chunks, valids = [], []
        for u in range(width):
            e = base + u
            c = ids_ref[tile * nq + jnp.maximum(jnp.minimum(e, cnt - 1), 0)]
            lo = jnp.broadcast_to(sel_ref[0, 0, pl.ds(2 * c, 1), :], (half, t))
            hi = jnp.broadcast_to(sel_ref[0, 0, pl.ds(2 * c + 1, 1), :], (half, t))
            picked = jnp.concatenate([lo, hi], axis=0) > 0.5
            dist = (t0 + qry_i) - (c * t + key_i)
            chunks.append(c)
            valids.append(picked & (dist >= jnp.where(e < cnt, 0, never)))
        _attend(ks_ref, vs_ref, chunks, valids, qi, qt_aug, kf_static, kf_ind, m_ref, l_ref, acc_ref)

    def slc_body(s, carry):
        slc_step(s * ATT_SUPER, ATT_SUPER)
        return carry

    n_full = jnp.right_shift(cnt, ATT_SUPER_SHIFT)
    rem = cnt - n_full * ATT_SUPER
    lax.fori_loop(0, n_full, slc_body, 0)

    @pl.when((rem > 0) & (rem <= ATT_SUPER // 2))
    def _():
        slc_step(n_full * ATT_SUPER, ATT_SUPER // 2)

    @pl.when(rem > ATT_SUPER // 2)
    def _():
        slc_step(n_full * ATT_SUPER, ATT_SUPER)

    o_slc = acc_ref[...] * (1.0 / l_ref[...])

    reset()
    chunks, valids = [], []
    for i in range(WIN_SIZE // t + 1):
        c = jnp.maximum(qi - i, 0)
        dist = (t0 + qry_i) - (c * t + key_i)
        chunks.append(c)
        valids.append((dist >= jnp.where(qi - i >= 0, 0, never)) & (dist < WIN_SIZE))
    _attend(kw_ref, vw_ref, chunks, valids, qi, qt_aug, kf_static, kf_ind, m_ref, l_ref, acc_ref)
    o_win = acc_ref[...] * (1.0 / l_ref[...])

    gates = _sigmoid(gate_ref[0, 0])
    for r in range(NSA_REP):
        cs = slice(r * t, (r + 1) * t)
        o_t = gates[3 * r + 1:3 * r + 2, :] * o_slc[:, cs] + gates[3 * r + 2:3 * r + 3, :] * o_win[:, cs]
        o_ref[:, cs] = (o_t.T + ocmp_ref[:, cs]).astype(o_ref.dtype)


def _slc_win_attention(ids, cnt, za, sel_t, gates_t, o_cmp, batch, seq):
    t = ATT_T
    nslc = seq // SLC_BLOCK
    nq = seq // t

    def kv(off):
        return pl.BlockSpec((seq, NSA_DIM), lambda b, g, i, *_: (b, off // NSA_DIM + g))

    grid_spec = pltpu.PrefetchScalarGridSpec(
        num_scalar_prefetch=2,
        grid=(batch, NSA_GROUPS, nq),
        in_specs=[pl.BlockSpec((1, NSA_DIM, NSA_GW), lambda b, g, i, *_: (g, 0, 0)),
                  pl.BlockSpec((t, NSA_GW), lambda b, g, i, *_: (b * nq + i, OFF_BQ // NSA_GW + g)),
                  kv(OFF_BKS), kv(OFF_BVS), kv(OFF_BKW), kv(OFF_BVW),
                  pl.BlockSpec((1, 1, nslc, t), lambda b, g, i, *_: (b, g, 0, i)),
                  pl.BlockSpec((1, 1, 3 * NSA_REP, t), lambda b, g, i, *_: (b, g, 0, i)),
                  pl.BlockSpec((t, NSA_GW), lambda b, g, i, *_: (b * nq + i, g))],
        out_specs=pl.BlockSpec((t, NSA_GW), lambda b, g, i, *_: (b * nq + i, g)),
        scratch_shapes=[pltpu.VMEM((1, NSA_GW), F32), pltpu.VMEM((1, NSA_GW), F32),
                        pltpu.VMEM((NSA_DIM, NSA_GW), F32)])
    return pl.pallas_call(
        functools.partial(_slc_win_kernel, nq=nq),
        grid_spec=grid_spec,
        out_shape=jax.ShapeDtypeStruct((batch * seq, NSA_WIDTH), BF16),
        compiler_params=_cparams("parallel", "parallel", "arbitrary"),
        name="nsa_slc_win",
    )(ids, cnt, _alibi_query_features(), za, za, za, za, za, sel_t, gates_t, o_cmp)


def _nsa(za, pe, w1, w2, batch, seq):
    nq = seq // ATT_T
    kvc = _compress(za, pe, w1, w2, batch, seq)
    gates = (za[:, OFF_BGATE:OFF_BGATE + GATE_COLS].reshape(batch, seq, NSA_GROUPS, 3 * NSA_REP)
             .transpose(0, 2, 1, 3))
    o_cmp, sel_t, used = _cmp_attention(za, kvc, gates, batch, seq)
    nslc = seq // SLC_BLOCK
    used = used.transpose(0, 1, 2, 4, 3).reshape(batch * NSA_GROUPS * nq, nslc // 2, 2)
    unused = (jnp.max(used, axis=-1) < 0.5).astype(jnp.int32)
    ids = jnp.argsort(unused, axis=-1, stable=True).astype(jnp.int32).reshape(-1)
    cnt = (nslc // 2 - jnp.sum(unused, axis=-1)).astype(jnp.int32)
    return _slc_win_attention(ids, cnt, za, sel_t, gates.transpose(0, 1, 3, 2), o_cmp, batch, seq)


def _layer(x, layer, norm1_w, w_in_t, lb_table, hgrn_norm_w, pe, w1, w2, w_branch, w_out, norm2_w, w_ff1, w_ff2,
           batch, seq):
    h = _rmsnorm(x, norm1_w, BF16)
    wide = dict(tn=PROJ_TN, a_single_buffer=True)
    za = _ar_matmul(h, w_in_t, layer, F32, n_cols=ZA_WIDTH, transposed=True, name="proj_in_a", **wide)
    zb = _ar_matmul(h, w_in_t, layer, F32, n_cols=ZB_WIDTH, col0=OFF_BGATE + GATE_COLS, transposed=True,
                    name="proj_in_b", **wide)
    o_a = _hgrn2(za, lb_table, hgrn_norm_w, layer, batch, seq)
    o_b = _nsa(za, pe, w1, w2, batch, seq)
    o_c = _retention(zb, batch, seq)
    merged = _branch_merge(o_a, o_b, o_c, w_branch, layer, zb)
    x = _ar_matmul(merged, w_out, layer, F32, n_cols=D_MODEL, res=x, name="proj_out", **wide)
    h = _rmsnorm(x, norm2_w, BF16)
    u, w_ff2_bf = _ar_matmul(h, w_ff1, layer, BF16, n_cols=D_FF, relu2=True, side_cast=w_ff2, name="ffn_up",
                             **wide)
    return _ar_matmul(u, w_ff2_bf, 0, F32, n_cols=D_MODEL, res=x, tm=512, name="ffn_down")


def kernel(x, norm1_w, w_in, hgrn_lb_table, hgrn_norm_w, cmp_pe_k, cmp_pe_v, cmp_w1_k, cmp_w1_v, cmp_w2_k,
           cmp_w2_v, w_branch, w_out, norm2_w, w_ff1, w_ff2, final_norm_w):
    batch, seq, d = x.shape
    xf = x.reshape(batch * seq, d)
    w_in_t = jnp.swapaxes(w_in, 1, 2)
    for l in range(DEPTH):
        pe = jnp.stack([cmp_pe_k[l], cmp_pe_v[l]])
        w1 = jnp.stack([cmp_w1_k[l], cmp_w1_v[l]]).astype(BF16)
        w2 = jnp.stack([cmp_w2_k[l], cmp_w2_v[l]]).astype(BF16)
        xf = _layer(xf, l, norm1_w[l], w_in_t, hgrn_lb_table, hgrn_norm_w[l], pe, w1, w2, w_branch, w_out,
                    norm2_w[l], w_ff1, w_ff2, batch, seq)
    return _rmsnorm(xf, final_norm_w, F32).reshape(batch, seq, d)
```
